```python
import math
import jax, jax.numpy as jnp
from jax import lax
import numpy as np

D_MODEL = 1024
BATCH = 4
SEQ = 4096
DEPTH = 4
DEC_BATCH = 32
DEC_SEQ = 4
PAST_LEN = 8192
PAGE_SIZE = 128

N_MIXERS = 2
N_SSM_LAYERS = (DEPTH + 1) // 2
N_ATTN_LAYERS = DEPTH // 2
SSM_GROUP = 16
SSM_GROUPS = D_MODEL // SSM_GROUP
SSM_STATE = 64
SWA_GROUPS = ((128, 1), (512, 4), (2048, 16))
N_SWA_GROUPS = len(SWA_GROUPS)
SWA_HEADS = 8
SWA_HEAD_DIM = 64
SWA_WIDTH = SWA_HEADS * SWA_HEAD_DIM
Q_BLOCK = 128
N_MEM = 256
MEM_HEADS = 4
MEM_HEAD_DIM = D_MODEL // MEM_HEADS
D_FF = -(-8 * D_MODEL // (3 * 256)) * 256
RMS_EPS = 1e-6
LOG_STEP_MIN = math.log(1e-3)
LOG_STEP_MAX = math.log(1e-1)

kernel_name = 'hybrid_s5_dilated_swa_memxattn_step'


def rms_norm(x, g):
    xf = x.astype(jnp.float32)
    y = xf * lax.rsqrt(jnp.mean(xf * xf, -1, keepdims=True) + RMS_EPS) * g.astype(jnp.float32)
    return y.astype(x.dtype)


def ssm_mixer(u, h0_re, h0_im, lam_re, lam_im, b_re, b_im, c_re, c_im, d_skip, log_step, w_glu, b_glu):
    f32 = jnp.float32
    bt, L, _ = u.shape
    uf = u.astype(f32)
    ug = uf.reshape(bt, L, SSM_GROUPS, SSM_GROUP)
    lr, li = lam_re.astype(f32), lam_im.astype(f32)
    step = jnp.exp(log_step.astype(f32))[:, None]
    mag = jnp.exp(lr * step)
    ar, ai = mag * jnp.cos(li * step), mag * jnp.sin(li * step)
    den = lr * lr + li * li
    fr = ((ar - 1.0) * lr + ai * li) / den
    fi = (ai * lr - (ar - 1.0) * li) / den
    br, bi = b_re.astype(f32), b_im.astype(f32)
    bbr = fr[..., None] * br - fi[..., None] * bi
    bbi = fr[..., None] * bi + fi[..., None] * br
    bur = jnp.einsum('blgc,gpc->blgp', ug, bbr)
    bui = jnp.einsum('blgc,gpc->blgp', ug, bbi)
    a_r = jnp.broadcast_to(ar, (L,) + ar.shape)
    a_i = jnp.broadcast_to(ai, (L,) + ai.shape)

    def combine(e1, e2):
        a1r, a1i, b1r, b1i = e1
        a2r, a2i, b2r, b2i = e2
        return (a2r * a1r - a2i * a1i, a2r * a1i + a2i * a1r,
                a2r * b1r - a2i * b1i + b2r, a2r * b1i + a2i * b1r + b2i)

    def scan_one(xr, xi, yr, yi):
        return lax.associative_scan(combine, (xr, xi, yr, yi), axis=0)

    pr, pi_, hr, hi = jax.vmap(scan_one, in_axes=(None, None, 0, 0))(a_r, a_i, bur, bui)
    h0r = h0_re.astype(f32)[:, None]
    h0i = h0_im.astype(f32)[:, None]
    hr = hr + pr * h0r - pi_ * h0i
    hi = hi + pr * h0i + pi_ * h0r
    yc = (jnp.einsum('blgp,gcp->blgc', hr, c_re.astype(f32))
          - jnp.einsum('blgp,gcp->blgc', hi, c_im.astype(f32)))
    y = yc.reshape(bt, L, D_MODEL) + d_skip.astype(f32) * uf
    g = jax.nn.gelu(y)
    out = g * jax.nn.sigmoid(g @ w_glu.astype(f32) + b_glu.astype(f32))
    return out.astype(u.dtype), hr[:, -1], hi[:, -1]


def dilated_group_attn(q, k_all, v_all, offset, dil, n_keys):
    bt, L, nh, hd = q.shape
    blk = Q_BLOCK if L % Q_BLOCK == 0 else L
    nb = L // blk
    tap = jnp.arange(n_keys) * dil
    scale = hd ** -0.5

    def block(b):
        qb = lax.dynamic_slice_in_dim(q, b * blk, blk, axis=1)
        pos = offset + b * blk + jnp.arange(blk)
        idx = pos[:, None] - tap[None, :]
        valid = idx >= 0
        idx = jnp.maximum(idx, 0)
        kb = jnp.take(k_all, idx, axis=1)
        vb = jnp.take(v_all, idx, axis=1)
        s = jnp.einsum('bqhd,bqjhd->bqhj', qb, kb).astype(jnp.float32) * scale
        s = jnp.where(valid[None, :, None, :], s, -jnp.inf)
        m = jnp.max(s, -1, keepdims=True)
        p = jnp.exp(s - m)
        den = jnp.sum(p, -1, keepdims=True)
        o = jnp.einsum('bqhj,bqjhd->bqhd', (p / den).astype(vb.dtype), vb)
        lse = (m + jnp.log(den))[..., 0]
        return o, lse

    if nb == 1:
        return block(0)
    o, lse = lax.map(block, jnp.arange(nb))
    o = jnp.moveaxis(o, 0, 1).reshape(bt, L, nh, hd)
    lse = jnp.moveaxis(lse, 0, 1).reshape(bt, L, nh)
    return o, lse


def attn_mixer(h, past_k, past_v, w_qkv, q_g, k_g, w_o):
    bt, L, _ = h.shape
    qkv = (h @ w_qkv).reshape(bt, L, 3, N_SWA_GROUPS, SWA_HEADS, SWA_HEAD_DIM)
    q = rms_norm(qkv[:, :, 0], q_g[:, None, :])
    k = rms_norm(qkv[:, :, 1], k_g[:, None, :])
    v = qkv[:, :, 2]
    outs, lses, new_k, new_v = [], [], [], []
    for g, (win, dil) in enumerate(SWA_GROUPS):
        kg, vg = k[:, :, g], v[:, :, g]
        if past_k is None:
            k_all, v_all, off = kg, vg, 0
            keep = min(win, L)
            new_k.append(kg[:, L - keep:])
            new_v.append(vg[:, L - keep:])
        else:
            k_all = jnp.concatenate([past_k[g].astype(kg.dtype), kg], axis=1)
            v_all = jnp.concatenate([past_v[g].astype(vg.dtype), vg], axis=1)
            off = past_k[g].shape[1]
            new_k.append(kg)
            new_v.append(vg)
        o, lse = dilated_group_attn(q[:, :, g], k_all, v_all, off, dil, win // dil + 1)
        outs.append(o)
        lses.append(lse)
    w = jax.nn.softmax(jnp.stack(lses, 0), axis=0)
    o = jnp.einsum('gblh,gblhd->blhd', w, jnp.stack(outs, 0).astype(jnp.float32))
    y = o.astype(h.dtype).reshape(bt, L, SWA_WIDTH) @ w_o
    return y, new_k, new_v


def mem_kv(mem, g_in, w_kv, k_g):
    bt = mem.shape[0]
    kv = (rms_norm(mem, g_in) @ w_kv).reshape(bt, N_MEM, 2, MEM_HEADS, MEM_HEAD_DIM)
    return rms_norm(kv[:, :, 0], k_g), kv[:, :, 1]


def mem_attn(h, mk, mv, w_q, q_g, w_o):
    bt, L, _ = h.shape
    q = rms_norm((h @ w_q).reshape(bt, L, MEM_HEADS, MEM_HEAD_DIM), q_g)
    s = jnp.einsum('blhd,bmhd->bhlm', q, mk.astype(q.dtype)).astype(jnp.float32) * MEM_HEAD_DIM ** -0.5
    p = jax.nn.softmax(s, axis=-1)
    o = jnp.einsum('bhlm,bmhd->blhd', p.astype(mv.dtype), mv)
    return o.astype(h.dtype).reshape(bt, L, MEM_HEADS * MEM_HEAD_DIM) @ w_o


def swiglu(h, w_in, w_out):
    gu = h @ w_in
    return (jax.nn.silu(gu[..., :D_FF]) * gu[..., D_FF:]) @ w_out


def trunk(x, mem_k, mem_v, h0_re, h0_im, swa_k_cache, swa_v_cache, p):
    ssm_re, ssm_im = [], []
    swa_k = [[] for _ in SWA_GROUPS]
    swa_v = [[] for _ in SWA_GROUPS]
    for i in range(DEPTH):
        j = i // N_MIXERS
        h = rms_norm(x, p['norm_mix_g'][i])
        if i % N_MIXERS == 0:
            y, hr, hi = ssm_mixer(h, h0_re[j], h0_im[j], p['ssm_lambda_re'][j], p['ssm_lambda_im'][j],
                                  p['ssm_b_re'][j], p['ssm_b_im'][j], p['ssm_c_re'][j], p['ssm_c_im'][j],
                                  p['ssm_d'][j], p['ssm_log_step'][j], p['ssm_w_glu'][j], p['ssm_b_glu'][j])
            ssm_re.append(hr)
            ssm_im.append(hi)
        else:
            pk = None if swa_k_cache is None else [c[j] for c in swa_k_cache]
            pv = None if swa_v_cache is None else [c[j] for c in swa_v_cache]
            y, nk, nv = attn_mixer(h, pk, pv, p['attn_w_qkv'][j], p['attn_q_norm_g'][j],
                                   p['attn_k_norm_g'][j], p['attn_w_o'][j])
            for g in range(N_SWA_GROUPS):
                swa_k[g].append(nk[g])
                swa_v[g].append(nv[g])
        x = x + y.astype(x.dtype)
        x = x + mem_attn(rms_norm(x, p['norm_mem_g'][i]), mem_k[i], mem_v[i], p['mem_w_q'][i],
                         p['mem_q_norm_g'][i], p['mem_w_o'][i])
        x = x + swiglu(rms_norm(x, p['norm_ffn_g'][i]), p['ffn_w_in'][i], p['ffn_w_out'][i])
    return (x, jnp.stack(ssm_re, 0), jnp.stack(ssm_im, 0),
            [jnp.stack(a, 0) for a in swa_k], [jnp.stack(a, 0) for a in swa_v])


def setup_inputs(seed: int = 0) -> dict:
    key = jax.random.key(seed)
    ks = iter(jax.random.split(key, 48))
    f32 = jnp.float32

    def nrm(shape, scale):
        return jax.random.normal(next(ks), shape, f32) * scale

    d = {}
    d['x_prompt'] = nrm((BATCH, SEQ, D_MODEL), 1.0)
    d['x_sample'] = nrm((DEC_BATCH, DEC_SEQ, D_MODEL), 1.0)
    d['mem_prompt'] = nrm((BATCH, N_MEM, D_MODEL), 1.0)
    d['state_ssm_re'] = nrm((N_SSM_LAYERS, DEC_BATCH, SSM_GROUPS, SSM_STATE), 0.2)
    d['state_ssm_im'] = nrm((N_SSM_LAYERS, DEC_BATCH, SSM_GROUPS, SSM_STATE), 0.2)
    for g, (win, dil) in enumerate(SWA_GROUPS):
        rows = min(win, PAST_LEN)
        d['cache_swa%d_k' % g] = nrm((N_ATTN_LAYERS, DEC_BATCH, rows, SWA_HEADS, SWA_HEAD_DIM), 1.0)
        d['cache_swa%d_v' % g] = nrm((N_ATTN_LAYERS, DEC_BATCH, rows, SWA_HEADS, SWA_HEAD_DIM), 1.0)
    d['cache_mem_k'] = nrm((DEPTH, DEC_BATCH, N_MEM, MEM_HEADS, MEM_HEAD_DIM), 1.0)
    d['cache_mem_v'] = nrm((DEPTH, DEC_BATCH, N_MEM, MEM_HEADS, MEM_HEAD_DIM), 1.0)
    d['norm_mix_g'] = 1.0 + nrm((DEPTH, D_MODEL), 0.02)
    d['norm_mem_g'] = 1.0 + nrm((DEPTH, D_MODEL), 0.02)
    d['norm_memin_g'] = 1.0 + nrm((DEPTH, D_MODEL), 0.02)
    d['norm_ffn_g'] = 1.0 + nrm((DEPTH, D_MODEL), 0.02)
    sshape = (N_SSM_LAYERS, SSM_GROUPS, SSM_STATE)
    d['ssm_lambda_re'] = -0.5 + nrm(sshape, 0.01)
    d['ssm_lambda_im'] = jnp.pi * jnp.arange(SSM_STATE, dtype=f32) + nrm(sshape, 0.01)
    d['ssm_b_re'] = nrm(sshape + (SSM_GROUP,), (2.0 * SSM_GROUP) ** -0.5)
    d['ssm_b_im'] = nrm(sshape + (SSM_GROUP,), (2.0 * SSM_GROUP) ** -0.5)
    d['ssm_c_re'] = nrm((N_SSM_LAYERS, SSM_GROUPS, SSM_GROUP, SSM_STATE), 4.0 * (2.0 * SSM_STATE) ** -0.5)
    d['ssm_c_im'] = nrm((N_SSM_LAYERS, SSM_GROUPS, SSM_GROUP, SSM_STATE), 4.0 * (2.0 * SSM_STATE) ** -0.5)
    d['ssm_d'] = nrm((N_SSM_LAYERS, D_MODEL), 0.5)
    d['ssm_log_step'] = jax.random.uniform(next(ks), (N_SSM_LAYERS, SSM_GROUPS), f32,
                                           LOG_STEP_MIN, LOG_STEP_MAX)
    d['ssm_w_glu'] = nrm((N_SSM_LAYERS, D_MODEL, D_MODEL), D_MODEL ** -0.5)
    d['ssm_b_glu'] = nrm((N_SSM_LAYERS, D_MODEL), 0.01)
    d['attn_w_qkv'] = nrm((N_ATTN_LAYERS, D_MODEL, 3 * N_SWA_GROUPS * SWA_WIDTH), D_MODEL ** -0.5)
    d['attn_q_norm_g'] = 1.0 + nrm((N_ATTN_LAYERS, N_SWA_GROUPS, SWA_HEAD_DIM), 0.02)
    d['attn_k_norm_g'] = 1.0 + nrm((N_ATTN_LAYERS, N_SWA_GROUPS, SWA_HEAD_DIM), 0.02)
    d['attn_w_o'] = nrm((N_ATTN_LAYERS, SWA_WIDTH, D_MODEL), SWA_WIDTH ** -0.5)
    d['mem_w_q'] = nrm((DEPTH, D_MODEL, MEM_HEADS * MEM_HEAD_DIM), D_MODEL ** -0.5)
    d['mem_w_kv'] = nrm((DEPTH, D_MODEL, 2 * MEM_HEADS * MEM_HEAD_DIM), D_MODEL ** -0.5)
    d['mem_q_norm_g'] = 1.0 + nrm((DEPTH, MEM_HEAD_DIM), 0.02)
    d['mem_k_norm_g'] = 1.0 + nrm((DEPTH, MEM_HEAD_DIM), 0.02)
    d['mem_w_o'] = nrm((DEPTH, MEM_HEADS * MEM_HEAD_DIM, D_MODEL), (MEM_HEADS * MEM_HEAD_DIM) ** -0.5)
    d['ffn_w_in'] = nrm((DEPTH, D_MODEL, 2 * D_FF), D_MODEL ** -0.5)
    d['ffn_w_out'] = nrm((DEPTH, D_FF, D_MODEL), D_FF ** -0.5)
    return d


def reference(x_prompt, x_sample, mem_prompt, state_ssm_re, state_ssm_im,
              cache_swa0_k, cache_swa0_v, cache_swa1_k, cache_swa1_v, cache_swa2_k, cache_swa2_v,
              cache_mem_k, cache_mem_v,
              norm_mix_g, norm_mem_g, norm_memin_g, norm_ffn_g,
              ssm_lambda_re, ssm_lambda_im, ssm_b_re, ssm_b_im, ssm_c_re, ssm_c_im, ssm_d,
              ssm_log_step, ssm_w_glu, ssm_b_glu,
              attn_w_qkv, attn_q_norm_g, attn_k_norm_g, attn_w_o,
              mem_w_q, mem_w_kv, mem_q_norm_g, mem_k_norm_g, mem_w_o,
              ffn_w_in, ffn_w_out):
    p = dict(norm_mix_g=norm_mix_g, norm_mem_g=norm_mem_g, norm_ffn_g=norm_ffn_g,
             ssm_lambda_re=ssm_lambda_re, ssm_lambda_im=ssm_lambda_im, ssm_b_re=ssm_b_re,
             ssm_b_im=ssm_b_im, ssm_c_re=ssm_c_re, ssm_c_im=ssm_c_im, ssm_d=ssm_d,
             ssm_log_step=ssm_log_step, ssm_w_glu=ssm_w_glu, ssm_b_glu=ssm_b_glu,
             attn_w_qkv=attn_w_qkv, attn_q_norm_g=attn_q_norm_g, attn_k_norm_g=attn_k_norm_g,
             attn_w_o=attn_w_o, mem_w_q=mem_w_q, mem_q_norm_g=mem_q_norm_g, mem_w_o=mem_w_o,
             ffn_w_in=ffn_w_in, ffn_w_out=ffn_w_out)

    pmk, pmv = [], []
    for i in range(DEPTH):
        k_i, v_i = mem_kv(mem_prompt, norm_memin_g[i], mem_w_kv[i], mem_k_norm_g[i])
        pmk.append(k_i)
        pmv.append(v_i)
    p_mem_k = jnp.stack(pmk, 0)
    p_mem_v = jnp.stack(pmv, 0)
    zeros_state = jnp.zeros((N_SSM_LAYERS, x_prompt.shape[0], SSM_GROUPS, SSM_STATE), jnp.float32)
    y_prompt, p_ssm_re, p_ssm_im, p_swa_k, p_swa_v = trunk(
        x_prompt, p_mem_k, p_mem_v, zeros_state, zeros_state, None, None, p)

    y_sample, s_ssm_re, s_ssm_im, s_swa_k, s_swa_v = trunk(
        x_sample, cache_mem_k, cache_mem_v, state_ssm_re, state_ssm_im,
        [cache_swa0_k, cache_swa1_k, cache_swa2_k], [cache_swa0_v, cache_swa1_v, cache_swa2_v], p)

    p_swa0_k, p_swa1_k, p_swa2_k = p_swa_k
    p_swa0_v, p_swa1_v, p_swa2_v = p_swa_v
    s_swa0_k, s_swa1_k, s_swa2_k = s_swa_k
    s_swa0_v, s_swa1_v, s_swa2_v = s_swa_v
    return (y_prompt, y_sample,
            p_ssm_re, p_ssm_im, p_swa0_k, p_swa0_v, p_swa1_k, p_swa1_v, p_swa2_k, p_swa2_v,
            p_mem_k, p_mem_v,
            s_ssm_re, s_ssm_im, s_swa0_k, s_swa0_v, s_swa1_k, s_swa1_v, s_swa2_k, s_swa2_v)
```

```python
import functools
import math

import jax
import jax.numpy as jnp
from jax import lax
from jax.experimental import pallas as pl
from jax.experimental.pallas import tpu as pltpu

F32 = jnp.float32
BF16 = jnp.bfloat16

D_MODEL = 1024
DEPTH = 4
SSM_GROUP = 16
SSM_GROUPS = D_MODEL // SSM_GROUP
SSM_STATE = 64
SWA_GROUPS = ((128, 1), (512, 4), (2048, 16))
N_SWA = len(SWA_GROUPS)
SWA_HEADS = 8
SWA_HEAD_DIM = 64
SWA_WIDTH = SWA_HEADS * SWA_HEAD_DIM
QKV_WIDTH = 3 * N_SWA * SWA_WIDTH
SWA_KEYS_BACK = 128
N_MEM = 256
MEM_HEADS = 4
MEM_HEAD_DIM = D_MODEL // MEM_HEADS
D_FF = -(-8 * D_MODEL // (3 * 256)) * 256
RMS_EPS = 1e-6
NEG_BIG = -1e30

LANES = 128
SUBLANES = 8
VMEM_LIMIT = 48 * 1024 * 1024
Q_BLK = 128
FF_BLK = 256
SAMPLE_ROWS = SUBLANES
SSM_Q_PROMPT = 16


def _cparams(*sem):
    return pltpu.CompilerParams(dimension_semantics=sem, vmem_limit_bytes=VMEM_LIMIT)


def _dot(a, b):
    return jnp.dot(a, b, preferred_element_type=F32)


def _dot_nt(a, b, precision=None):
    return lax.dot_general(a, b, (((1,), (1,)), ((), ())), preferred_element_type=F32,
                           precision=precision)


def _rms(x, g):
    return x * lax.rsqrt(jnp.mean(x * x, -1, keepdims=True) + RMS_EPS) * g


def _headnorm(y, g_row, hd):
    tm, n = y.shape
    outs = []
    if hd % LANES == 0:
        for c in range(n // hd):
            yc = y[:, c * hd:(c + 1) * hd]
            ms = jnp.mean(yc * yc, -1, keepdims=True)
            outs.append(yc * lax.rsqrt(ms + RMS_EPS))
    else:
        assert 2 * hd == LANES
        lo = lax.broadcasted_iota(jnp.int32, (tm, LANES), 1) < hd
        for c in range(n // LANES):
            yc = y[:, c * LANES:(c + 1) * LANES]
            sq = yc * yc
            s_lo = jnp.sum(jnp.where(lo, sq, 0.0), -1, keepdims=True)
            s_hi = jnp.sum(jnp.where(lo, 0.0, sq), -1, keepdims=True)
            ms = jnp.where(lo, s_lo, s_hi) * (1.0 / hd)
            outs.append(yc * lax.rsqrt(ms + RMS_EPS))
    return jnp.concatenate(outs, -1) * g_row


def _sigmoid(x):
    return 1.0 / (1.0 + jnp.exp(-x))


def _gelu_tanh(x):
    return 0.5 * x * (1.0 + jnp.tanh(math.sqrt(2.0 / math.pi) * (x + 0.044715 * (x * x * x))))


def _norm_cast_kernel(x_ref, g_ref, o_ref):
    o_ref[...] = _rms(x_ref[...], g_ref[...]).astype(BF16)


def norm_cast(x, g, tm):
    t = x.shape[0]
    return pl.pallas_call(
        _norm_cast_kernel,
        grid=(t // tm,),
        in_specs=[pl.BlockSpec((tm, D_MODEL), lambda i: (i, 0)),
                  pl.BlockSpec((1, D_MODEL), lambda i: (0, 0))],
        out_specs=pl.BlockSpec((tm, D_MODEL), lambda i: (i, 0)),
        out_shape=jax.ShapeDtypeStruct((t, D_MODEL), BF16),
        compiler_params=_cparams("parallel"),
        name="norm_cast",
    )(x, g)


def _qkv_kernel(x_ref, g_ref, w_ref, hg_ref, o_ref, xn_ref):
    j = pl.program_id(1)

    @pl.when(j == 0)
    def _():
        xn_ref[...] = _rms(x_ref[...], g_ref[...]).astype(BF16)

    y = _dot(xn_ref[...], w_ref[...])

    @pl.when(j < 2 * N_SWA)
    def _():
        o_ref[...] = _headnorm(y, hg_ref[...], SWA_HEAD_DIM)

    @pl.when(j >= 2 * N_SWA)
    def _():
        o_ref[...] = y


def qkv_proj(x, g, w, head_gain, tm):
    t = x.shape[0]
    return pl.pallas_call(
        _qkv_kernel,
        grid=(t // tm, QKV_WIDTH // SWA_WIDTH),
        in_specs=[pl.BlockSpec((tm, D_MODEL), lambda i, j: (i, 0)),
                  pl.BlockSpec((1, D_MODEL), lambda i, j: (0, 0)),
                  pl.BlockSpec((D_MODEL, SWA_WIDTH), lambda i, j: (0, j)),
                  pl.BlockSpec((1, SWA_WIDTH), lambda i, j: (0, j))],
        out_specs=pl.BlockSpec((tm, SWA_WIDTH), lambda i, j: (i, j)),
        out_shape=jax.ShapeDtypeStruct((t, QKV_WIDTH), F32),
        scratch_shapes=[pltpu.VMEM((tm, D_MODEL), BF16)],
        compiler_params=_cparams("parallel", "arbitrary"),
        name="qkv_proj",
    )(x, g, w, head_gain)


def _mem_kv_kernel(x_ref, g_ref, w_ref, kg_ref, o_ref):
    j = pl.program_id(1)
    y = _dot(_rms(x_ref[...], g_ref[...]).astype(BF16), w_ref[...])

    @pl.when(j == 0)
    def _():
        o_ref[...] = _headnorm(y, kg_ref[...], MEM_HEAD_DIM)

    @pl.when(j == 1)
    def _():
        o_ref[...] = y


def mem_kv(mem, g_in, w_kv, k_gain):
    t = mem.shape[0]
    return pl.pallas_call(
        _mem_kv_kernel,
        grid=(DEPTH, 2),
        in_specs=[pl.BlockSpec((t, D_MODEL), lambda l, j: (0, 0)),
                  pl.BlockSpec((None, 1, D_MODEL), lambda l, j: (l, 0, 0)),
                  pl.BlockSpec((None, D_MODEL, D_MODEL), lambda l, j: (l, 0, j)),
                  pl.BlockSpec((None, 1, D_MODEL), lambda l, j: (l, 0, 0))],
        out_specs=pl.BlockSpec((None, t, D_MODEL), lambda l, j: (l, 0, j)),
        out_shape=jax.ShapeDtypeStruct((DEPTH, t, 2 * D_MODEL), F32),
        compiler_params=_cparams("parallel", "arbitrary"),
        name="mem_kv",
    )(mem, g_in, w_kv, k_gain)


def _swiglu_kernel(x_ref, g_ref, wg_ref, wu_ref, wo_ref, o_ref, xn_ref, acc_ref):
    c = pl.program_id(1)

    @pl.when(c == 0)
    def _():
        xn_ref[...] = _rms(x_ref[...], g_ref[...]).astype(BF16)
        acc_ref[...] = jnp.zeros_like(acc_ref)

    xn = xn_ref[...]
    gate = _dot(xn, wg_ref[...])
    up = _dot(xn, wu_ref[...])
    act = gate * _sigmoid(gate) * up
    acc_ref[...] += _dot(act.astype(BF16), wo_ref[...])

    @pl.when(c == pl.num_programs(1) - 1)
    def _():
        o_ref[...] = x_ref[...] + acc_ref[...]


def swiglu_block(x, g, w_in, w_out, tm):
    t = x.shape[0]
    nc = D_FF // FF_BLK
    return pl.pallas_call(
        _swiglu_kernel,
        grid=(t // tm, nc),
        in_specs=[pl.BlockSpec((tm, D_MODEL), lambda i, c: (i, 0)),
                  pl.BlockSpec((1, D_MODEL), lambda i, c: (0, 0)),
                  pl.BlockSpec((D_MODEL, FF_BLK), lambda i, c: (0, c)),
                  pl.BlockSpec((D_MODEL, FF_BLK), lambda i, c: (0, nc + c)),
                  pl.BlockSpec((FF_BLK, D_MODEL), lambda i, c: (c, 0))],
        out_specs=pl.BlockSpec((tm, D_MODEL), lambda i, c: (i, 0)),
        out_shape=jax.ShapeDtypeStruct((t, D_MODEL), F32),
        scratch_shapes=[pltpu.VMEM((tm, D_MODEL), BF16), pltpu.VMEM((tm, D_MODEL), F32)],
        compiler_params=_cparams("parallel", "arbitrary"),
        name="swiglu",
    )(x, g, w_in, w_in, w_out)


def _mem_heads(q, mk, mv):
    outs = []
    for h in range(MEM_HEADS):
        sl = slice(h * MEM_HEAD_DIM, (h + 1) * MEM_HEAD_DIM)
        s = _dot_nt(q[:, sl].astype(BF16), mk[:, sl])
        m = jnp.max(s, -1, keepdims=True)
        p = jnp.exp(s - m)
        l = jnp.sum(p, -1, keepdims=True)
        outs.append(_dot(p.astype(BF16), mv[:, sl]) / l)
    return jnp.concatenate(outs, -1)


def _mem_attn_kernel(x_ref, g_ref, wq_ref, qg_ref, mk_ref, mv_ref, wo_ref, o_ref):
    x = x_ref[...]
    q = _dot(_rms(x, g_ref[...]).astype(BF16), wq_ref[...])
    q = _headnorm(q, qg_ref[...], MEM_HEAD_DIM) * (MEM_HEAD_DIM ** -0.5)
    o = _mem_heads(q, mk_ref[...].astype(BF16), mv_ref[...].astype(BF16))
    o_ref[...] = x + _dot(o.astype(BF16), wo_ref[...])


def mem_attn_prompt(x, g, w_q, q_gain, mkv, layer, w_o, batch, tm):
    t = x.shape[0]
    per_b = t // batch // tm
    return pl.pallas_call(
        _mem_attn_kernel,
        grid=(batch, per_b),
        in_specs=[pl.BlockSpec((tm, D_MODEL), lambda b, i: (b * per_b + i, 0)),
                  pl.BlockSpec((1, D_MODEL), lambda b, i: (0, 0)),
                  pl.BlockSpec((D_MODEL, D_MODEL), lambda b, i: (0, 0)),
                  pl.BlockSpec((1, D_MODEL), lambda b, i: (0, 0)),
                  pl.BlockSpec((None, N_MEM, D_MODEL), lambda b, i: (layer, b, 0)),
                  pl.BlockSpec((None, N_MEM, D_MODEL), lambda b, i: (layer, b, 1)),
                  pl.BlockSpec((D_MODEL, D_MODEL), lambda b, i: (0, 0))],
        out_specs=pl.BlockSpec((tm, D_MODEL), lambda b, i: (b * per_b + i, 0)),
        out_shape=jax.ShapeDtypeStruct((t, D_MODEL), F32),
        compiler_params=_cparams("parallel", "arbitrary"),
        name="mem_attn_prompt",
    )(x, g, w_q, q_gain, mkv, mkv, w_o)


def _mem_attn_sample_kernel(x_ref, g_ref, wq_ref, qg_ref, mk_ref, mv_ref, wo_ref, o_ref,
                            q_scr, o_scr):
    b = pl.program_id(0)

    @pl.when(b == 0)
    def _():
        q = _dot(_rms(x_ref[...], g_ref[...]).astype(BF16), wq_ref[...])
        q_scr[...] = _headnorm(q, qg_ref[...], MEM_HEAD_DIM) * (MEM_HEAD_DIM ** -0.5)

    rows = pl.ds(pl.multiple_of(b * SAMPLE_ROWS, SAMPLE_ROWS), SAMPLE_ROWS)
    o_scr[rows, :] = _mem_heads(q_scr[rows, :], mk_ref[...].astype(BF16), mv_ref[...].astype(BF16))

    @pl.when(b == pl.num_programs(0) - 1)
    def _():
        o_ref[...] = x_ref[...] + _dot(o_scr[...].astype(BF16), wo_ref[...])


def mem_attn_sample(x, g, w_q, q_gain, cache_k, cache_v, layer, w_o, batch):
    t = x.shape[0]
    const = lambda b: (0, 0)
    return pl.pallas_call(
        _mem_attn_sample_kernel,
        grid=(batch,),
        in_specs=[pl.BlockSpec((t, D_MODEL), const),
                  pl.BlockSpec((1, D_MODEL), const),
                  pl.BlockSpec((D_MODEL, D_MODEL), const),
                  pl.BlockSpec((1, D_MODEL), const),
                  pl.BlockSpec((None, None, N_MEM, D_MODEL), lambda b: (layer, b, 0, 0)),
                  pl.BlockSpec((None, None, N_MEM, D_MODEL), lambda b: (layer, b, 0, 0)),
                  pl.BlockSpec((D_MODEL, D_MODEL), const)],
        out_specs=pl.BlockSpec((t, D_MODEL), const),
        out_shape=jax.ShapeDtypeStruct((t, D_MODEL), F32),
        scratch_shapes=[pltpu.VMEM((t, D_MODEL), F32), pltpu.VMEM((t, D_MODEL), F32)],
        compiler_params=_cparams("arbitrary"),
        name="mem_attn_sample",
    )(x, g, w_q, q_gain, cache_k, cache_v, w_o)


def _swa_prompt_kernel(q_ref, kp_ref, kc_ref, vp_ref, vc_ref, o_ref, l_ref):
    i = pl.program_id(2)
    q = (q_ref[...] * (SWA_HEAD_DIM ** -0.5)).astype(BF16)
    k = jnp.concatenate([kp_ref[...], kc_ref[...]], 0).astype(BF16)
    v = jnp.concatenate([vp_ref[...], vc_ref[...]], 0).astype(BF16)
    qq = lax.broadcasted_iota(jnp.int32, (Q_BLK, 2 * Q_BLK), 0)
    kk = lax.broadcasted_iota(jnp.int32, (Q_BLK, 2 * Q_BLK), 1)
    valid = (kk >= qq) & (kk <= qq + SWA_KEYS_BACK) & ((kk >= Q_BLK) | (i > 0))
    outs, lses = [], []
    for h in range(SWA_HEADS):
        sl = slice(h * SWA_HEAD_DIM, (h + 1) * SWA_HEAD_DIM)
        s = jnp.where(valid, _dot_nt(q[:, sl], k[:, sl]), NEG_BIG)
        m = jnp.max(s, -1, keepdims=True)
        p = jnp.exp(s - m)
        l = jnp.sum(p, -1, keepdims=True)
        outs.append(_dot(p.astype(BF16), v[:, sl]) / l)
        lses.append(jnp.broadcast_to(m + jnp.log(l), (Q_BLK, SWA_HEAD_DIM)))
    o_ref[...] = jnp.concatenate(outs, -1)
    l_ref[...] = jnp.concatenate(lses, -1)


def swa_prompt(qkv, batch, seq, group):
    dil = SWA_GROUPS[group][1]
    nblk = QKV_WIDTH // SWA_WIDTH
    rows = seq // dil
    qkv3 = qkv.reshape(batch, rows, dil * QKV_WIDTH)
    blk = (None, Q_BLK, SWA_WIDTH)
    q_spec = pl.BlockSpec(blk, lambda b, r, i: (b, i, r * nblk + group))
    kp_spec = pl.BlockSpec(blk, lambda b, r, i: (b, jnp.maximum(i - 1, 0), r * nblk + N_SWA + group))
    kc_spec = pl.BlockSpec(blk, lambda b, r, i: (b, i, r * nblk + N_SWA + group))
    vp_spec = pl.BlockSpec(blk, lambda b, r, i: (b, jnp.maximum(i - 1, 0), r * nblk + 2 * N_SWA + group))
    vc_spec = pl.BlockSpec(blk, lambda b, r, i: (b, i, r * nblk + 2 * N_SWA + group))
    o_spec = pl.BlockSpec(blk, lambda b, r, i: (b, i, r))
    shape = jax.ShapeDtypeStruct((batch, rows, dil * SWA_WIDTH), F32)
    o, lse = pl.pallas_call(
        _swa_prompt_kernel,
        grid=(batch, dil, rows // Q_BLK),
        in_specs=[q_spec, kp_spec, kc_spec, vp_spec, vc_spec],
        out_specs=[o_spec, o_spec],
        out_shape=[shape, shape],
        compiler_params=_cparams("parallel", "parallel", "arbitrary"),
        name="swa_prompt_g%d" % group,
    )(qkv3, qkv3, qkv3, qkv3, qkv3)
    return o.reshape(batch * seq, SWA_WIDTH), lse.reshape(batch * seq, SWA_WIDTH)


def _swa_sample_kernel(q_ref, kn_ref, vn_ref, kc_ref, vc_ref, o_ref, l_ref, *, group, n_new):
    nq = SAMPLE_ROWS
    nr = nq * SWA_HEADS
    q = q_ref[...] * (SWA_HEAD_DIM ** -0.5)
    qbd = jnp.concatenate([jnp.broadcast_to(q[i:i + 1, :], (SWA_HEADS, SWA_WIDTH)) for i in range(nq)], 0)
    row = lax.broadcasted_iota(jnp.int32, (nr, SWA_WIDTH), 0)
    lane = lax.broadcasted_iota(jnp.int32, (nr, SWA_WIDTH), 1)
    own = (lane // SWA_HEAD_DIM) == (row % SWA_HEADS)
    qbd = jnp.where(own, qbd, 0.0).astype(BF16)

    def qi(n):
        return lax.broadcasted_iota(jnp.int32, (nr, n), 0) // SWA_HEADS

    def col(n):
        return lax.broadcasted_iota(jnp.int32, (nr, n), 1)

    parts = []
    kn, vn = kn_ref[...].astype(BF16), vn_ref[...].astype(BF16)
    if group == 0:
        parts.append((kn, vn, (col(nq) <= qi(nq)) & (col(nq) < n_new)))
        rows_c = kc_ref.shape[0]
        parts.append((kc_ref[...].astype(BF16), vc_ref[...].astype(BF16), col(rows_c) >= qi(rows_c)))
    elif group == 1:
        parts.append((kn, vn, (col(nq) == qi(nq)) & (col(nq) < n_new)))
        rows_c = kc_ref.shape[0]
        parts.append((kc_ref[...].astype(BF16), vc_ref[...].astype(BF16),
                      (col(rows_c) % SWA_GROUPS[1][1]) == qi(rows_c)))
    else:
        parts.append((kn, vn, (col(nq) == qi(nq)) & (col(nq) < n_new)))
        rows_c = kc_ref.shape[0]
        for c in range(n_new):
            sl = slice(c * SWA_WIDTH, (c + 1) * SWA_WIDTH)
            parts.append((kc_ref[:, sl].astype(BF16), vc_ref[:, sl].astype(BF16), qi(rows_c) == c))

    scores = [jnp.where(valid, _dot_nt(qbd, kp), NEG_BIG) for kp, _, valid in parts]
    m = scores[0].max(-1, keepdims=True)
    for s in scores[1:]:
        m = jnp.maximum(m, s.max(-1, keepdims=True))
    l = jnp.zeros_like(m)
    acc = jnp.zeros((nr, SWA_WIDTH), F32)
    for s, (_, vp, _) in zip(scores, parts):
        p = jnp.exp(s - m)
        l = l + p.sum(-1, keepdims=True)
        acc = acc + _dot(p.astype(BF16), vp)
    o_full = jnp.where(own, acc / l, 0.0)
    l_full = jnp.where(own, m + jnp.log(l), 0.0)
    o_ref[...] = jnp.concatenate(
        [o_full[i * SWA_HEADS:(i + 1) * SWA_HEADS].sum(0, keepdims=True) for i in range(nq)], 0)
    l_ref[...] = jnp.concatenate(
        [l_full[i * SWA_HEADS:(i + 1) * SWA_HEADS].sum(0, keepdims=True) for i in range(nq)], 0)


def swa_sample(qkv, cache_k, cache_v, layer, group, n_new):
    batch = qkv.shape[0]
    rows = cache_k.shape[2]
    blk = (None, SAMPLE_ROWS, SWA_WIDTH)
    if group == 2:
        dil = SWA_GROUPS[2][1]
        cache_k = cache_k.reshape(cache_k.shape[0], batch, rows // dil, dil * SWA_WIDTH)
        cache_v = cache_v.reshape(cache_v.shape[0], batch, rows // dil, dil * SWA_WIDTH)
        c_spec = pl.BlockSpec((None, None, rows // dil, n_new * SWA_WIDTH), lambda b: (layer, b, 0, 0))
    else:
        c_spec = pl.BlockSpec((None, None, rows, SWA_WIDTH), lambda b: (layer, b, 0, 0))
    shape = jax.ShapeDtypeStruct((batch, SAMPLE_ROWS, SWA_WIDTH), F32)
    return pl.pallas_call(
        functools.partial(_swa_sample_kernel, group=group, n_new=n_new),
        grid=(batch,),
        in_specs=[pl.BlockSpec(blk, lambda b: (b, 0, group)),
                  pl.BlockSpec(blk, lambda b: (b, 0, N_SWA + group)),
                  pl.BlockSpec(blk, lambda b: (b, 0, 2 * N_SWA + group)),
                  c_spec, c_spec],
        out_specs=[pl.BlockSpec(blk, lambda b: (b, 0, 0))] * 2,
        out_shape=[shape, shape],
        compiler_params=_cparams("parallel"),
        name="swa_sample_g%d" % group,
    )(qkv, qkv, qkv, cache_k, cache_v)


def _attn_out_kernel(x_ref, o0, o1, o2, l0, l1, l2, w_ref, out_ref):
    a, b, c = l0[...], l1[...], l2[...]
    m = jnp.maximum(jnp.maximum(a, b), c)
    ea, eb, ec = jnp.exp(a - m), jnp.exp(b - m), jnp.exp(c - m)
    o = (ea * o0[...] + eb * o1[...] + ec * o2[...]) / (ea + eb + ec)
    out_ref[...] = x_ref[...] + _dot(o.astype(BF16), w_ref[...])


def attn_out(x, outs, lses, w_o, tm):
    t = x.shape[0]
    row = pl.BlockSpec((tm, SWA_WIDTH), lambda i: (i, 0))
    return pl.pallas_call(
        _attn_out_kernel,
        grid=(t // tm,),
        in_specs=[pl.BlockSpec((tm, D_MODEL), lambda i: (i, 0))] + [row] * 6
                 + [pl.BlockSpec((SWA_WIDTH, D_MODEL), lambda i: (0, 0))],
        out_specs=pl.BlockSpec((tm, D_MODEL), lambda i: (i, 0)),
        out_shape=jax.ShapeDtypeStruct((t, D_MODEL), F32),
        compiler_params=_cparams("parallel"),
        name="attn_out",
    )(x, *outs, *lses, w_o)


def _ssm_param_kernel(lr_ref, li_ref, ls_ref, btr_ref, bti_ref, cr_ref, ci_ref,
                      win_ref, wout_ref, kt_ref, aqr_ref, aqi_ref, e_scr, *, q):
    g = SSM_GROUP
    lr, li = lr_ref[...], li_ref[...]
    step = jnp.exp(ls_ref[...])
    lo = lax.broadcasted_iota(jnp.int32, (1, LANES), 1) < SSM_STATE

    def powers(tau):
        mag = jnp.exp(tau * (lr * step))
        ang = tau * (li * step)
        return mag * jnp.cos(ang), mag * jnp.sin(ang)

    tau = lax.broadcasted_iota(jnp.int32, (q + 1, LANES), 0).astype(F32)
    c2, s2 = powers(tau)
    ar, ai = c2[1:2], s2[1:2]
    den = lr * lr + li * li
    fr = ((ar - 1.0) * lr + ai * li) / den
    fi = (ai * lr - (ar - 1.0) * li) / den
    btr, bti = btr_ref[...], bti_ref[...]
    bbr = fr * btr - fi * bti
    bbi = fr * bti + fi * btr
    bba = jnp.where(lo, bbr, bbi)
    bbb = jnp.where(lo, -bbi, bbr)
    pa = jnp.where(lo, c2, -s2)
    pb = jnp.where(lo, s2, c2)
    cr, ci = cr_ref[...], ci_ref[...]
    for t in range(q + 1):
        e_scr[t * g:(t + 1) * g, :] = cr * pa[t:t + 1] - ci * pb[t:t + 1]
    wout_ref[...] = e_scr[g:(q + 1) * g, :].astype(BF16)
    kt_ref[...] = _dot_nt(bba, e_scr[0:q * g, :], precision=lax.Precision.HIGHEST)
    c2r, s2r = powers((q - 1.0) - tau[0:q])
    for s in range(q):
        win_ref[s * g:(s + 1) * g, :] = (c2r[s:s + 1] * bba + s2r[s:s + 1] * bbb).astype(BF16)
    aqr_ref[...] = c2[q:q + 1]
    aqi_ref[...] = jnp.where(lo, -s2[q:q + 1], s2[q:q + 1])


def ssm_params(lam_re, lam_im, log_step, b_re, b_im, c_re, c_im, q):
    gr, st, g = SSM_GROUPS, SSM_STATE, SSM_GROUP
    dup = lambda a: jnp.concatenate([a, a], -1)
    lr = dup(lam_re).reshape(gr, 1, LANES)
    li = dup(lam_im).reshape(gr, 1, LANES)
    ls = jnp.broadcast_to(log_step[:, None, None], (gr, 1, LANES))
    btr = dup(jnp.swapaxes(b_re, 1, 2))
    bti = dup(jnp.swapaxes(b_im, 1, 2))
    cr, ci = dup(c_re), dup(c_im)
    vec = pl.BlockSpec((None, 1, LANES), lambda i: (i, 0, 0))
    mat = pl.BlockSpec((None, g, LANES), lambda i: (i, 0, 0))
    big = pl.BlockSpec((None, q * g, LANES), lambda i: (i, 0, 0))
    win, wout, kt, aqr, aqi = pl.pallas_call(
        functools.partial(_ssm_param_kernel, q=q),
        grid=(gr,),
        in_specs=[vec, vec, vec, mat, mat, mat, mat],
        out_specs=[big, big, pl.BlockSpec((None, g, q * g), lambda i: (i, 0, 0)), vec, vec],
        out_shape=[jax.ShapeDtypeStruct((gr, q * g, LANES), BF16),
                   jax.ShapeDtypeStruct((gr, q * g, LANES), BF16),
                   jax.ShapeDtypeStruct((gr, g, q * g), F32),
                   jax.ShapeDtypeStruct((gr, 1, LANES), F32),
                   jax.ShapeDtypeStruct((gr, 1, LANES), F32)],
        scratch_shapes=[pltpu.VMEM(((q + 1) * g, LANES), F32)],
        compiler_params=_cparams("parallel"),
        name="ssm_params_q%d" % q,
    )(lr, li, ls, btr, bti, cr, ci)
    kt4 = kt.reshape(gr, g, q, g)
    lag = jnp.arange(q)[None, :] - jnp.arange(q)[:, None]
    tt = jnp.where((lag >= 0)[None, None, :, :, None], kt4[:, :, jnp.maximum(lag, 0), :], 0.0)
    tt = jnp.transpose(tt, (0, 2, 1, 3, 4)).reshape(gr, q * g, q * g).astype(BF16)
    return win, wout, tt, aqr, aqi


def _ssm_in_kernel(u_ref, win_ref, s_ref):
    s_ref[...] = _dot(u_ref[...], win_ref[...])


def _ssm_scan_kernel(s_ref, h0_ref, ar_ref, ai_ref, hp_ref, hf_ref, h_scr):
    @pl.when(pl.program_id(0) == 0)
    def _():
        h_scr[...] = h0_ref[...]

    def body(k, h):
        hp_ref[k] = h.astype(BF16)
        return ar_ref[...] * h + ai_ref[...] * pltpu.roll(h, SSM_STATE, 1) + s_ref[k]

    h = lax.fori_loop(0, s_ref.shape[0], body, h_scr[...])
    h_scr[...] = h
    hf_ref[...] = h


def _ssm_out_kernel(u_ref, tt_ref, hp_ref, wout_ref, y_ref):
    y_ref[...] = _dot(u_ref[...], tt_ref[...]) + _dot_nt(hp_ref[...], wout_ref[...])


def ssm_mix(h_bf16, h0_re, h0_im, params, batch, seq, q):
    win, wout, tt, aqr, aqi = params
    gr, st, g = SSM_GROUPS, SSM_STATE, SSM_GROUP
    nk = seq // q
    n = nk * batch
    qg = q * g
    u = h_bf16.reshape(batch, nk, q, gr, g).transpose(3, 1, 0, 2, 4).reshape(gr, n, qg)
    s = pl.pallas_call(
        _ssm_in_kernel,
        grid=(gr,),
        in_specs=[pl.BlockSpec((None, n, qg), lambda i: (i, 0, 0)),
                  pl.BlockSpec((None, qg, LANES), lambda i: (i, 0, 0))],
        out_specs=pl.BlockSpec((n, LANES), lambda i: (0, i)),
        out_shape=jax.ShapeDtypeStruct((n, gr * LANES), F32),
        compiler_params=_cparams("parallel"),
        name="ssm_in_q%d" % q,
    )(u, win)
    rows = batch * gr
    kc = min(nk, 32)
    tile = lambda a: jnp.broadcast_to(a.reshape(1, gr, LANES), (batch, gr, LANES)).reshape(rows, LANES)
    h0 = jnp.concatenate([h0_re, h0_im], -1).reshape(rows, LANES)
    full = pl.BlockSpec((rows, LANES), lambda i: (0, 0))
    hp, hf = pl.pallas_call(
        _ssm_scan_kernel,
        grid=(nk // kc,),
        in_specs=[pl.BlockSpec((kc, rows, LANES), lambda i: (i, 0, 0)), full, full, full],
        out_specs=[pl.BlockSpec((kc, rows, LANES), lambda i: (i, 0, 0)), full],
        out_shape=[jax.ShapeDtypeStruct((nk, rows, LANES), BF16),
                   jax.ShapeDtypeStruct((rows, LANES), F32)],
        scratch_shapes=[pltpu.VMEM((rows, LANES), F32)],
        compiler_params=_cparams("arbitrary"),
        name="ssm_scan_q%d" % q,
    )(s.reshape(nk, rows, LANES), h0, tile(aqr), tile(aqi))
    y = pl.pallas_call(
        _ssm_out_kernel,
        grid=(gr,),
        in_specs=[pl.BlockSpec((None, n, qg), lambda i: (i, 0, 0)),
                  pl.BlockSpec((None, qg, qg), lambda i: (i, 0, 0)),
                  pl.BlockSpec((n, LANES), lambda i: (0, i)),
                  pl.BlockSpec((None, qg, LANES), lambda i: (i, 0, 0))],
        out_specs=pl.BlockSpec((None, n, qg), lambda i: (i, 0, 0)),
        out_shape=jax.ShapeDtypeStruct((gr, n, qg), F32),
        compiler_params=_cparams("parallel"),
        name="ssm_out_q%d" % q,
    )(u, tt, hp.reshape(n, gr * LANES), wout)
    y = y.reshape(gr, nk, batch, q, g).transpose(2, 1, 3, 0, 4).reshape(batch * seq, D_MODEL)
    hf = hf.reshape(batch, gr, LANES)
    return y, hf[..., :st], hf[..., st:]


def _ssm_glu_kernel(x_ref, gm_ref, y_ref, d_ref, w_ref, b_ref, o_ref):
    x = x_ref[...]
    y = y_ref[...] + d_ref[...] * _rms(x, gm_ref[...])
    gl = _gelu_tanh(y)
    z = _dot(gl.astype(BF16), w_ref[...]) + b_ref[...]
    o_ref[...] = x + gl * _sigmoid(z)


def ssm_glu(x, g_mix, y, d_skip, w_glu, b_glu, tm):
    t = x.shape[0]
    row = pl.BlockSpec((tm, D_MODEL), lambda i: (i, 0))
    vec = pl.BlockSpec((1, D_MODEL), lambda i: (0, 0))
    return pl.pallas_call(
        _ssm_glu_kernel,
        grid=(t // tm,),
        in_specs=[row, vec, row, vec, pl.BlockSpec((D_MODEL, D_MODEL), lambda i: (0, 0)), vec],
        out_specs=row,
        out_shape=jax.ShapeDtypeStruct((t, D_MODEL), F32),
        compiler_params=_cparams("parallel"),
        name="ssm_glu",
    )(x, g_mix, y, d_skip, w_glu, b_glu)


def _row(v):
    return v.reshape(1, -1).astype(F32)


def kernel(x_prompt, x_sample, mem_prompt, state_ssm_re, state_ssm_im, cache_swa0_k, cache_swa0_v, cache_swa1_k, cache_swa1_v, cache_swa2_k, cache_swa2_v, cache_mem_k, cache_mem_v, norm_mix_g, norm_mem_g, norm_memin_g, norm_ffn_g, ssm_lambda_re, ssm_lambda_im, ssm_b_re, ssm_b_im, ssm_c_re, ssm_c_im, ssm_d, ssm_log_step, ssm_w_glu, ssm_b_glu, attn_w_qkv, attn_q_norm_g, attn_k_norm_g, attn_w_o, mem_w_q, mem_w_kv, mem_q_norm_g, mem_k_norm_g, mem_w_o, ffn_w_in, ffn_w_out):
    pb, seq, _ = x_prompt.shape
    sb, dec, _ = x_sample.shape
    n_ssm = state_ssm_re.shape[0]
    n_attn = cache_swa0_k.shape[0]
    tm_p = 512
    tm_s = sb * SAMPLE_ROWS

    w_glu = ssm_w_glu.astype(BF16)
    w_qkv = attn_w_qkv.astype(BF16)
    w_ao = attn_w_o.astype(BF16)
    w_mq = mem_w_q.astype(BF16)
    w_mkv = mem_w_kv.astype(BF16)
    w_mo = mem_w_o.astype(BF16)
    w_fi = ffn_w_in.astype(BF16)
    w_fo = ffn_w_out.astype(BF16)

    ones_w = jnp.ones((N_SWA, SWA_WIDTH), F32)
    head_gain = [jnp.concatenate([jnp.tile(attn_q_norm_g[j], (1, SWA_HEADS)),
                                  jnp.tile(attn_k_norm_g[j], (1, SWA_HEADS)), ones_w], 0).reshape(1, QKV_WIDTH)
                 for j in range(n_attn)]
    mem_q_gain = [_row(jnp.tile(mem_q_norm_g[i], MEM_HEADS)) for i in range(DEPTH)]
    mem_k_gain = jnp.tile(mem_k_norm_g, (1, MEM_HEADS)).reshape(DEPTH, 1, D_MODEL)

    mkv = mem_kv(mem_prompt.reshape(pb * N_MEM, D_MODEL), norm_memin_g.reshape(DEPTH, 1, D_MODEL),
                 w_mkv, mem_k_gain)
    mkv5 = mkv.reshape(DEPTH, pb, N_MEM, 2, MEM_HEADS, MEM_HEAD_DIM)
    p_mem_k, p_mem_v = mkv5[:, :, :, 0], mkv5[:, :, :, 1]
    mkv_b = mkv.reshape(DEPTH, pb * N_MEM, 2 * D_MODEL)

    ssm_p = [[ssm_params(ssm_lambda_re[j], ssm_lambda_im[j], ssm_log_step[j], ssm_b_re[j], ssm_b_im[j],
                         ssm_c_re[j], ssm_c_im[j], q) for q in (SSM_Q_PROMPT, dec)] for j in range(n_ssm)]

    caches_k = [c.reshape(c.shape[0], sb, c.shape[2], SWA_WIDTH) for c in (cache_swa0_k, cache_swa1_k, cache_swa2_k)]
    caches_v = [c.reshape(c.shape[0], sb, c.shape[2], SWA_WIDTH) for c in (cache_swa0_v, cache_swa1_v, cache_swa2_v)]
    cmk = cache_mem_k.reshape(DEPTH, sb, N_MEM, D_MODEL)
    cmv = cache_mem_v.reshape(DEPTH, sb, N_MEM, D_MODEL)

    xp = x_prompt.reshape(pb * seq, D_MODEL)
    xs = jnp.pad(x_sample, ((0, 0), (0, SAMPLE_ROWS - dec), (0, 0))).reshape(tm_s, D_MODEL)

    p_ssm_re, p_ssm_im, s_ssm_re, s_ssm_im = [], [], [], []
    p_swa_k = [[] for _ in SWA_GROUPS]
    p_swa_v = [[] for _ in SWA_GROUPS]
    s_swa_k = [[] for _ in SWA_GROUPS]
    s_swa_v = [[] for _ in SWA_GROUPS]
    zeros_state = jnp.zeros((pb, SSM_GROUPS, SSM_STATE), F32)

    for i in range(DEPTH):
        j = i // 2
        g_mix = _row(norm_mix_g[i])
        if i % 2 == 0:
            d_skip, b_glu = _row(ssm_d[j]), _row(ssm_b_glu[j])
            hb = norm_cast(xp, g_mix, tm_p)
            y, fr, fi = ssm_mix(hb, zeros_state, zeros_state, ssm_p[j][0], pb, seq, SSM_Q_PROMPT)
            p_ssm_re.append(fr)
            p_ssm_im.append(fi)
            xp = ssm_glu(xp, g_mix, y, d_skip, w_glu[j], b_glu, tm_p)
            hb = norm_cast(xs, g_mix, tm_s).reshape(sb, SAMPLE_ROWS, D_MODEL)[:, :dec].reshape(sb * dec, D_MODEL)
            y, fr, fi = ssm_mix(hb, state_ssm_re[j], state_ssm_im[j], ssm_p[j][1], sb, dec, dec)
            s_ssm_re.append(fr)
            s_ssm_im.append(fi)
            y = jnp.pad(y.reshape(sb, dec, D_MODEL), ((0, 0), (0, SAMPLE_ROWS - dec), (0, 0))).reshape(tm_s, D_MODEL)
            xs = ssm_glu(xs, g_mix, y, d_skip, w_glu[j], b_glu, tm_s)
        else:
            qkv = qkv_proj(xp, g_mix, w_qkv[j], head_gain[j], tm_p)
            qkv4 = qkv.reshape(pb, seq, 3 * N_SWA, SWA_HEADS, SWA_HEAD_DIM)
            outs, lses = [], []
            for g, (win, _) in enumerate(SWA_GROUPS):
                keep = min(win, seq)
                p_swa_k[g].append(qkv4[:, seq - keep:, N_SWA + g])
                p_swa_v[g].append(qkv4[:, seq - keep:, 2 * N_SWA + g])
                o, lse = swa_prompt(qkv, pb, seq, g)
                outs.append(o)
                lses.append(lse)
            xp = attn_out(xp, outs, lses, w_ao[j], tm_p)
            qkv = qkv_proj(xs, g_mix, w_qkv[j], head_gain[j], tm_s)
            qkv3 = qkv.reshape(sb, SAMPLE_ROWS, QKV_WIDTH)
            qkv4 = qkv.reshape(sb, SAMPLE_ROWS, 3 * N_SWA, SWA_HEADS, SWA_HEAD_DIM)
            outs, lses = [], []
            for g in range(N_SWA):
                s_swa_k[g].append(qkv4[:, :dec, N_SWA + g])
                s_swa_v[g].append(qkv4[:, :dec, 2 * N_SWA + g])
                o, lse = swa_sample(qkv3, caches_k[g], caches_v[g], j, g, dec)
                outs.append(o.reshape(tm_s, SWA_WIDTH))
                lses.append(lse.reshape(tm_s, SWA_WIDTH))
            xs = attn_out(xs, outs, lses, w_ao[j], tm_s)

        g_mem = _row(norm_mem_g[i])
        xp = mem_attn_prompt(xp, g_mem, w_mq[i], mem_q_gain[i], mkv_b, i, w_mo[i], pb, tm_p)
        xs = mem_attn_sample(xs, g_mem, w_mq[i], mem_q_gain[i], cmk, cmv, i, w_mo[i], sb)
        g_ffn = _row(norm_ffn_g[i])
        xp = swiglu_block(xp, g_ffn, w_fi[i], w_fo[i], 1024)
        xs = swiglu_block(xs, g_ffn, w_fi[i], w_fo[i], tm_s)

    y_prompt = xp.reshape(pb, seq, D_MODEL)
    y_sample = xs.reshape(sb, SAMPLE_ROWS, D_MODEL)[:, :dec]
    st = lambda a: jnp.stack(a, 0)
    return (y_prompt, y_sample,
            st(p_ssm_re), st(p_ssm_im),
            st(p_swa_k[0]), st(p_swa_v[0]), st(p_swa_k[1]), st(p_swa_v[1]), st(p_swa_k[2]), st(p_swa_v[2]),
            p_mem_k, p_mem_v,
            st(s_ssm_re), st(s_ssm_im),
            st(s_swa_k[0]), st(s_swa_v[0]), st(s_swa_k[1]), st(s_swa_v[1]), st(s_swa_k[2]), st(s_swa_v[2]))
```

```python
import functools
import math

import jax
import jax.numpy as jnp
from jax import lax
from jax.experimental import pallas as pl
from jax.experimental.pallas import tpu as pltpu

F32 = jnp.float32
BF16 = jnp.bfloat16

D_MODEL = 1024
DEPTH = 4
SSM_GROUP = 16
SSM_GROUPS = D_MODEL // SSM_GROUP
SSM_STATE = 64
SWA_GROUPS = ((128, 1), (512, 4), (2048, 16))
N_SWA = len(SWA_GROUPS)
SWA_HEADS = 8
SWA_HEAD_DIM = 64
SWA_WIDTH = SWA_HEADS * SWA_HEAD_DIM
QKV_WIDTH = 3 * N_SWA * SWA_WIDTH
SWA_KEYS_BACK = 128
N_MEM = 256
MEM_HEADS = 4
MEM_HEAD_DIM = D_MODEL // MEM_HEADS
D_FF = -(-8 * D_MODEL // (3 * 256)) * 256
RMS_EPS = 1e-6
NEG_BIG = -1e30

LANES = 128
SUBLANES = 8
VMEM_LIMIT = 48 * 1024 * 1024
Q_BLK = 128
FF_BLK = 256
SAMPLE_ROWS = SUBLANES
SSM_Q_PROMPT = 16


def _cparams(*sem):
    return pltpu.CompilerParams(dimension_semantics=sem, vmem_limit_bytes=VMEM_LIMIT)


def _dot(a, b):
    return jnp.dot(a, b, preferred_element_type=F32)


def _dot_nt(a, b, precision=None):
    return lax.dot_general(a, b, (((1,), (1,)), ((), ())), preferred_element_type=F32,
                           precision=precision)


def _rms(x, g):
    return x * lax.rsqrt(jnp.mean(x * x, -1, keepdims=True) + RMS_EPS) * g


def _headnorm(y, g_row, hd):
    tm, n = y.shape
    outs = []
    if hd % LANES == 0:
        for c in range(n // hd):
            yc = y[:, c * hd:(c + 1) * hd]
            ms = jnp.mean(yc * yc, -1, keepdims=True)
            outs.append(yc * lax.rsqrt(ms + RMS_EPS))
    else:
        assert 2 * hd == LANES
        lo = lax.broadcasted_iota(jnp.int32, (tm, LANES), 1) < hd
        for c in range(n // LANES):
            yc = y[:, c * LANES:(c + 1) * LANES]
            sq = yc * yc
            s_lo = jnp.sum(jnp.where(lo, sq, 0.0), -1, keepdims=True)
            s_hi = jnp.sum(jnp.where(lo, 0.0, sq), -1, keepdims=True)
            ms = jnp.where(lo, s_lo, s_hi) * (1.0 / hd)
            outs.append(yc * lax.rsqrt(ms + RMS_EPS))
    return jnp.concatenate(outs, -1) * g_row


def _sigmoid(x):
    return 1.0 / (1.0 + jnp.exp(-x))


def _gelu_tanh(x):
    return 0.5 * x * (1.0 + jnp.tanh(math.sqrt(2.0 / math.pi) * (x + 0.044715 * (x * x * x))))


def _norm_cast_kernel(x_ref, g_ref, o_ref):
    o_ref[...] = _rms(x_ref[...], g_ref[...]).astype(BF16)


def norm_cast(x, g, tm):
    t = x.shape[0]
    return pl.pallas_call(
        _norm_cast_kernel,
        grid=(t // tm,),
        in_specs=[pl.BlockSpec((tm, D_MODEL), lambda i: (i, 0)),
                  pl.BlockSpec((1, D_MODEL), lambda i: (0, 0))],
        out_specs=pl.BlockSpec((tm, D_MODEL), lambda i: (i, 0)),
        out_shape=jax.ShapeDtypeStruct((t, D_MODEL), BF16),
        compiler_params=_cparams("parallel"),
        name="norm_cast",
    )(x, g)


def _qkv_kernel(x_ref, g_ref, w_ref, hg_ref, o0_ref, o1_ref, o2_ref, xn_ref, y_ref, *, dils):
    j = pl.program_id(1)

    @pl.when(j == 0)
    def _():
        xn_ref[...] = _rms(x_ref[...], g_ref[...]).astype(BF16)

    tm = x_ref.shape[0]
    for grp, (o_ref, dil) in enumerate(zip((o0_ref, o1_ref, o2_ref), dils)):
        @pl.when(j == grp)
        def _(o_ref=o_ref, dil=dil):
            y = _dot(xn_ref[...], w_ref[...])
            qk = _headnorm(y[:, :2 * SWA_WIDTH], hg_ref[...], SWA_HEAD_DIM)
            if dil == 1:
                o_ref[0, :, :2 * SWA_WIDTH] = qk
                o_ref[0, :, 2 * SWA_WIDTH:] = y[:, 2 * SWA_WIDTH:]
            else:
                nqk = 2 * SWA_WIDTH // LANES
                for c in range(y_ref.shape[0]):
                    y_ref[c] = (qk[:, c * LANES:(c + 1) * LANES] if c < nqk
                                else y[:, c * LANES:(c + 1) * LANES])
                    for r in range(dil):
                        o_ref[r, :, c * LANES:(c + 1) * LANES] = y_ref[c, pl.ds(r, tm // dil, stride=dil), :]


def qkv_proj(x, g, w, layer, head_gain, batch, dils, tm):
    t = x.shape[0]
    seq = t // batch
    per_b = seq // tm
    gw = 3 * SWA_WIDTH
    return pl.pallas_call(
        functools.partial(_qkv_kernel, dils=dils),
        grid=(t // tm, N_SWA),
        in_specs=[pl.BlockSpec((tm, D_MODEL), lambda i, j: (i, 0)),
                  pl.BlockSpec((1, D_MODEL), lambda i, j: (0, 0)),
                  pl.BlockSpec((None, D_MODEL, gw), lambda i, j: (layer, 0, j)),
                  pl.BlockSpec((None, 1, 2 * SWA_WIDTH), lambda i, j: (j, 0, 0))],
        out_specs=[pl.BlockSpec((None, d, tm // d, gw), lambda i, j: (i // per_b, 0, i % per_b, 0))
                   for d in dils],
        out_shape=[jax.ShapeDtypeStruct((batch, d, seq // d, gw), F32) for d in dils],
        scratch_shapes=[pltpu.VMEM((tm, D_MODEL), BF16), pltpu.VMEM((gw // LANES, tm, LANES), F32)],
        compiler_params=_cparams("parallel", "arbitrary"),
        name="qkv_proj",
    )(x, g, w, head_gain)


def _mem_kv_kernel(x_ref, g_ref, w_ref, kg_ref, o_ref):
    j = pl.program_id(1)
    y = _dot(_rms(x_ref[...], g_ref[...]).astype(BF16), w_ref[...])

    @pl.when(j == 0)
    def _():
        o_ref[...] = _headnorm(y, kg_ref[...], MEM_HEAD_DIM)

    @pl.when(j == 1)
    def _():
        o_ref[...] = y


def mem_kv(mem, g_in, w_kv, k_gain):
    t = mem.shape[0]
    return pl.pallas_call(
        _mem_kv_kernel,
        grid=(DEPTH, 2),
        in_specs=[pl.BlockSpec((t, D_MODEL), lambda l, j: (0, 0)),
                  pl.BlockSpec((None, 1, D_MODEL), lambda l, j: (l, 0, 0)),
                  pl.BlockSpec((None, D_MODEL, D_MODEL), lambda l, j: (l, 0, j)),
                  pl.BlockSpec((None, 1, D_MODEL), lambda l, j: (l, 0, 0))],
        out_specs=pl.BlockSpec((None, t, D_MODEL), lambda l, j: (l, 0, j)),
        out_shape=jax.ShapeDtypeStruct((DEPTH, t, 2 * D_MODEL), F32),
        compiler_params=_cparams("parallel", "arbitrary"),
        name="mem_kv",
    )(mem, g_in, w_kv, k_gain)


def _swiglu_kernel(x_ref, g_ref, wg_ref, wu_ref, wo_ref, o_ref, xn_ref, acc_ref):
    c = pl.program_id(1)

    @pl.when(c == 0)
    def _():
        xn_ref[...] = _rms(x_ref[...], g_ref[...]).astype(BF16)
        acc_ref[...] = jnp.zeros_like(acc_ref)

    xn = xn_ref[...]
    gate = _dot(xn, wg_ref[...])
    up = _dot(xn, wu_ref[...])
    act = gate * _sigmoid(gate) * up
    acc_ref[...] += _dot(act.astype(BF16), wo_ref[...])

    @pl.when(c == pl.num_programs(1) - 1)
    def _():
        o_ref[...] = x_ref[...] + acc_ref[...]


def swiglu_block(x, g, w_in, w_out, layer, tm):
    t = x.shape[0]
    nc = D_FF // FF_BLK
    return pl.pallas_call(
        _swiglu_kernel,
        grid=(t // tm, nc),
        in_specs=[pl.BlockSpec((tm, D_MODEL), lambda i, c: (i, 0)),
                  pl.BlockSpec((1, D_MODEL), lambda i, c: (0, 0)),
                  pl.BlockSpec((None, D_MODEL, FF_BLK), lambda i, c: (layer, 0, c)),
                  pl.BlockSpec((None, D_MODEL, FF_BLK), lambda i, c: (layer, 0, nc + c)),
                  pl.BlockSpec((None, FF_BLK, D_MODEL), lambda i, c: (layer, c, 0))],
        out_specs=pl.BlockSpec((tm, D_MODEL), lambda i, c: (i, 0)),
        out_shape=jax.ShapeDtypeStruct((t, D_MODEL), F32),
        scratch_shapes=[pltpu.VMEM((tm, D_MODEL), BF16), pltpu.VMEM((tm, D_MODEL), F32)],
        compiler_params=_cparams("parallel", "arbitrary"),
        name="swiglu",
    )(x, g, w_in, w_in, w_out)


def _mem_heads(q, key_head, value_head):
    outs = []
    for h in range(MEM_HEADS):
        sl = slice(h * MEM_HEAD_DIM, (h + 1) * MEM_HEAD_DIM)
        s = _dot_nt(q[:, sl].astype(BF16), key_head(h))
        m = jnp.max(s, -1, keepdims=True)
        p = jnp.exp(s - m)
        l = jnp.sum(p, -1, keepdims=True)
        outs.append(_dot(p.astype(BF16), value_head(h)) / l)
    return jnp.concatenate(outs, -1)


def _mem_attn_kernel(x_ref, g_ref, wq_ref, qg_ref, mk_ref, mv_ref, wo_ref, o_ref):
    x = x_ref[...]
    q = _dot(_rms(x, g_ref[...]).astype(BF16), wq_ref[...])
    q = _headnorm(q, qg_ref[...], MEM_HEAD_DIM) * (MEM_HEAD_DIM ** -0.5)
    head = lambda ref: lambda h: ref[:, h * MEM_HEAD_DIM:(h + 1) * MEM_HEAD_DIM].astype(BF16)
    o = _mem_heads(q, head(mk_ref), head(mv_ref))
    o_ref[...] = x + _dot(o.astype(BF16), wo_ref[...])


def mem_attn_prompt(x, g, w_q, q_gain, mkv, layer, w_o, batch, tm):
    t = x.shape[0]
    per_b = t // batch // tm
    wspec = pl.BlockSpec((None, D_MODEL, D_MODEL), lambda b, i: (layer, 0, 0))
    return pl.pallas_call(
        _mem_attn_kernel,
        grid=(batch, per_b),
        in_specs=[pl.BlockSpec((tm, D_MODEL), lambda b, i: (b * per_b + i, 0)),
                  pl.BlockSpec((1, D_MODEL), lambda b, i: (0, 0)),
                  wspec,
                  pl.BlockSpec((1, D_MODEL), lambda b, i: (0, 0)),
                  pl.BlockSpec((None, N_MEM, D_MODEL), lambda b, i: (layer, b, 0)),
                  pl.BlockSpec((None, N_MEM, D_MODEL), lambda b, i: (layer, b, 1)),
                  wspec],
        out_specs=pl.BlockSpec((tm, D_MODEL), lambda b, i: (b * per_b + i, 0)),
        out_shape=jax.ShapeDtypeStruct((t, D_MODEL), F32),
        compiler_params=_cparams("parallel", "arbitrary"),
        name="mem_attn_prompt",
    )(x, g, w_q, q_gain, mkv, mkv, w_o)


def _mem_attn_sample_kernel(x_ref, g_ref, wq_ref, qg_ref, mk_ref, mv_ref, wo_ref, o_ref,
                            q_scr, o_scr):
    b = pl.program_id(0)

    @pl.when(b == 0)
    def _():
        q = _dot(_rms(x_ref[...], g_ref[...]).astype(BF16), wq_ref[...])
        q_scr[...] = _headnorm(q, qg_ref[...], MEM_HEAD_DIM) * (MEM_HEAD_DIM ** -0.5)

    rows = pl.ds(pl.multiple_of(b * SAMPLE_ROWS, SAMPLE_ROWS), SAMPLE_ROWS)
    head = lambda ref: lambda h: ref[:, h, :].astype(BF16)
    o_scr[rows, :] = _mem_heads(q_scr[rows, :], head(mk_ref), head(mv_ref))

    @pl.when(b == pl.num_programs(0) - 1)
    def _():
        o_ref[...] = x_ref[...] + _dot(o_scr[...].astype(BF16), wo_ref[...])


def mem_attn_sample(x, g, w_q, q_gain, cache_k, cache_v, layer, w_o, batch):
    t = x.shape[0]
    const = lambda b: (0, 0)
    wspec = pl.BlockSpec((None, D_MODEL, D_MODEL), lambda b: (layer, 0, 0))
    cspec = pl.BlockSpec((None, None, N_MEM, MEM_HEADS, MEM_HEAD_DIM), lambda b: (layer, b, 0, 0, 0))
    return pl.pallas_call(
        _mem_attn_sample_kernel,
        grid=(batch,),
        in_specs=[pl.BlockSpec((t, D_MODEL), const),
                  pl.BlockSpec((1, D_MODEL), const),
                  wspec,
                  pl.BlockSpec((1, D_MODEL), const),
                  cspec, cspec, wspec],
        out_specs=pl.BlockSpec((t, D_MODEL), const),
        out_shape=jax.ShapeDtypeStruct((t, D_MODEL), F32),
        scratch_shapes=[pltpu.VMEM((t, D_MODEL), F32), pltpu.VMEM((t, D_MODEL), F32)],
        compiler_params=_cparams("arbitrary"),
        name="mem_attn_sample",
    )(x, g, w_q, q_gain, cache_k, cache_v, w_o)


def _swa_prompt_kernel(q_ref, kp_ref, kc_ref, vp_ref, vc_ref, o_ref, l_ref):
    i = pl.program_id(2)
    q = (q_ref[...] * (SWA_HEAD_DIM ** -0.5)).astype(BF16)
    k = jnp.concatenate([kp_ref[...], kc_ref[...]], 0).astype(BF16)
    v = jnp.concatenate([vp_ref[...], vc_ref[...]], 0).astype(BF16)
    qq = lax.broadcasted_iota(jnp.int32, (Q_BLK, 2 * Q_BLK), 0)
    kk = lax.broadcasted_iota(jnp.int32, (Q_BLK, 2 * Q_BLK), 1)
    valid = (kk >= qq) & (kk <= qq + SWA_KEYS_BACK) & ((kk >= Q_BLK) | (i > 0))
    outs, lses = [], []
    for h in range(SWA_HEADS):
        sl = slice(h * SWA_HEAD_DIM, (h + 1) * SWA_HEAD_DIM)
        s = jnp.where(valid, _dot_nt(q[:, sl], k[:, sl]), NEG_BIG)
        m = jnp.max(s, -1, keepdims=True)
        p = jnp.exp(s - m)
        l = jnp.sum(p, -1, keepdims=True)
        outs.append(_dot(p.astype(BF16), v[:, sl]) / l)
        lses.append(jnp.broadcast_to(m + jnp.log(l), (Q_BLK, SWA_HEAD_DIM)))
    o_ref[...] = jnp.concatenate(outs, -1)
    l_ref[...] = jnp.concatenate(lses, -1)


def swa_prompt(qkv, group):
    batch, dil, rows, _ = qkv.shape
    blk = (None, None, Q_BLK, SWA_WIDTH)
    prev = lambda i: jnp.maximum(i - 1, 0)
    q_spec = pl.BlockSpec(blk, lambda b, r, i: (b, r, i, 0))
    kp_spec = pl.BlockSpec(blk, lambda b, r, i: (b, r, prev(i), 1))
    kc_spec = pl.BlockSpec(blk, lambda b, r, i: (b, r, i, 1))
    vp_spec = pl.BlockSpec(blk, lambda b, r, i: (b, r, prev(i), 2))
    vc_spec = pl.BlockSpec(blk, lambda b, r, i: (b, r, i, 2))
    shape = jax.ShapeDtypeStruct((batch, dil, rows, SWA_WIDTH), F32)
    return pl.pallas_call(
        _swa_prompt_kernel,
        grid=(batch, dil, rows // Q_BLK),
        in_specs=[q_spec, kp_spec, kc_spec, vp_spec, vc_spec],
        out_specs=[q_spec, q_spec],
        out_shape=[shape, shape],
        compiler_params=_cparams("parallel", "parallel", "arbitrary"),
        name="swa_prompt_g%d" % group,
    )(qkv, qkv, qkv, qkv, qkv)


ROWS_BLK = 64


def _swa_rows_kernel(*refs, dil, n_layers):
    srcs, (ko_ref, vo_ref) = refs[:2 * n_layers], refs[2 * n_layers:]
    layer = pl.program_id(0)
    for li in range(n_layers):
        @pl.when(layer == li)
        def _(li=li):
            for src, dst in ((srcs[2 * li], ko_ref), (srcs[2 * li + 1], vo_ref)):
                for r in range(dil):
                    for h in range(SWA_HEADS):
                        dst[pl.ds(r, ROWS_BLK, stride=dil), h, :] = (
                            src[r, :, h * SWA_HEAD_DIM:(h + 1) * SWA_HEAD_DIM])


def swa_last_rows(qkvs):
    n_layers = len(qkvs)
    batch, dil, rows, _ = qkvs[0].shape
    keep = SWA_KEYS_BACK * dil
    nb = SWA_KEYS_BACK // ROWS_BLK
    first = (rows - SWA_KEYS_BACK) // ROWS_BLK
    in_specs, args = [], []
    for qkv in qkvs:
        for kind in (1, 2):
            in_specs.append(pl.BlockSpec((None, dil, ROWS_BLK, SWA_WIDTH),
                                         lambda l, b, a, kind=kind: (b, 0, first + a, kind)))
            args.append(qkv)
    o_spec = pl.BlockSpec((None, None, ROWS_BLK * dil, SWA_HEADS, SWA_HEAD_DIM), lambda l, b, a: (l, b, a, 0, 0))
    shape = jax.ShapeDtypeStruct((n_layers, batch, keep, SWA_HEADS, SWA_HEAD_DIM), F32)
    return pl.pallas_call(
        functools.partial(_swa_rows_kernel, dil=dil, n_layers=n_layers),
        grid=(n_layers, batch, nb),
        in_specs=in_specs,
        out_specs=[o_spec, o_spec],
        out_shape=[shape, shape],
        compiler_params=_cparams("parallel", "parallel", "parallel"),
        name="swa_last_rows_d%d" % dil,
    )(*args)


def _swa_sample_kernel(q_ref, kn_ref, vn_ref, kc_ref, vc_ref, o_ref, l_ref, *, dil, n_new, res_major):
    nq = SAMPLE_ROWS
    q = (q_ref[...] * (SWA_HEAD_DIM ** -0.5)).astype(BF16)
    kn, vn = kn_ref[...].astype(BF16), vn_ref[...].astype(BF16)

    def qi(n):
        return lax.broadcasted_iota(jnp.int32, (nq, n), 0)

    def col(n):
        return lax.broadcasted_iota(jnp.int32, (nq, n), 1)

    new_valid = (col(nq) < n_new) & (col(nq) <= qi(nq)) & ((qi(nq) - col(nq)) % dil == 0)
    outs, lses = [], []
    for h in range(SWA_HEADS):
        sl = slice(h * SWA_HEAD_DIM, (h + 1) * SWA_HEAD_DIM)
        parts = [(kn[:, sl], vn[:, sl], new_valid)]
        if res_major:
            npos = kc_ref.shape[0]
            for c in range(n_new):
                parts.append((kc_ref[:, c * SWA_HEADS + h, :].astype(BF16),
                              vc_ref[:, c * SWA_HEADS + h, :].astype(BF16), qi(npos) == c))
        else:
            rows = kc_ref.shape[0] // SWA_HEADS
            rsl = pl.ds(h, rows, stride=SWA_HEADS)
            parts.append((kc_ref[rsl, :].astype(BF16), vc_ref[rsl, :].astype(BF16),
                          (col(rows) >= qi(rows)) & ((col(rows) - qi(rows)) % dil == 0)))
        scores = [jnp.where(valid, _dot_nt(q[:, sl], kp), NEG_BIG) for kp, _, valid in parts]
        m = scores[0].max(-1, keepdims=True)
        for s in scores[1:]:
            m = jnp.maximum(m, s.max(-1, keepdims=True))
        l = jnp.zeros_like(m)
        acc = jnp.zeros((nq, SWA_HEAD_DIM), F32)
        for s, (_, vp, _) in zip(scores, parts):
            p = jnp.exp(s - m)
            l = l + p.sum(-1, keepdims=True)
            acc = acc + _dot(p.astype(BF16), vp)
        outs.append(acc / l)
        lses.append(jnp.broadcast_to(m + jnp.log(l), (nq, SWA_HEAD_DIM)))
    o_ref[...] = jnp.concatenate(outs, -1)
    l_ref[...] = jnp.concatenate(lses, -1)


def swa_sample(qkv, cache_k, cache_v, layer, group, n_new):
    batch = qkv.shape[0]
    n_layers, _, rows, heads, hd = cache_k.shape
    dil = SWA_GROUPS[group][1]
    blk = (None, SAMPLE_ROWS, SWA_WIDTH)
    res_major = dil >= 2 * n_new
    if res_major:
        view = (n_layers, batch, rows // dil, dil * heads, hd)
        c_spec = pl.BlockSpec((None, None, rows // dil, n_new * heads, hd), lambda b: (layer, b, 0, 0, 0))
    else:
        view = (n_layers, batch, rows * heads, hd)
        c_spec = pl.BlockSpec((None, None, rows * heads, hd), lambda b: (layer, b, 0, 0))
    shape = jax.ShapeDtypeStruct((batch, SAMPLE_ROWS, SWA_WIDTH), F32)
    return pl.pallas_call(
        functools.partial(_swa_sample_kernel, dil=dil, n_new=n_new, res_major=res_major),
        grid=(batch,),
        in_specs=[pl.BlockSpec(blk, lambda b: (b, 0, 0)),
                  pl.BlockSpec(blk, lambda b: (b, 0, 1)),
                  pl.BlockSpec(blk, lambda b: (b, 0, 2)),
                  c_spec, c_spec],
        out_specs=[pl.BlockSpec(blk, lambda b: (b, 0, 0))] * 2,
        out_shape=[shape, shape],
        compiler_params=_cparams("parallel"),
        name="swa_sample_g%d" % group,
    )(qkv, qkv, qkv, cache_k.reshape(view), cache_v.reshape(view))


def _attn_out_kernel(x_ref, o0, o1, o2, l0, l1, l2, w_ref, out_ref, *scr, dils):
    tm = x_ref.shape[0]
    scr = list(scr)

    def rows(ref, dil):
        if dil == 1:
            return ref[0]
        buf = scr.pop()
        for c in range(buf.shape[0]):
            for r in range(dil):
                buf[c, pl.ds(r, tm // dil, stride=dil), :] = ref[r, :, c * LANES:(c + 1) * LANES]
        return jnp.concatenate([buf[c] for c in range(buf.shape[0])], -1)

    a, b, c = rows(l0, dils[0]), rows(l1, dils[1]), rows(l2, dils[2])
    m = jnp.maximum(jnp.maximum(a, b), c)
    ea, eb, ec = jnp.exp(a - m), jnp.exp(b - m), jnp.exp(c - m)
    o = (ea * rows(o0, dils[0]) + eb * rows(o1, dils[1]) + ec * rows(o2, dils[2])) / (ea + eb + ec)
    out_ref[...] = x_ref[...] + _dot(o.astype(BF16), w_ref[...])


def attn_out(x, outs, lses, w_o, layer, batch, tm):
    t = x.shape[0]
    per_b = t // batch // tm
    dils = tuple(o.shape[1] for o in outs)
    specs = [pl.BlockSpec((None, d, tm // d, SWA_WIDTH), lambda i: (i // per_b, 0, i % per_b, 0)) for d in dils]
    n_scr = 2 * sum(d > 1 for d in dils)
    return pl.pallas_call(
        functools.partial(_attn_out_kernel, dils=dils),
        grid=(t // tm,),
        in_specs=[pl.BlockSpec((tm, D_MODEL), lambda i: (i, 0))] + specs + specs
                 + [pl.BlockSpec((None, SWA_WIDTH, D_MODEL), lambda i: (layer, 0, 0))],
        out_specs=pl.BlockSpec((tm, D_MODEL), lambda i: (i, 0)),
        out_shape=jax.ShapeDtypeStruct((t, D_MODEL), F32),
        scratch_shapes=[pltpu.VMEM((SWA_WIDTH // LANES, tm, LANES), F32)] * n_scr,
        compiler_params=_cparams("parallel"),
        name="attn_out",
    )(x, *outs, *lses, w_o)


SCAN_PASSES = 8


def _ssm_param_kernel(lr_ref, li_ref, ls_ref, btr_ref, bti_ref, cr_ref, ci_ref,
                      win_ref, wout_ref, kt_ref, aqr_ref, aqi_ref, e_scr, *, q):
    g = SSM_GROUP
    lr, li = lr_ref[...], li_ref[...]
    step = jnp.exp(ls_ref[...])
    lo = lax.broadcasted_iota(jnp.int32, (1, LANES), 1) < SSM_STATE

    def powers(tau):
        mag = jnp.exp(tau * (lr * step))
        ang = tau * (li * step)
        return mag * jnp.cos(ang), mag * jnp.sin(ang)

    tau = lax.broadcasted_iota(jnp.int32, (q + 1, LANES), 0).astype(F32)
    c2, s2 = powers(tau)
    ar, ai = c2[1:2], s2[1:2]
    den = lr * lr + li * li
    fr = ((ar - 1.0) * lr + ai * li) / den
    fi = (ai * lr - (ar - 1.0) * li) / den
    btr, bti = btr_ref[...], bti_ref[...]
    bbr = fr * btr - fi * bti
    bbi = fr * bti + fi * btr
    bba = jnp.where(lo, bbr, bbi)
    bbb = jnp.where(lo, -bbi, bbr)
    pa = jnp.where(lo, c2, -s2)
    pb = jnp.where(lo, s2, c2)
    cr, ci = cr_ref[...], ci_ref[...]
    for t in range(q + 1):
        e_scr[t * g:(t + 1) * g, :] = cr * pa[t:t + 1] - ci * pb[t:t + 1]
    wout_ref[...] = e_scr[g:(q + 1) * g, :].astype(BF16)
    kt_ref[...] = _dot_nt(bba, e_scr[0:q * g, :], precision=lax.Precision.HIGHEST)
    c2r, s2r = powers((q - 1.0) - tau[0:q])
    for s in range(q):
        win_ref[s * g:(s + 1) * g, :] = (c2r[s:s + 1] * bba + s2r[s:s + 1] * bbb).astype(BF16)
    pw = lax.broadcasted_iota(jnp.int32, (SCAN_PASSES, LANES), 0)
    cq, sq = powers((q * jnp.left_shift(1, pw)).astype(F32))
    aqr_ref[...] = cq
    aqi_ref[...] = jnp.where(lo, -sq, sq)


def ssm_params(lam_re, lam_im, log_step, b_re, b_im, c_re, c_im, q):
    gr, st, g = SSM_GROUPS, SSM_STATE, SSM_GROUP
    dup = lambda a: jnp.concatenate([a, a], -1)
    lr = dup(lam_re).reshape(gr, 1, LANES)
    li = dup(lam_im).reshape(gr, 1, LANES)
    ls = jnp.broadcast_to(log_step[:, None, None], (gr, 1, LANES))
    btr = dup(jnp.swapaxes(b_re, 1, 2))
    bti = dup(jnp.swapaxes(b_im, 1, 2))
    cr, ci = dup(c_re), dup(c_im)
    vec = pl.BlockSpec((None, 1, LANES), lambda i: (i, 0, 0))
    mat = pl.BlockSpec((None, g, LANES), lambda i: (i, 0, 0))
    big = pl.BlockSpec((None, q * g, LANES), lambda i: (i, 0, 0))
    pws = pl.BlockSpec((None, SCAN_PASSES, LANES), lambda i: (i, 0, 0))
    win, wout, kt, aqr, aqi = pl.pallas_call(
        functools.partial(_ssm_param_kernel, q=q),
        grid=(gr,),
        in_specs=[vec, vec, vec, mat, mat, mat, mat],
        out_specs=[big, big, pl.BlockSpec((None, g, q * g), lambda i: (i, 0, 0)), pws, pws],
        out_shape=[jax.ShapeDtypeStruct((gr, q * g, LANES), BF16),
                   jax.ShapeDtypeStruct((gr, q * g, LANES), BF16),
                   jax.ShapeDtypeStruct((gr, g, q * g), F32),
                   jax.ShapeDtypeStruct((gr, SCAN_PASSES, LANES), F32),
                   jax.ShapeDtypeStruct((gr, SCAN_PASSES, LANES), F32)],
        scratch_shapes=[pltpu.VMEM(((q + 1) * g, LANES), F32)],
        compiler_params=_cparams("parallel"),
        name="ssm_params_q%d" % q,
    )(lr, li, ls, btr, bti, cr, ci)
    kt4 = kt.reshape(gr, g, q, g)
    lag = jnp.arange(q)[None, :] - jnp.arange(q)[:, None]
    tt = jnp.where((lag >= 0)[None, None, :, :, None], kt4[:, :, jnp.maximum(lag, 0), :], 0.0)
    tt = jnp.transpose(tt, (0, 2, 1, 3, 4)).reshape(gr, q * g, q * g).astype(BF16)
    return win, wout, tt, aqr, aqi


def _ssm_in_kernel(u_ref, win_ref, s_ref):
    s_ref[...] = _dot(u_ref[...], win_ref[...])


def _ssm_scan_kernel(s_ref, h0_ref, ar_ref, ai_ref, hp_ref, hf_ref, h_scr):
    @pl.when(pl.program_id(0) == 0)
    def _():
        h_scr[...] = h0_ref[...]

    def body(k, h):
        hp_ref[k] = h.astype(BF16)
        return ar_ref[...] * h + ai_ref[...] * pltpu.roll(h, SSM_STATE, 1) + s_ref[k]

    h = lax.fori_loop(0, s_ref.shape[0], body, h_scr[...])
    h_scr[...] = h
    hf_ref[...] = h


def _ssm_out_kernel(u_ref, tt_ref, hp_ref, wout_ref, y_ref):
    y_ref[...] = _dot(u_ref[...], tt_ref[...]) + _dot_nt(hp_ref[...], wout_ref[...])


def ssm_mix(h_bf16, h0_re, h0_im, params, batch, seq, q):
    win, wout, tt, aqr, aqi = params
    gr, st, g = SSM_GROUPS, SSM_STATE, SSM_GROUP
    nk = seq // q
    n = nk * batch
    qg = q * g
    u = h_bf16.reshape(batch, nk, q, gr, g).transpose(3, 1, 0, 2, 4).reshape(gr, n, qg)
    s = pl.pallas_call(
        _ssm_in_kernel,
        grid=(gr,),
        in_specs=[pl.BlockSpec((None, n, qg), lambda i: (i, 0, 0)),
                  pl.BlockSpec((None, qg, LANES), lambda i: (i, 0, 0))],
        out_specs=pl.BlockSpec((n, LANES), lambda i: (0, i)),
        out_shape=jax.ShapeDtypeStruct((n, gr * LANES), F32),
        compiler_params=_cparams("parallel"),
        name="ssm_in_q%d" % q,
    )(u, win)
    rows = batch * gr
    kc = min(nk, 32)
    tile = lambda a: jnp.broadcast_to(a[:, 0].reshape(1, gr, LANES), (batch, gr, LANES)).reshape(rows, LANES)
    h0 = jnp.concatenate([h0_re, h0_im], -1).reshape(rows, LANES)
    full = pl.BlockSpec((rows, LANES), lambda i: (0, 0))
    hp, hf = pl.pallas_call(
        _ssm_scan_kernel,
        grid=(nk // kc,),
        in_specs=[pl.BlockSpec((kc, rows, LANES), lambda i: (i, 0, 0)), full, full, full],
        out_specs=[pl.BlockSpec((kc, rows, LANES), lambda i: (i, 0, 0)), full],
        out_shape=[jax.ShapeDtypeStruct((nk, rows, LANES), BF16),
                   jax.ShapeDtypeStruct((rows, LANES), F32)],
        scratch_shapes=[pltpu.VMEM((rows, LANES), F32)],
        compiler_params=_cparams("arbitrary"),
        name="ssm_scan_q%d" % q,
    )(s.reshape(nk, rows, LANES), h0, tile(aqr), tile(aqi))
    y = pl.pallas_call(
        _ssm_out_kernel,
        grid=(gr,),
        in_specs=[pl.BlockSpec((None, n, qg), lambda i: (i, 0, 0)),
                  pl.BlockSpec((None, qg, qg), lambda i: (i, 0, 0)),
                  pl.BlockSpec((n, LANES), lambda i: (0, i)),
                  pl.BlockSpec((None, qg, LANES), lambda i: (i, 0, 0))],
        out_specs=pl.BlockSpec((None, n, qg), lambda i: (i, 0, 0)),
        out_shape=jax.ShapeDtypeStruct((gr, n, qg), F32),
        compiler_params=_cparams("parallel"),
        name="ssm_out_q%d" % q,
    )(u, tt, hp.reshape(n, gr * LANES), wout)
    y = y.reshape(gr, nk, batch, q, g).transpose(2, 1, 3, 0, 4).reshape(batch * seq, D_MODEL)
    hf = hf.reshape(batch, gr, LANES)
    return y, hf[..., :st], hf[..., st:]


SSM_TOKENS_BLK = 2048


GROUPS_PER_TILE = LANES // SSM_GROUP
N_LANE_TILES = D_MODEL // LANES


def _to_chunks_kernel(*refs, q):
    x_refs, (g_ref, u_ref, r_scr, ut_scr) = refs[:N_LANE_TILES], refs[N_LANE_TILES:]
    tb = r_scr.shape[0]
    nkb = tb // q
    ss = jnp.zeros((tb, 1), F32)
    for x_ref in x_refs:
        x = x_ref[...]
        ss = ss + jnp.sum(x * x, -1, keepdims=True)
    r_scr[...] = jnp.broadcast_to(lax.rsqrt(ss * (1.0 / D_MODEL) + RMS_EPS), (tb, LANES))
    for c, x_ref in enumerate(x_refs):
        gain = g_ref[:, c * LANES:(c + 1) * LANES]
        for s in range(q):
            rows = pl.ds(s, nkb, stride=q)
            m = x_ref[rows, :] * r_scr[rows, :] * gain
            ut_scr[:, s * SSM_GROUP:(s + 1) * SSM_GROUP, :] = m.T.reshape(GROUPS_PER_TILE, SSM_GROUP, nkb)
        for grp in range(GROUPS_PER_TILE):
            u_ref[c * GROUPS_PER_TILE + grp] = ut_scr[grp].T.astype(BF16)


def _from_chunks_kernel(y_ref, o_ref, yt_scr, o_scr, *, q):
    nkb = o_ref.shape[0] // q
    for c in range(N_LANE_TILES):
        for grp in range(GROUPS_PER_TILE):
            yt_scr[grp] = y_ref[c * GROUPS_PER_TILE + grp].T
        for t in range(q):
            o_scr[pl.ds(t, nkb, stride=q), :] = yt_scr[:, t * SSM_GROUP:(t + 1) * SSM_GROUP, :].reshape(LANES, nkb).T
        o_ref[:, c * LANES:(c + 1) * LANES] = o_scr[...]


def _ssm_group_kernel(u_ref, win_ref, wout_ref, tt_ref, ar_ref, ai_ref, y_ref, hf_ref, h_scr, *, nk):
    u = u_ref[...]
    h = _dot(u, win_ref[...])
    n = h.shape[0]
    k_idx = lax.broadcasted_iota(jnp.int32, (n, LANES), 0) % nk
    d = 1
    for j in range(SCAN_PASSES):
        if d >= nk:
            break
        hs = jnp.where(k_idx >= d, pltpu.roll(h, d, 0), 0.0)
        h = h + ar_ref[j:j + 1, :] * hs + ai_ref[j:j + 1, :] * pltpu.roll(hs, SSM_STATE, 1)
        d *= 2
    hp = jnp.where(k_idx >= 1, pltpu.roll(h, 1, 0), 0.0)
    y_ref[...] = _dot(u, tt_ref[...]) + _dot_nt(hp.astype(BF16), wout_ref[...])
    h_scr[...] = h
    hf_ref[...] = h_scr[pl.ds(nk - 1, n // nk, stride=nk), :]


def ssm_mix_prompt(x, g_mix, params, batch, seq, q):
    win, wout, tt, aqr, aqi = params
    gr, st = SSM_GROUPS, SSM_STATE
    nk = seq // q
    assert nk <= 2 ** SCAN_PASSES
    n = nk * batch
    qg = q * SSM_GROUP
    tb = SSM_TOKENS_BLK
    nkb = tb // q
    per_b = seq // tb
    tok = pl.BlockSpec((tb, D_MODEL), lambda b, i: (b * per_b + i, 0))
    chk = pl.BlockSpec((gr, nkb, qg), lambda b, i: (0, b * per_b + i, 0))
    lane_tiles = [pl.BlockSpec((tb, LANES), lambda b, i, c=c: (b * per_b + i, c)) for c in range(N_LANE_TILES)]
    u = pl.pallas_call(
        functools.partial(_to_chunks_kernel, q=q),
        grid=(batch, per_b),
        in_specs=lane_tiles + [pl.BlockSpec((1, D_MODEL), lambda b, i: (0, 0))],
        out_specs=chk,
        out_shape=jax.ShapeDtypeStruct((gr, n, qg), BF16),
        scratch_shapes=[pltpu.VMEM((tb, LANES), F32), pltpu.VMEM((GROUPS_PER_TILE, qg, nkb), F32)],
        compiler_params=_cparams("parallel", "parallel"),
        name="ssm_to_chunks",
    )(*([x] * N_LANE_TILES), g_mix)
    per_g = lambda r, c: pl.BlockSpec((None, r, c), lambda i: (i, 0, 0))
    y, hf = pl.pallas_call(
        functools.partial(_ssm_group_kernel, nk=nk),
        grid=(gr,),
        in_specs=[per_g(n, qg), per_g(qg, LANES), per_g(qg, LANES), per_g(qg, qg),
                  per_g(SCAN_PASSES, LANES), per_g(SCAN_PASSES, LANES)],
        out_specs=[per_g(n, qg), per_g(batch, LANES)],
        out_shape=[jax.ShapeDtypeStruct((gr, n, qg), F32), jax.ShapeDtypeStruct((gr, batch, LANES), F32)],
        scratch_shapes=[pltpu.VMEM((n, LANES), F32)],
        compiler_params=_cparams("parallel"),
        name="ssm_group",
    )(u, win, wout, tt, aqr, aqi)
    y = pl.pallas_call(
        functools.partial(_from_chunks_kernel, q=q),
        grid=(batch, per_b),
        in_specs=[chk],
        out_specs=tok,
        out_shape=jax.ShapeDtypeStruct((batch * seq, D_MODEL), F32),
        scratch_shapes=[pltpu.VMEM((GROUPS_PER_TILE, qg, nkb), F32), pltpu.VMEM((tb, LANES), F32)],
        compiler_params=_cparams("parallel", "parallel"),
        name="ssm_from_chunks",
    )(y)
    hf = jnp.swapaxes(hf, 0, 1)
    return y, hf[..., :st], hf[..., st:]


def _ssm_glu_kernel(x_ref, gm_ref, y_ref, d_ref, w_ref, b_ref, o_ref):
    x = x_ref[...]
    y = y_ref[...] + d_ref[...] * _rms(x, gm_ref[...])
    gl = _gelu_tanh(y)
    z = _dot(gl.astype(BF16), w_ref[...]) + b_ref[...]
    o_ref[...] = x + gl * _sigmoid(z)


def ssm_glu(x, g_mix, y, d_skip, w_glu, layer, b_glu, tm):
    t = x.shape[0]
    row = pl.BlockSpec((tm, D_MODEL), lambda i: (i, 0))
    vec = pl.BlockSpec((1, D_MODEL), lambda i: (0, 0))
    return pl.pallas_call(
        _ssm_glu_kernel,
        grid=(t // tm,),
        in_specs=[row, vec, row, vec, pl.BlockSpec((None, D_MODEL, D_MODEL), lambda i: (layer, 0, 0)), vec],
        out_specs=row,
        out_shape=jax.ShapeDtypeStruct((t, D_MODEL), F32),
        compiler_params=_cparams("parallel"),
        name="ssm_glu",
    )(x, g_mix, y, d_skip, w_glu, b_glu)


def _row(v):
    return v.reshape(1, -1).astype(F32)


def kernel(x_prompt, x_sample, mem_prompt, state_ssm_re, state_ssm_im, cache_swa0_k, cache_swa0_v, cache_swa1_k, cache_swa1_v, cache_swa2_k, cache_swa2_v, cache_mem_k, cache_mem_v, norm_mix_g, norm_mem_g, norm_memin_g, norm_ffn_g, ssm_lambda_re, ssm_lambda_im, ssm_b_re, ssm_b_im, ssm_c_re, ssm_c_im, ssm_d, ssm_log_step, ssm_w_glu, ssm_b_glu, attn_w_qkv, attn_q_norm_g, attn_k_norm_g, attn_w_o, mem_w_q, mem_w_kv, mem_q_norm_g, mem_k_norm_g, mem_w_o, ffn_w_in, ffn_w_out):
    pb, seq, _ = x_prompt.shape
    sb, dec, _ = x_sample.shape
    n_ssm = state_ssm_re.shape[0]
    n_attn = cache_swa0_k.shape[0]
    tm_p = 512
    tm_s = sb * SAMPLE_ROWS

    w_glu = ssm_w_glu.astype(BF16)
    w_qkv = attn_w_qkv.reshape(n_attn, D_MODEL, 3, N_SWA, SWA_WIDTH).transpose(0, 1, 3, 2, 4)
    w_qkv = w_qkv.reshape(n_attn, D_MODEL, QKV_WIDTH).astype(BF16)
    w_ao = attn_w_o.astype(BF16)
    w_mq = mem_w_q.astype(BF16)
    w_mkv = mem_w_kv.astype(BF16)
    w_mo = mem_w_o.astype(BF16)
    w_fi = ffn_w_in.astype(BF16)
    w_fo = ffn_w_out.astype(BF16)

    head_gain = [jnp.concatenate([jnp.tile(attn_q_norm_g[j], (1, SWA_HEADS)),
                                  jnp.tile(attn_k_norm_g[j], (1, SWA_HEADS))], 1).reshape(N_SWA, 1, 2 * SWA_WIDTH)
                 for j in range(n_attn)]
    mem_q_gain = [_row(jnp.tile(mem_q_norm_g[i], MEM_HEADS)) for i in range(DEPTH)]
    mem_k_gain = jnp.tile(mem_k_norm_g, (1, MEM_HEADS)).reshape(DEPTH, 1, D_MODEL)

    mkv = mem_kv(mem_prompt.reshape(pb * N_MEM, D_MODEL), norm_memin_g.reshape(DEPTH, 1, D_MODEL),
                 w_mkv, mem_k_gain)
    mkv5 = mkv.reshape(DEPTH, pb, N_MEM, 2, MEM_HEADS, MEM_HEAD_DIM)
    p_mem_k, p_mem_v = mkv5[:, :, :, 0], mkv5[:, :, :, 1]
    mkv_b = mkv.reshape(DEPTH, pb * N_MEM, 2 * D_MODEL)

    ssm_p = [[ssm_params(ssm_lambda_re[j], ssm_lambda_im[j], ssm_log_step[j], ssm_b_re[j], ssm_b_im[j],
                         ssm_c_re[j], ssm_c_im[j], q) for q in (SSM_Q_PROMPT, dec)] for j in range(n_ssm)]

    caches_k = (cache_swa0_k, cache_swa1_k, cache_swa2_k)
    caches_v = (cache_swa0_v, cache_swa1_v, cache_swa2_v)
    dils = tuple(d for _, d in SWA_GROUPS)
    assert all(min(w, seq) == SWA_KEYS_BACK * d for w, d in SWA_GROUPS)

    xp = x_prompt.reshape(pb * seq, D_MODEL)
    xs = jnp.pad(x_sample, ((0, 0), (0, SAMPLE_ROWS - dec), (0, 0))).reshape(tm_s, D_MODEL)

    p_ssm_re, p_ssm_im, s_ssm_re, s_ssm_im = [], [], [], []
    p_qkv = [[] for _ in SWA_GROUPS]
    s_swa_k = [[] for _ in SWA_GROUPS]
    s_swa_v = [[] for _ in SWA_GROUPS]

    for i in range(DEPTH):
        j = i // 2
        g_mix = _row(norm_mix_g[i])
        if i % 2 == 0:
            d_skip, b_glu = _row(ssm_d[j]), _row(ssm_b_glu[j])
            y, fr, fi = ssm_mix_prompt(xp, g_mix, ssm_p[j][0], pb, seq, SSM_Q_PROMPT)
            p_ssm_re.append(fr)
            p_ssm_im.append(fi)
            xp = ssm_glu(xp, g_mix, y, d_skip, w_glu, j, b_glu, tm_p)
            hb = norm_cast(xs, g_mix, tm_s).reshape(sb, SAMPLE_ROWS, D_MODEL)[:, :dec].reshape(sb * dec, D_MODEL)
            y, fr, fi = ssm_mix(hb, state_ssm_re[j], state_ssm_im[j], ssm_p[j][1], sb, dec, dec)
            s_ssm_re.append(fr)
            s_ssm_im.append(fi)
            y = jnp.pad(y.reshape(sb, dec, D_MODEL), ((0, 0), (0, SAMPLE_ROWS - dec), (0, 0))).reshape(tm_s, D_MODEL)
            xs = ssm_glu(xs, g_mix, y, d_skip, w_glu, j, b_glu, tm_s)
        else:
            qkvs = qkv_proj(xp, g_mix, w_qkv, j, head_gain[j], pb, dils, tm_p)
            outs, lses = [], []
            for g in range(N_SWA):
                p_qkv[g].append(qkvs[g])
                o, lse = swa_prompt(qkvs[g], g)
                outs.append(o)
                lses.append(lse)
            xp = attn_out(xp, outs, lses, w_ao, j, pb, tm_p)
            qkvs = qkv_proj(xs, g_mix, w_qkv, j, head_gain[j], 1, (1,) * N_SWA, tm_s)
            outs, lses = [], []
            for g in range(N_SWA):
                qkv3 = qkvs[g].reshape(sb, SAMPLE_ROWS, 3 * SWA_WIDTH)
                qkv5 = qkv3.reshape(sb, SAMPLE_ROWS, 3, SWA_HEADS, SWA_HEAD_DIM)
                s_swa_k[g].append(qkv5[:, :dec, 1])
                s_swa_v[g].append(qkv5[:, :dec, 2])
                o, lse = swa_sample(qkv3, caches_k[g], caches_v[g], j, g, dec)
                outs.append(o.reshape(1, 1, tm_s, SWA_WIDTH))
                lses.append(lse.reshape(1, 1, tm_s, SWA_WIDTH))
            xs = attn_out(xs, outs, lses, w_ao, j, 1, tm_s)

        g_mem = _row(norm_mem_g[i])
        xp = mem_attn_prompt(xp, g_mem, w_mq, mem_q_gain[i], mkv_b, i, w_mo, pb, tm_p)
        xs = mem_attn_sample(xs, g_mem, w_mq, mem_q_gain[i], cache_mem_k, cache_mem_v, i, w_mo, sb)
        g_ffn = _row(norm_ffn_g[i])
        xp = swiglu_block(xp, g_ffn, w_fi, w_fo, i, 1024)
        xs = swiglu_block(xs, g_ffn, w_fi, w_fo, i, tm_s)

    p_swa = [swa_last_rows(p_qkv[g]) for g in range(N_SWA)]
    y_prompt = xp.reshape(pb, seq, D_MODEL)
    y_sample = xs.reshape(sb, SAMPLE_ROWS, D_MODEL)[:, :dec]
    st = lambda a: jnp.stack(a, 0)
    return (y_prompt, y_sample,
            st(p_ssm_re), st(p_ssm_im),
            p_swa[0][0], p_swa[0][1], p_swa[1][0], p_swa[1][1], p_swa[2][0], p_swa[2][1],
            p_mem_k, p_mem_v,
            st(s_ssm_re), st(s_ssm_im),
            st(s_swa_k[0]), st(s_swa_v[0]), st(s_swa_k[1]), st(s_swa_v[1]), st(s_swa_k[2]), st(s_swa_v[2]))
```

```python
import functools
import math

import jax
import jax.numpy as jnp
from jax import lax
from jax.experimental import pallas as pl
from jax.experimental.pallas import tpu as pltpu

F32 = jnp.float32
BF16 = jnp.bfloat16

D_MODEL = 1024
DEPTH = 4
SSM_GROUP = 16
SSM_GROUPS = D_MODEL // SSM_GROUP
SSM_STATE = 64
SWA_GROUPS = ((128, 1), (512, 4), (2048, 16))
N_SWA = len(SWA_GROUPS)
SWA_HEADS = 8
SWA_HEAD_DIM = 64
SWA_WIDTH = SWA_HEADS * SWA_HEAD_DIM
QKV_WIDTH = 3 * N_SWA * SWA_WIDTH
SWA_KEYS_BACK = 128
N_MEM = 256
MEM_HEADS = 4
MEM_HEAD_DIM = D_MODEL // MEM_HEADS
D_FF = -(-8 * D_MODEL // (3 * 256)) * 256
RMS_EPS = 1e-6
NEG_BIG = -1e30

LANES = 128
SUBLANES = 8
VMEM_LIMIT = 48 * 1024 * 1024
Q_BLK = 128
FF_BLK = 256
SAMPLE_ROWS = SUBLANES
SSM_Q_PROMPT = 16


def _cparams(*sem):
    return pltpu.CompilerParams(dimension_semantics=sem, vmem_limit_bytes=VMEM_LIMIT)


def _dot(a, b):
    return jnp.dot(a, b, preferred_element_type=F32)


def _dot_nt(a, b, precision=None):
    return lax.dot_general(a, b, (((1,), (1,)), ((), ())), preferred_element_type=F32,
                           precision=precision)


def _rms(x, g):
    return x * lax.rsqrt(jnp.mean(x * x, -1, keepdims=True) + RMS_EPS) * g


def _headnorm(y, g_row, hd):
    tm, n = y.shape
    outs = []
    if hd % LANES == 0:
        for c in range(n // hd):
            yc = y[:, c * hd:(c + 1) * hd]
            ms = jnp.mean(yc * yc, -1, keepdims=True)
            outs.append(yc * lax.rsqrt(ms + RMS_EPS))
    else:
        assert 2 * hd == LANES
        lo = lax.broadcasted_iota(jnp.int32, (tm, LANES), 1) < hd
        for c in range(n // LANES):
            yc = y[:, c * LANES:(c + 1) * LANES]
            sq = yc * yc
            s_lo = jnp.sum(jnp.where(lo, sq, 0.0), -1, keepdims=True)
            s_hi = jnp.sum(jnp.where(lo, 0.0, sq), -1, keepdims=True)
            ms = jnp.where(lo, s_lo, s_hi) * (1.0 / hd)
            outs.append(yc * lax.rsqrt(ms + RMS_EPS))
    return jnp.concatenate(outs, -1) * g_row


def _sigmoid(x):
    return 1.0 / (1.0 + jnp.exp(-x))


def _gelu_tanh(x):
    return 0.5 * x * (1.0 + jnp.tanh(math.sqrt(2.0 / math.pi) * (x + 0.044715 * (x * x * x))))


def _norm_cast_kernel(x_ref, g_ref, o_ref):
    o_ref[...] = _rms(x_ref[...], g_ref[...]).astype(BF16)


def norm_cast(x, g, tm):
    t = x.shape[0]
    return pl.pallas_call(
        _norm_cast_kernel,
        grid=(t // tm,),
        in_specs=[pl.BlockSpec((tm, D_MODEL), lambda i: (i, 0)),
                  pl.BlockSpec((1, D_MODEL), lambda i: (0, 0))],
        out_specs=pl.BlockSpec((tm, D_MODEL), lambda i: (i, 0)),
        out_shape=jax.ShapeDtypeStruct((t, D_MODEL), BF16),
        compiler_params=_cparams("parallel"),
        name="norm_cast",
    )(x, g)


def _qkv_kernel(x_ref, g_ref, wq_ref, wk_ref, wv_ref, hg_ref, o0_ref, o1_ref, o2_ref, xn_ref, y_ref, *, dils):
    j = pl.program_id(1)

    @pl.when(j == 0)
    def _():
        xn_ref[...] = _rms(x_ref[...], g_ref[...]).astype(BF16)

    tm = x_ref.shape[0]
    w = SWA_WIDTH
    for grp, (o_ref, dil) in enumerate(zip((o0_ref, o1_ref, o2_ref), dils)):
        @pl.when(j == grp)
        def _(o_ref=o_ref, dil=dil):
            xn = xn_ref[...]
            parts = (_headnorm(_dot(xn, wq_ref[...]), hg_ref[:, :w], SWA_HEAD_DIM),
                     _headnorm(_dot(xn, wk_ref[...]), hg_ref[:, w:], SWA_HEAD_DIM),
                     _dot(xn, wv_ref[...]))
            if dil == 1:
                for kind, y in enumerate(parts):
                    o_ref[0, :, kind * w:(kind + 1) * w] = y
            else:
                per = w // LANES
                for c in range(y_ref.shape[0]):
                    y_ref[c] = parts[c // per][:, (c % per) * LANES:(c % per + 1) * LANES]
                    for r in range(dil):
                        o_ref[r, :, c * LANES:(c + 1) * LANES] = y_ref[c, pl.ds(r, tm // dil, stride=dil), :]


def qkv_proj(x, g, w, layer, head_gain, batch, dils, tm):
    t = x.shape[0]
    seq = t // batch
    per_b = seq // tm
    gw = 3 * SWA_WIDTH
    wspec = lambda kind: pl.BlockSpec((None, D_MODEL, SWA_WIDTH), lambda i, j: (layer, 0, kind * N_SWA + j))
    return pl.pallas_call(
        functools.partial(_qkv_kernel, dils=dils),
        grid=(t // tm, N_SWA),
        in_specs=[pl.BlockSpec((tm, D_MODEL), lambda i, j: (i, 0)),
                  pl.BlockSpec((1, D_MODEL), lambda i, j: (0, 0)),
                  wspec(0), wspec(1), wspec(2),
                  pl.BlockSpec((None, 1, 2 * SWA_WIDTH), lambda i, j: (j, 0, 0))],
        out_specs=[pl.BlockSpec((None, d, tm // d, gw), lambda i, j: (i // per_b, 0, i % per_b, 0))
                   for d in dils],
        out_shape=[jax.ShapeDtypeStruct((batch, d, seq // d, gw), F32) for d in dils],
        scratch_shapes=[pltpu.VMEM((tm, D_MODEL), BF16), pltpu.VMEM((gw // LANES, tm, LANES), F32)],
        compiler_params=_cparams("parallel", "arbitrary"),
        name="qkv_proj",
    )(x, g, w, w, w, head_gain)


def _mem_kv_kernel(x_ref, g_ref, w_ref, kg_ref, o_ref):
    j = pl.program_id(1)
    y = _dot(_rms(x_ref[...], g_ref[...]).astype(BF16), w_ref[...])

    @pl.when(j == 0)
    def _():
        o_ref[...] = _headnorm(y, kg_ref[...], MEM_HEAD_DIM)

    @pl.when(j == 1)
    def _():
        o_ref[...] = y


def mem_kv(mem, g_in, w_kv, k_gain):
    t = mem.shape[0]
    return pl.pallas_call(
        _mem_kv_kernel,
        grid=(DEPTH, 2),
        in_specs=[pl.BlockSpec((t, D_MODEL), lambda l, j: (0, 0)),
                  pl.BlockSpec((None, 1, D_MODEL), lambda l, j: (l, 0, 0)),
                  pl.BlockSpec((None, D_MODEL, D_MODEL), lambda l, j: (l, 0, j)),
                  pl.BlockSpec((None, 1, D_MODEL), lambda l, j: (l, 0, 0))],
        out_specs=pl.BlockSpec((None, t, D_MODEL), lambda l, j: (l, 0, j)),
        out_shape=jax.ShapeDtypeStruct((DEPTH, t, 2 * D_MODEL), F32),
        compiler_params=_cparams("parallel", "arbitrary"),
        name="mem_kv",
    )(mem, g_in, w_kv, k_gain)


def _swiglu_kernel(x_ref, g_ref, wg_ref, wu_ref, wo_ref, o_ref, xn_ref, acc_ref):
    c = pl.program_id(1)

    @pl.when(c == 0)
    def _():
        xn_ref[...] = _rms(x_ref[...], g_ref[...]).astype(BF16)
        acc_ref[...] = jnp.zeros_like(acc_ref)

    xn = xn_ref[...]
    gate = _dot(xn, wg_ref[...])
    up = _dot(xn, wu_ref[...])
    act = gate * _sigmoid(gate) * up
    acc_ref[...] += _dot(act.astype(BF16), wo_ref[...])

    @pl.when(c == pl.num_programs(1) - 1)
    def _():
        o_ref[...] = x_ref[...] + acc_ref[...]


def swiglu_block(x, g, w_in, w_out, layer, tm):
    t = x.shape[0]
    nc = D_FF // FF_BLK
    return pl.pallas_call(
        _swiglu_kernel,
        grid=(t // tm, nc),
        in_specs=[pl.BlockSpec((tm, D_MODEL), lambda i, c: (i, 0)),
                  pl.BlockSpec((1, D_MODEL), lambda i, c: (0, 0)),
                  pl.BlockSpec((None, D_MODEL, FF_BLK), lambda i, c: (layer, 0, c)),
                  pl.BlockSpec((None, D_MODEL, FF_BLK), lambda i, c: (layer, 0, nc + c)),
                  pl.BlockSpec((None, FF_BLK, D_MODEL), lambda i, c: (layer, c, 0))],
        out_specs=pl.BlockSpec((tm, D_MODEL), lambda i, c: (i, 0)),
        out_shape=jax.ShapeDtypeStruct((t, D_MODEL), F32),
        scratch_shapes=[pltpu.VMEM((tm, D_MODEL), BF16), pltpu.VMEM((tm, D_MODEL), F32)],
        compiler_params=_cparams("parallel", "arbitrary"),
        name="swiglu",
    )(x, g, w_in, w_in, w_out)


def _mem_heads(q, key_head, value_head):
    outs = []
    for h in range(MEM_HEADS):
        sl = slice(h * MEM_HEAD_DIM, (h + 1) * MEM_HEAD_DIM)
        s = _dot_nt(q[:, sl].astype(BF16), key_head(h))
        m = jnp.max(s, -1, keepdims=True)
        p = jnp.exp(s - m)
        l = jnp.sum(p, -1, keepdims=True)
        outs.append(_dot(p.astype(BF16), value_head(h)) / l)
    return jnp.concatenate(outs, -1)


def _mem_attn_kernel(x_ref, g_ref, wq_ref, qg_ref, mk_ref, mv_ref, wo_ref, o_ref):
    x = x_ref[...]
    q = _dot(_rms(x, g_ref[...]).astype(BF16), wq_ref[...])
    q = _headnorm(q, qg_ref[...], MEM_HEAD_DIM) * (MEM_HEAD_DIM ** -0.5)
    head = lambda ref: lambda h: ref[:, h * MEM_HEAD_DIM:(h + 1) * MEM_HEAD_DIM].astype(BF16)
    o = _mem_heads(q, head(mk_ref), head(mv_ref))
    o_ref[...] = x + _dot(o.astype(BF16), wo_ref[...])


def mem_attn_prompt(x, g, w_q, q_gain, mkv, layer, w_o, batch, tm):
    t = x.shape[0]
    per_b = t // batch // tm
    wspec = pl.BlockSpec((None, D_MODEL, D_MODEL), lambda b, i: (layer, 0, 0))
    return pl.pallas_call(
        _mem_attn_kernel,
        grid=(batch, per_b),
        in_specs=[pl.BlockSpec((tm, D_MODEL), lambda b, i: (b * per_b + i, 0)),
                  pl.BlockSpec((1, D_MODEL), lambda b, i: (0, 0)),
                  wspec,
                  pl.BlockSpec((1, D_MODEL), lambda b, i: (0, 0)),
                  pl.BlockSpec((None, N_MEM, D_MODEL), lambda b, i: (layer, b, 0)),
                  pl.BlockSpec((None, N_MEM, D_MODEL), lambda b, i: (layer, b, 1)),
                  wspec],
        out_specs=pl.BlockSpec((tm, D_MODEL), lambda b, i: (b * per_b + i, 0)),
        out_shape=jax.ShapeDtypeStruct((t, D_MODEL), F32),
        compiler_params=_cparams("parallel", "arbitrary"),
        name="mem_attn_prompt",
    )(x, g, w_q, q_gain, mkv, mkv, w_o)


def _mem_attn_sample_kernel(x_ref, g_ref, wq_ref, qg_ref, mk_ref, mv_ref, wo_ref, o_ref,
                            q_scr, o_scr):
    b = pl.program_id(0)

    @pl.when(b == 0)
    def _():
        q = _dot(_rms(x_ref[...], g_ref[...]).astype(BF16), wq_ref[...])
        q_scr[...] = _headnorm(q, qg_ref[...], MEM_HEAD_DIM) * (MEM_HEAD_DIM ** -0.5)

    rows = pl.ds(pl.multiple_of(b * SAMPLE_ROWS, SAMPLE_ROWS), SAMPLE_ROWS)
    hd, nr, nk = MEM_HEAD_DIM, MEM_HEADS * SAMPLE_ROWS, N_MEM * MEM_HEADS
    q = q_scr[rows, :]
    qx = jnp.concatenate([q[:, h * hd:(h + 1) * hd] for h in range(MEM_HEADS)], 0).astype(BF16)
    s = _dot_nt(qx, mk_ref[...].reshape(nk, hd).astype(BF16))
    own = (lax.broadcasted_iota(jnp.int32, (nr, nk), 1) % MEM_HEADS
           == lax.broadcasted_iota(jnp.int32, (nr, nk), 0) // SAMPLE_ROWS)
    s = jnp.where(own, s, NEG_BIG)
    p = jnp.exp(s - jnp.max(s, -1, keepdims=True))
    o = _dot(p.astype(BF16), mv_ref[...].reshape(nk, hd).astype(BF16)) / jnp.sum(p, -1, keepdims=True)
    o_scr[rows, :] = jnp.concatenate([o[h * SAMPLE_ROWS:(h + 1) * SAMPLE_ROWS] for h in range(MEM_HEADS)], -1)

    @pl.when(b == pl.num_programs(0) - 1)
    def _():
        o_ref[...] = x_ref[...] + _dot(o_scr[...].astype(BF16), wo_ref[...])


def mem_attn_sample(x, g, w_q, q_gain, cache_k, cache_v, layer, w_o, batch):
    t = x.shape[0]
    const = lambda b: (0, 0)
    wspec = pl.BlockSpec((None, D_MODEL, D_MODEL), lambda b: (layer, 0, 0))
    cspec = pl.BlockSpec((None, None, N_MEM, MEM_HEADS, MEM_HEAD_DIM), lambda b: (layer, b, 0, 0, 0))
    return pl.pallas_call(
        _mem_attn_sample_kernel,
        grid=(batch,),
        in_specs=[pl.BlockSpec((t, D_MODEL), const),
                  pl.BlockSpec((1, D_MODEL), const),
                  wspec,
                  pl.BlockSpec((1, D_MODEL), const),
                  cspec, cspec, wspec],
        out_specs=pl.BlockSpec((t, D_MODEL), const),
        out_shape=jax.ShapeDtypeStruct((t, D_MODEL), F32),
        scratch_shapes=[pltpu.VMEM((t, D_MODEL), F32), pltpu.VMEM((t, D_MODEL), F32)],
        compiler_params=_cparams("arbitrary"),
        name="mem_attn_sample",
    )(x, g, w_q, q_gain, cache_k, cache_v, w_o)


Q_SUB = 2


def _swa_prompt_kernel(q_ref, kp_ref, kc_ref, vp_ref, vc_ref, o_ref, l_ref):
    i = pl.program_id(2)
    q = (q_ref[...] * (SWA_HEAD_DIM ** -0.5)).astype(BF16)
    k = jnp.concatenate([kp_ref[...], kc_ref[...]], 0).astype(BF16)
    v = jnp.concatenate([vp_ref[...], vc_ref[...]], 0).astype(BF16)
    qq = lax.broadcasted_iota(jnp.int32, (Q_BLK, 2 * Q_BLK), 0)
    kk = lax.broadcasted_iota(jnp.int32, (Q_BLK, 2 * Q_BLK), 1)
    band = (kk >= qq) & (kk <= qq + SWA_KEYS_BACK)
    for sub in range(Q_SUB):
        rows = slice(sub * Q_BLK, (sub + 1) * Q_BLK)
        win = slice(sub * Q_BLK, (sub + 2) * Q_BLK)
        valid = band if sub else band & ((kk >= Q_BLK) | (i > 0))
        outs, lses = [], []
        for h in range(SWA_HEADS):
            sl = slice(h * SWA_HEAD_DIM, (h + 1) * SWA_HEAD_DIM)
            s = jnp.where(valid, _dot_nt(q[rows, sl], k[win, sl]), NEG_BIG)
            m = jnp.max(s, -1, keepdims=True)
            p = jnp.exp(s - m)
            l = jnp.sum(p, -1, keepdims=True)
            outs.append(_dot(p.astype(BF16), v[win, sl]) / l)
            lses.append(jnp.broadcast_to(m + jnp.log(l), (Q_BLK, SWA_HEAD_DIM)))
        o_ref[rows, :] = jnp.concatenate(outs, -1)
        l_ref[rows, :] = jnp.concatenate(lses, -1)


def swa_prompt(qkv, group):
    batch, dil, rows, _ = qkv.shape
    blk = (None, None, Q_SUB * Q_BLK, SWA_WIDTH)
    pblk = (None, None, Q_BLK, SWA_WIDTH)
    prev = lambda i: jnp.maximum(i * Q_SUB - 1, 0)
    q_spec = pl.BlockSpec(blk, lambda b, r, i: (b, r, i, 0))
    kp_spec = pl.BlockSpec(pblk, lambda b, r, i: (b, r, prev(i), 1))
    kc_spec = pl.BlockSpec(blk, lambda b, r, i: (b, r, i, 1))
    vp_spec = pl.BlockSpec(pblk, lambda b, r, i: (b, r, prev(i), 2))
    vc_spec = pl.BlockSpec(blk, lambda b, r, i: (b, r, i, 2))
    shape = jax.ShapeDtypeStruct((batch, dil, rows, SWA_WIDTH), F32)
    return pl.pallas_call(
        _swa_prompt_kernel,
        grid=(batch, dil, rows // (Q_SUB * Q_BLK)),
        in_specs=[q_spec, kp_spec, kc_spec, vp_spec, vc_spec],
        out_specs=[q_spec, q_spec],
        out_shape=[shape, shape],
        compiler_params=_cparams("parallel", "parallel", "arbitrary"),
        name="swa_prompt_g%d" % group,
    )(qkv, qkv, qkv, qkv, qkv)


ROWS_BLK = 64


def _swa_rows_kernel(*refs, dil, n_layers):
    srcs, (ko_ref, vo_ref) = refs[:2 * n_layers], refs[2 * n_layers:]
    layer = pl.program_id(0)
    for li in range(n_layers):
        @pl.when(layer == li)
        def _(li=li):
            for src, dst in ((srcs[2 * li], ko_ref), (srcs[2 * li + 1], vo_ref)):
                for r in range(dil):
                    x = src[r]
                    by_head = jnp.stack([x[:, h * SWA_HEAD_DIM:(h + 1) * SWA_HEAD_DIM]
                                         for h in range(SWA_HEADS)], 0)
                    dst[pl.ds(r, ROWS_BLK, stride=dil)] = jnp.swapaxes(by_head, 0, 1)


def swa_last_rows(qkvs):
    n_layers = len(qkvs)
    batch, dil, rows, _ = qkvs[0].shape
    keep = SWA_KEYS_BACK * dil
    nb = SWA_KEYS_BACK // ROWS_BLK
    first = (rows - SWA_KEYS_BACK) // ROWS_BLK
    in_specs, args = [], []
    for qkv in qkvs:
        for kind in (1, 2):
            in_specs.append(pl.BlockSpec((None, dil, ROWS_BLK, SWA_WIDTH),
                                         lambda l, b, a, kind=kind: (b, 0, first + a, kind)))
            args.append(qkv)
    o_spec = pl.BlockSpec((None, None, ROWS_BLK * dil, SWA_HEADS, SWA_HEAD_DIM), lambda l, b, a: (l, b, a, 0, 0))
    shape = jax.ShapeDtypeStruct((n_layers, batch, keep, SWA_HEADS, SWA_HEAD_DIM), F32)
    return pl.pallas_call(
        functools.partial(_swa_rows_kernel, dil=dil, n_layers=n_layers),
        grid=(n_layers, batch, nb),
        in_specs=in_specs,
        out_specs=[o_spec, o_spec],
        out_shape=[shape, shape],
        compiler_params=_cparams("parallel", "parallel", "parallel"),
        name="swa_last_rows_d%d" % dil,
    )(*args)


def _swa_sample_kernel(q_ref, kn_ref, vn_ref, kc_ref, vc_ref, o_ref, l_ref, bias_scr, *, dil, n_new, res_major):
    nq, nh, hd = SAMPLE_ROWS, SWA_HEADS, SWA_HEAD_DIM
    nc = nh * nq
    nrc = kc_ref.shape[0] * (kc_ref.shape[1] if res_major else 1)
    nr = nrc + nc

    @pl.when(pl.program_id(0) == 0)
    def _():
        row = lax.broadcasted_iota(jnp.int32, (nr, nc), 0)
        col = lax.broadcasted_iota(jnp.int32, (nr, nc), 1)
        hq, iq = col // nq, col % nq
        if res_major:
            ok_c = (row // nh) % n_new == iq
        else:
            ok_c = (row // nh >= iq) & ((row // nh - iq) % dil == 0)
        rn = row - nrc
        hn, jn = rn // nq, rn % nq
        ok_n = (jn < n_new) & (jn <= iq) & ((iq - jn) % dil == 0)
        valid = ((row < nrc) & (row % nh == hq) & ok_c) | ((row >= nrc) & (hn == hq) & ok_n)
        bias_scr[...] = jnp.where(valid, 0.0, NEG_BIG)

    def by_head(x):
        return jnp.concatenate([x[:, h * hd:(h + 1) * hd] for h in range(nh)], 0)

    flat = (lambda ref: ref[...].reshape(nrc, hd)) if res_major else (lambda ref: ref[...])
    qx = by_head(q_ref[...] * (hd ** -0.5)).astype(BF16)
    keys = jnp.concatenate([flat(kc_ref), by_head(kn_ref[...])], 0).astype(BF16)
    vals = jnp.concatenate([flat(vc_ref), by_head(vn_ref[...])], 0).astype(BF16)
    s = _dot_nt(keys, qx) + bias_scr[...]
    m = jnp.max(s, 0, keepdims=True)
    p = jnp.exp(s - m)
    l = jnp.sum(p, 0, keepdims=True)
    acc = lax.dot_general(p.astype(BF16), vals, (((0,), (0,)), ((), ())), preferred_element_type=F32)
    eye = lax.broadcasted_iota(jnp.int32, (nc, nc), 0) == lax.broadcasted_iota(jnp.int32, (nc, nc), 1)
    to_col = lambda v: jnp.sum(jnp.where(eye, jnp.broadcast_to(v, (nc, nc)), 0.0), -1, keepdims=True)
    o = acc / to_col(l)
    lse = jnp.broadcast_to(to_col(m + jnp.log(l)), (nc, hd))
    o_ref[...] = jnp.concatenate([o[h * nq:(h + 1) * nq] for h in range(nh)], -1)
    l_ref[...] = jnp.concatenate([lse[h * nq:(h + 1) * nq] for h in range(nh)], -1)


def swa_sample(qkv, cache_k, cache_v, layer, group, n_new):
    batch = qkv.shape[0]
    n_layers, _, rows, heads, hd = cache_k.shape
    dil = SWA_GROUPS[group][1]
    blk = (None, SAMPLE_ROWS, SWA_WIDTH)
    res_major = dil >= 2 * n_new
    if res_major:
        view = (n_layers, batch, rows // dil, dil * heads, hd)
        c_spec = pl.BlockSpec((None, None, rows // dil, n_new * heads, hd), lambda b: (layer, b, 0, 0, 0))
    else:
        view = (n_layers, batch, rows * heads, hd)
        c_spec = pl.BlockSpec((None, None, rows * heads, hd), lambda b: (layer, b, 0, 0))
    shape = jax.ShapeDtypeStruct((batch, SAMPLE_ROWS, SWA_WIDTH), F32)
    n_rows = (n_new * rows // dil if res_major else rows) * heads + SAMPLE_ROWS * heads
    return pl.pallas_call(
        functools.partial(_swa_sample_kernel, dil=dil, n_new=n_new, res_major=res_major),
        grid=(batch,),
        in_specs=[pl.BlockSpec(blk, lambda b: (b, 0, 0)),
                  pl.BlockSpec(blk, lambda b: (b, 0, 1)),
                  pl.BlockSpec(blk, lambda b: (b, 0, 2)),
                  c_spec, c_spec],
        out_specs=[pl.BlockSpec(blk, lambda b: (b, 0, 0))] * 2,
        out_shape=[shape, shape],
        scratch_shapes=[pltpu.VMEM((n_rows, SAMPLE_ROWS * heads), F32)],
        compiler_params=_cparams("arbitrary"),
        name="swa_sample_g%d" % group,
    )(qkv, qkv, qkv, cache_k.reshape(view), cache_v.reshape(view))


def _attn_out_kernel(x_ref, o0, o1, o2, l0, l1, l2, w_ref, out_ref, *scr, dils):
    tm = x_ref.shape[0]
    scr = list(scr)

    def rows(ref, dil):
        if dil == 1:
            return ref[0]
        buf = scr.pop()
        for c in range(buf.shape[0]):
            for r in range(dil):
                buf[c, pl.ds(r, tm // dil, stride=dil), :] = ref[r, :, c * LANES:(c + 1) * LANES]
        return jnp.concatenate([buf[c] for c in range(buf.shape[0])], -1)

    a, b, c = rows(l0, dils[0]), rows(l1, dils[1]), rows(l2, dils[2])
    m = jnp.maximum(jnp.maximum(a, b), c)
    ea, eb, ec = jnp.exp(a - m), jnp.exp(b - m), jnp.exp(c - m)
    o = (ea * rows(o0, dils[0]) + eb * rows(o1, dils[1]) + ec * rows(o2, dils[2])) / (ea + eb + ec)
    out_ref[...] = x_ref[...] + _dot(o.astype(BF16), w_ref[...])


def attn_out(x, outs, lses, w_o, layer, batch, tm):
    t = x.shape[0]
    per_b = t // batch // tm
    dils = tuple(o.shape[1] for o in outs)
    specs = [pl.BlockSpec((None, d, tm // d, SWA_WIDTH), lambda i: (i // per_b, 0, i % per_b, 0)) for d in dils]
    n_scr = 2 * sum(d > 1 for d in dils)
    return pl.pallas_call(
        functools.partial(_attn_out_kernel, dils=dils),
        grid=(t // tm,),
        in_specs=[pl.BlockSpec((tm, D_MODEL), lambda i: (i, 0))] + specs + specs
                 + [pl.BlockSpec((None, SWA_WIDTH, D_MODEL), lambda i: (layer, 0, 0))],
        out_specs=pl.BlockSpec((tm, D_MODEL), lambda i: (i, 0)),
        out_shape=jax.ShapeDtypeStruct((t, D_MODEL), F32),
        scratch_shapes=[pltpu.VMEM((SWA_WIDTH // LANES, tm, LANES), F32)] * n_scr,
        compiler_params=_cparams("parallel"),
        name="attn_out",
    )(x, *outs, *lses, w_o)


SCAN_PASSES = 8


def _ssm_param_kernel(lr_ref, li_ref, ls_ref, btr_ref, bti_ref, cr_ref, ci_ref,
                      win_ref, wout_ref, tt_ref, aqr_ref, aqi_ref, e_scr, *, q):
    g = SSM_GROUP
    lr, li = lr_ref[...], li_ref[...]
    step = jnp.exp(ls_ref[...])
    lo = lax.broadcasted_iota(jnp.int32, (1, LANES), 1) < SSM_STATE

    def powers(tau):
        mag = jnp.exp(tau * (lr * step))
        ang = tau * (li * step)
        return mag * jnp.cos(ang), mag * jnp.sin(ang)

    tau = lax.broadcasted_iota(jnp.int32, (q + 1, LANES), 0).astype(F32)
    c2, s2 = powers(tau)
    ar, ai = c2[1:2], s2[1:2]
    den = lr * lr + li * li
    fr = ((ar - 1.0) * lr + ai * li) / den
    fi = (ai * lr - (ar - 1.0) * li) / den
    btr, bti = btr_ref[...], bti_ref[...]
    bbr = fr * btr - fi * bti
    bbi = fr * bti + fi * btr
    bba = jnp.where(lo, bbr, bbi)
    bbb = jnp.where(lo, -bbi, bbr)
    pa = jnp.where(lo, c2, -s2)
    pb = jnp.where(lo, s2, c2)
    cr, ci = cr_ref[...], ci_ref[...]
    for t in range(q + 1):
        e_scr[t * g:(t + 1) * g, :] = cr * pa[t:t + 1] - ci * pb[t:t + 1]
    wout_ref[...] = e_scr[g:(q + 1) * g, :].astype(BF16)
    kt = _dot_nt(bba, e_scr[0:q * g, :], precision=lax.Precision.HIGHEST)
    for s in range(q):
        blk = kt if s == 0 else jnp.concatenate([jnp.zeros((g, g * s), F32), kt[:, :(q - s) * g]], 1)
        tt_ref[s * g:(s + 1) * g, :] = blk.astype(BF16)
    c2r, s2r = powers((q - 1.0) - tau[0:q])
    for s in range(q):
        win_ref[s * g:(s + 1) * g, :] = (c2r[s:s + 1] * bba + s2r[s:s + 1] * bbb).astype(BF16)
    pw = lax.broadcasted_iota(jnp.int32, (SCAN_PASSES, LANES), 0)
    cq, sq = powers((q * jnp.left_shift(1, pw)).astype(F32))
    aqr_ref[...] = cq
    aqi_ref[...] = jnp.where(lo, -sq, sq)


def ssm_params(lam_re, lam_im, log_step, b_re, b_im, c_re, c_im, q):
    gr, st, g = SSM_GROUPS, SSM_STATE, SSM_GROUP
    dup = lambda a: jnp.concatenate([a, a], -1)
    lr = dup(lam_re).reshape(gr, 1, LANES)
    li = dup(lam_im).reshape(gr, 1, LANES)
    ls = jnp.broadcast_to(log_step[:, None, None], (gr, 1, LANES))
    btr = dup(jnp.swapaxes(b_re, 1, 2))
    bti = dup(jnp.swapaxes(b_im, 1, 2))
    cr, ci = dup(c_re), dup(c_im)
    vec = pl.BlockSpec((None, 1, LANES), lambda i: (i, 0, 0))
    mat = pl.BlockSpec((None, g, LANES), lambda i: (i, 0, 0))
    big = pl.BlockSpec((None, q * g, LANES), lambda i: (i, 0, 0))
    pws = pl.BlockSpec((None, SCAN_PASSES, LANES), lambda i: (i, 0, 0))
    return pl.pallas_call(
        functools.partial(_ssm_param_kernel, q=q),
        grid=(gr,),
        in_specs=[vec, vec, vec, mat, mat, mat, mat],
        out_specs=[big, big, pl.BlockSpec((None, q * g, q * g), lambda i: (i, 0, 0)), pws, pws],
        out_shape=[jax.ShapeDtypeStruct((gr, q * g, LANES), BF16),
                   jax.ShapeDtypeStruct((gr, q * g, LANES), BF16),
                   jax.ShapeDtypeStruct((gr, q * g, q * g), BF16),
                   jax.ShapeDtypeStruct((gr, SCAN_PASSES, LANES), F32),
                   jax.ShapeDtypeStruct((gr, SCAN_PASSES, LANES), F32)],
        scratch_shapes=[pltpu.VMEM(((q + 1) * g, LANES), F32)],
        compiler_params=_cparams("parallel"),
        name="ssm_params_q%d" % q,
    )(lr, li, ls, btr, bti, cr, ci)


def _ssm_in_kernel(u_ref, win_ref, s_ref):
    s_ref[...] = _dot(u_ref[...], win_ref[...])


def _ssm_scan_kernel(s_ref, h0_ref, ar_ref, ai_ref, hp_ref, hf_ref, h_scr):
    @pl.when(pl.program_id(0) == 0)
    def _():
        h_scr[...] = h0_ref[...]

    def body(k, h):
        hp_ref[k] = h.astype(BF16)
        return ar_ref[...] * h + ai_ref[...] * pltpu.roll(h, SSM_STATE, 1) + s_ref[k]

    h = lax.fori_loop(0, s_ref.shape[0], body, h_scr[...])
    h_scr[...] = h
    hf_ref[...] = h


def _ssm_out_kernel(u_ref, tt_ref, hp_ref, wout_ref, y_ref):
    y_ref[...] = _dot(u_ref[...], tt_ref[...]) + _dot_nt(hp_ref[...], wout_ref[...])


def ssm_mix(h_bf16, h0_re, h0_im, params, batch, seq, q):
    win, wout, tt, aqr, aqi = params
    gr, st, g = SSM_GROUPS, SSM_STATE, SSM_GROUP
    nk = seq // q
    n = nk * batch
    qg = q * g
    u = h_bf16.reshape(batch, nk, q, gr, g).transpose(3, 1, 0, 2, 4).reshape(gr, n, qg)
    s = pl.pallas_call(
        _ssm_in_kernel,
        grid=(gr,),
        in_specs=[pl.BlockSpec((None, n, qg), lambda i: (i, 0, 0)),
                  pl.BlockSpec((None, qg, LANES), lambda i: (i, 0, 0))],
        out_specs=pl.BlockSpec((n, LANES), lambda i: (0, i)),
        out_shape=jax.ShapeDtypeStruct((n, gr * LANES), F32),
        compiler_params=_cparams("parallel"),
        name="ssm_in_q%d" % q,
    )(u, win)
    rows = batch * gr
    kc = min(nk, 32)
    tile = lambda a: jnp.broadcast_to(a[:, 0].reshape(1, gr, LANES), (batch, gr, LANES)).reshape(rows, LANES)
    h0 = jnp.concatenate([h0_re, h0_im], -1).reshape(rows, LANES)
    full = pl.BlockSpec((rows, LANES), lambda i: (0, 0))
    hp, hf = pl.pallas_call(
        _ssm_scan_kernel,
        grid=(nk // kc,),
        in_specs=[pl.BlockSpec((kc, rows, LANES), lambda i: (i, 0, 0)), full, full, full],
        out_specs=[pl.BlockSpec((kc, rows, LANES), lambda i: (i, 0, 0)), full],
        out_shape=[jax.ShapeDtypeStruct((nk, rows, LANES), BF16),
                   jax.ShapeDtypeStruct((rows, LANES), F32)],
        scratch_shapes=[pltpu.VMEM((rows, LANES), F32)],
        compiler_params=_cparams("arbitrary"),
        name="ssm_scan_q%d" % q,
    )(s.reshape(nk, rows, LANES), h0, tile(aqr), tile(aqi))
    y = pl.pallas_call(
        _ssm_out_kernel,
        grid=(gr,),
        in_specs=[pl.BlockSpec((None, n, qg), lambda i: (i, 0, 0)),
                  pl.BlockSpec((None, qg, qg), lambda i: (i, 0, 0)),
                  pl.BlockSpec((n, LANES), lambda i: (0, i)),
                  pl.BlockSpec((None, qg, LANES), lambda i: (i, 0, 0))],
        out_specs=pl.BlockSpec((None, n, qg), lambda i: (i, 0, 0)),
        out_shape=jax.ShapeDtypeStruct((gr, n, qg), F32),
        compiler_params=_cparams("parallel"),
        name="ssm_out_q%d" % q,
    )(u, tt, hp.reshape(n, gr * LANES), wout)
    y = y.reshape(gr, nk, batch, q, g).transpose(2, 1, 3, 0, 4).reshape(batch * seq, D_MODEL)
    hf = hf.reshape(batch, gr, LANES)
    return y, hf[..., :st], hf[..., st:]


SSM_TOKENS_BLK = 2048


GROUPS_PER_TILE = LANES // SSM_GROUP
N_LANE_TILES = D_MODEL // LANES


def _to_chunks_kernel(*refs, q):
    x_refs, (g_ref, u_ref, r_scr, ut_scr) = refs[:N_LANE_TILES], refs[N_LANE_TILES:]
    tb = r_scr.shape[0]
    nkb = tb // q
    ss = jnp.zeros((tb, 1), F32)
    for x_ref in x_refs:
        x = x_ref[...]
        ss = ss + jnp.sum(x * x, -1, keepdims=True)
    r_scr[...] = jnp.broadcast_to(lax.rsqrt(ss * (1.0 / D_MODEL) + RMS_EPS), (tb, LANES))
    for c, x_ref in enumerate(x_refs):
        gain = g_ref[:, c * LANES:(c + 1) * LANES]
        for s in range(q):
            rows = pl.ds(s, nkb, stride=q)
            m = x_ref[rows, :] * r_scr[rows, :] * gain
            ut_scr[:, s * SSM_GROUP:(s + 1) * SSM_GROUP, :] = m.T.reshape(GROUPS_PER_TILE, SSM_GROUP, nkb)
        for grp in range(GROUPS_PER_TILE):
            u_ref[c * GROUPS_PER_TILE + grp] = ut_scr[grp].T.astype(BF16)


def _from_chunks_kernel(y_ref, o_ref, yt_scr, o_scr, *, q):
    nkb = o_ref.shape[0] // q
    for c in range(N_LANE_TILES):
        for grp in range(GROUPS_PER_TILE):
            yt_scr[grp] = y_ref[c * GROUPS_PER_TILE + grp].T
        for t in range(q):
            o_scr[pl.ds(t, nkb, stride=q), :] = yt_scr[:, t * SSM_GROUP:(t + 1) * SSM_GROUP, :].reshape(LANES, nkb).T
        o_ref[:, c * LANES:(c + 1) * LANES] = o_scr[...]


def _ssm_group_kernel(u_ref, win_ref, wout_ref, tt_ref, ar_ref, ai_ref, y_ref, hf_ref, h_scr, *, nk):
    u = u_ref[...]
    h = _dot(u, win_ref[...])
    n = h.shape[0]
    k_idx = lax.broadcasted_iota(jnp.int32, (n, LANES), 0) % nk
    d = 1
    for j in range(SCAN_PASSES):
        if d >= nk:
            break
        hs = jnp.where(k_idx >= d, pltpu.roll(h, d, 0), 0.0)
        h = h + ar_ref[j:j + 1, :] * hs + ai_ref[j:j + 1, :] * pltpu.roll(hs, SSM_STATE, 1)
        d *= 2
    hp = jnp.where(k_idx >= 1, pltpu.roll(h, 1, 0), 0.0)
    y_ref[...] = _dot(u, tt_ref[...]) + _dot_nt(hp.astype(BF16), wout_ref[...])
    h_scr[...] = h
    hf_ref[...] = h_scr[pl.ds(nk - 1, n // nk, stride=nk), :]


def ssm_mix_prompt(x, g_mix, params, batch, seq, q):
    win, wout, tt, aqr, aqi = params
    gr, st = SSM_GROUPS, SSM_STATE
    nk = seq // q
    assert nk <= 2 ** SCAN_PASSES
    n = nk * batch
    qg = q * SSM_GROUP
    tb = SSM_TOKENS_BLK
    nkb = tb // q
    per_b = seq // tb
    tok = pl.BlockSpec((tb, D_MODEL), lambda b, i: (b * per_b + i, 0))
    chk = pl.BlockSpec((gr, nkb, qg), lambda b, i: (0, b * per_b + i, 0))
    lane_tiles = [pl.BlockSpec((tb, LANES), lambda b, i, c=c: (b * per_b + i, c)) for c in range(N_LANE_TILES)]
    u = pl.pallas_call(
        functools.partial(_to_chunks_kernel, q=q),
        grid=(batch, per_b),
        in_specs=lane_tiles + [pl.BlockSpec((1, D_MODEL), lambda b, i: (0, 0))],
        out_specs=chk,
        out_shape=jax.ShapeDtypeStruct((gr, n, qg), BF16),
        scratch_shapes=[pltpu.VMEM((tb, LANES), F32), pltpu.VMEM((GROUPS_PER_TILE, qg, nkb), F32)],
        compiler_params=_cparams("parallel", "parallel"),
        name="ssm_to_chunks",
    )(*([x] * N_LANE_TILES), g_mix)
    per_g = lambda r, c: pl.BlockSpec((None, r, c), lambda i: (i, 0, 0))
    y, hf = pl.pallas_call(
        functools.partial(_ssm_group_kernel, nk=nk),
        grid=(gr,),
        in_specs=[per_g(n, qg), per_g(qg, LANES), per_g(qg, LANES), per_g(qg, qg),
                  per_g(SCAN_PASSES, LANES), per_g(SCAN_PASSES, LANES)],
        out_specs=[per_g(n, qg), per_g(batch, LANES)],
        out_shape=[jax.ShapeDtypeStruct((gr, n, qg), F32), jax.ShapeDtypeStruct((gr, batch, LANES), F32)],
        scratch_shapes=[pltpu.VMEM((n, LANES), F32)],
        compiler_params=_cparams("parallel"),
        name="ssm_group",
    )(u, win, wout, tt, aqr, aqi)
    y = pl.pallas_call(
        functools.partial(_from_chunks_kernel, q=q),
        grid=(batch, per_b),
        in_specs=[chk],
        out_specs=tok,
        out_shape=jax.ShapeDtypeStruct((batch * seq, D_MODEL), F32),
        scratch_shapes=[pltpu.VMEM((GROUPS_PER_TILE, qg, nkb), F32), pltpu.VMEM((tb, LANES), F32)],
        compiler_params=_cparams("parallel", "parallel"),
        name="ssm_from_chunks",
    )(y)
    hf = jnp.swapaxes(hf, 0, 1)
    return y, hf[..., :st], hf[..., st:]


def _ssm_glu_kernel(x_ref, gm_ref, y_ref, d_ref, w_ref, b_ref, o_ref):
    x = x_ref[...]
    y = y_ref[...] + d_ref[...] * _rms(x, gm_ref[...])
    gl = _gelu_tanh(y)
    z = _dot(gl.astype(BF16), w_ref[...]) + b_ref[...]
    o_ref[...] = x + gl * _sigmoid(z)


def ssm_glu(x, g_mix, y, d_skip, w_glu, layer, b_glu, tm):
    t = x.shape[0]
    row = pl.BlockSpec((tm, D_MODEL), lambda i: (i, 0))
    vec = pl.BlockSpec((1, D_MODEL), lambda i: (0, 0))
    return pl.pallas_call(
        _ssm_glu_kernel,
        grid=(t // tm,),
        in_specs=[row, vec, row, vec, pl.BlockSpec((None, D_MODEL, D_MODEL), lambda i: (layer, 0, 0)), vec],
        out_specs=row,
        out_shape=jax.ShapeDtypeStruct((t, D_MODEL), F32),
        compiler_params=_cparams("parallel"),
        name="ssm_glu",
    )(x, g_mix, y, d_skip, w_glu, b_glu)


def _row(v):
    return v.reshape(1, -1).astype(F32)


def kernel(x_prompt, x_sample, mem_prompt, state_ssm_re, state_ssm_im, cache_swa0_k, cache_swa0_v, cache_swa1_k, cache_swa1_v, cache_swa2_k, cache_swa2_v, cache_mem_k, cache_mem_v, norm_mix_g, norm_mem_g, norm_memin_g, norm_ffn_g, ssm_lambda_re, ssm_lambda_im, ssm_b_re, ssm_b_im, ssm_c_re, ssm_c_im, ssm_d, ssm_log_step, ssm_w_glu, ssm_b_glu, attn_w_qkv, attn_q_norm_g, attn_k_norm_g, attn_w_o, mem_w_q, mem_w_kv, mem_q_norm_g, mem_k_norm_g, mem_w_o, ffn_w_in, ffn_w_out):
    pb, seq, _ = x_prompt.shape
    sb, dec, _ = x_sample.shape
    n_ssm = state_ssm_re.shape[0]
    n_attn = cache_swa0_k.shape[0]
    tm_p = 512
    tm_s = sb * SAMPLE_ROWS

    w_glu = ssm_w_glu.astype(BF16)
    w_qkv = attn_w_qkv.astype(BF16)
    w_ao = attn_w_o.astype(BF16)
    w_mq = mem_w_q.astype(BF16)
    w_mkv = mem_w_kv.astype(BF16)
    w_mo = mem_w_o.astype(BF16)
    w_fi = ffn_w_in.astype(BF16)
    w_fo = ffn_w_out.astype(BF16)

    head_gain = [jnp.concatenate([jnp.tile(attn_q_norm_g[j], (1, SWA_HEADS)),
                                  jnp.tile(attn_k_norm_g[j], (1, SWA_HEADS))], 1).reshape(N_SWA, 1, 2 * SWA_WIDTH)
                 for j in range(n_attn)]
    mem_q_gain = [_row(jnp.tile(mem_q_norm_g[i], MEM_HEADS)) for i in range(DEPTH)]
    mem_k_gain = jnp.tile(mem_k_norm_g, (1, MEM_HEADS)).reshape(DEPTH, 1, D_MODEL)

    mkv = mem_kv(mem_prompt.reshape(pb * N_MEM, D_MODEL), norm_memin_g.reshape(DEPTH, 1, D_MODEL),
                 w_mkv, mem_k_gain)
    mkv5 = mkv.reshape(DEPTH, pb, N_MEM, 2, MEM_HEADS, MEM_HEAD_DIM)
    p_mem_k, p_mem_v = mkv5[:, :, :, 0], mkv5[:, :, :, 1]
    mkv_b = mkv.reshape(DEPTH, pb * N_MEM, 2 * D_MODEL)

    ssm_p = [[ssm_params(ssm_lambda_re[j], ssm_lambda_im[j], ssm_log_step[j], ssm_b_re[j], ssm_b_im[j],
                         ssm_c_re[j], ssm_c_im[j], q) for q in (SSM_Q_PROMPT, dec)] for j in range(n_ssm)]

    caches_k = (cache_swa0_k, cache_swa1_k, cache_swa2_k)
    caches_v = (cache_swa0_v, cache_swa1_v, cache_swa2_v)
    dils = tuple(d for _, d in SWA_GROUPS)
    assert all(min(w, seq) == SWA_KEYS_BACK * d for w, d in SWA_GROUPS)

    xp = x_prompt.reshape(pb * seq, D_MODEL)
    xs = jnp.pad(x_sample, ((0, 0), (0, SAMPLE_ROWS - dec), (0, 0))).reshape(tm_s, D_MODEL)

    p_ssm_re, p_ssm_im, s_ssm_re, s_ssm_im = [], [], [], []
    p_qkv = [[] for _ in SWA_GROUPS]
    s_swa_k = [[] for _ in SWA_GROUPS]
    s_swa_v = [[] for _ in SWA_GROUPS]

    for i in range(DEPTH):
        j = i // 2
        g_mix = _row(norm_mix_g[i])
        if i % 2 == 0:
            d_skip, b_glu = _row(ssm_d[j]), _row(ssm_b_glu[j])
            y, fr, fi = ssm_mix_prompt(xp, g_mix, ssm_p[j][0], pb, seq, SSM_Q_PROMPT)
            p_ssm_re.append(fr)
            p_ssm_im.append(fi)
            xp = ssm_glu(xp, g_mix, y, d_skip, w_glu, j, b_glu, tm_p)
            hb = norm_cast(xs, g_mix, tm_s).reshape(sb, SAMPLE_ROWS, D_MODEL)[:, :dec].reshape(sb * dec, D_MODEL)
            y, fr, fi = ssm_mix(hb, state_ssm_re[j], state_ssm_im[j], ssm_p[j][1], sb, dec, dec)
            s_ssm_re.append(fr)
            s_ssm_im.append(fi)
            y = jnp.pad(y.reshape(sb, dec, D_MODEL), ((0, 0), (0, SAMPLE_ROWS - dec), (0, 0))).reshape(tm_s, D_MODEL)
            xs = ssm_glu(xs, g_mix, y, d_skip, w_glu, j, b_glu, tm_s)
        else:
            qkvs = qkv_proj(xp, g_mix, w_qkv, j, head_gain[j], pb, dils, tm_p)
            outs, lses = [], []
            for g in range(N_SWA):
                p_qkv[g].append(qkvs[g])
                o, lse = swa_prompt(qkvs[g], g)
                outs.append(o)
                lses.append(lse)
            xp = attn_out(xp, outs, lses, w_ao, j, pb, tm_p)
            qkvs = qkv_proj(xs, g_mix, w_qkv, j, head_gain[j], 1, (1,) * N_SWA, tm_s)
            outs, lses = [], []
            for g in range(N_SWA):
                qkv3 = qkvs[g].reshape(sb, SAMPLE_ROWS, 3 * SWA_WIDTH)
                qkv5 = qkv3.reshape(sb, SAMPLE_ROWS, 3, SWA_HEADS, SWA_HEAD_DIM)
                s_swa_k[g].append(qkv5[:, :dec, 1])
                s_swa_v[g].append(qkv5[:, :dec, 2])
                o, lse = swa_sample(qkv3, caches_k[g], caches_v[g], j, g, dec)
                outs.append(o.reshape(1, 1, tm_s, SWA_WIDTH))
                lses.append(lse.reshape(1, 1, tm_s, SWA_WIDTH))
            xs = attn_out(xs, outs, lses, w_ao, j, 1, tm_s)

        g_mem = _row(norm_mem_g[i])
        xp = mem_attn_prompt(xp, g_mem, w_mq, mem_q_gain[i], mkv_b, i, w_mo, pb, tm_p)
        xs = mem_attn_sample(xs, g_mem, w_mq, mem_q_gain[i], cache_mem_k, cache_mem_v, i, w_mo, sb)
        g_ffn = _row(norm_ffn_g[i])
        xp = swiglu_block(xp, g_ffn, w_fi, w_fo, i, 1024)
        xs = swiglu_block(xs, g_ffn, w_fi, w_fo, i, tm_s)

    p_swa = [swa_last_rows(p_qkv[g]) for g in range(N_SWA)]
    y_prompt = xp.reshape(pb, seq, D_MODEL)
    y_sample = xs.reshape(sb, SAMPLE_ROWS, D_MODEL)[:, :dec]
    st = lambda a: jnp.stack(a, 0)
    return (y_prompt, y_sample,
            st(p_ssm_re), st(p_ssm_im),
            p_swa[0][0], p_swa[0][1], p_swa[1][0], p_swa[1][1], p_swa[2][0], p_swa[2][1],
            p_mem_k, p_mem_v,
            st(s_ssm_re), st(s_ssm_im),
            st(s_swa_k[0]), st(s_swa_v[0]), st(s_swa_k[1]), st(s_swa_v[1]), st(s_swa_k[2]), st(s_swa_v[2]))
```

```python
import functools
import math

import jax
import jax.numpy as jnp
from jax import lax
from jax.experimental import pallas as pl
from jax.experimental.pallas import tpu as pltpu

F32 = jnp.float32
BF16 = jnp.bfloat16

D_MODEL = 1024
DEPTH = 4
SSM_GROUP = 16
SSM_GROUPS = D_MODEL // SSM_GROUP
SSM_STATE = 64
SWA_GROUPS = ((128, 1), (512, 4), (2048, 16))
N_SWA = len(SWA_GROUPS)
SWA_HEADS = 8
SWA_HEAD_DIM = 64
SWA_WIDTH = SWA_HEADS * SWA_HEAD_DIM
QKV_WIDTH = 3 * N_SWA * SWA_WIDTH
SWA_KEYS_BACK = 128
N_MEM = 256
MEM_HEADS = 4
MEM_HEAD_DIM = D_MODEL // MEM_HEADS
D_FF = -(-8 * D_MODEL // (3 * 256)) * 256
RMS_EPS = 1e-6
NEG_BIG = -1e30

LANES = 128
SUBLANES = 8
VMEM_LIMIT = 48 * 1024 * 1024
Q_BLK = 128
FF_BLK = 256
SAMPLE_ROWS = SUBLANES
SSM_Q_PROMPT = 16


def _cparams(*sem):
    return pltpu.CompilerParams(dimension_semantics=sem, vmem_limit_bytes=VMEM_LIMIT)


def _dot(a, b):
    return jnp.dot(a, b, preferred_element_type=F32)


def _dot_nt(a, b, precision=None):
    return lax.dot_general(a, b, (((1,), (1,)), ((), ())), preferred_element_type=F32,
                           precision=precision)


def _rms(x, g):
    return x * lax.rsqrt(jnp.mean(x * x, -1, keepdims=True) + RMS_EPS) * g


def _headnorm(y, g_row, hd):
    tm, n = y.shape
    outs = []
    if hd % LANES == 0:
        for c in range(n // hd):
            yc = y[:, c * hd:(c + 1) * hd]
            ms = jnp.mean(yc * yc, -1, keepdims=True)
            outs.append(yc * lax.rsqrt(ms + RMS_EPS))
    else:
        assert 2 * hd == LANES
        lo = lax.broadcasted_iota(jnp.int32, (tm, LANES), 1) < hd
        for c in range(n // LANES):
            yc = y[:, c * LANES:(c + 1) * LANES]
            sq = yc * yc
            s_lo = jnp.sum(jnp.where(lo, sq, 0.0), -1, keepdims=True)
            s_hi = jnp.sum(jnp.where(lo, 0.0, sq), -1, keepdims=True)
            ms = jnp.where(lo, s_lo, s_hi) * (1.0 / hd)
            outs.append(yc * lax.rsqrt(ms + RMS_EPS))
    return jnp.concatenate(outs, -1) * g_row


def _sigmoid(x):
    return 1.0 / (1.0 + jnp.exp(-x))


def _gelu_tanh(x):
    return 0.5 * x * (1.0 + jnp.tanh(math.sqrt(2.0 / math.pi) * (x + 0.044715 * (x * x * x))))


def _norm_cast_kernel(x_ref, g_ref, o_ref):
    o_ref[...] = _rms(x_ref[...], g_ref[...]).astype(BF16)


def norm_cast(x, g, tm):
    t = x.shape[0]
    return pl.pallas_call(
        _norm_cast_kernel,
        grid=(t // tm,),
        in_specs=[pl.BlockSpec((tm, D_MODEL), lambda i: (i, 0)),
                  pl.BlockSpec((1, D_MODEL), lambda i: (0, 0))],
        out_specs=pl.BlockSpec((tm, D_MODEL), lambda i: (i, 0)),
        out_shape=jax.ShapeDtypeStruct((t, D_MODEL), BF16),
        compiler_params=_cparams("parallel"),
        name="norm_cast",
    )(x, g)


def _qkv_kernel(x_ref, g_ref, wq_ref, wk_ref, wv_ref, hg_ref, o0_ref, o1_ref, o2_ref, xn_ref, y_ref, *, dils):
    j = pl.program_id(1)

    @pl.when(j == 0)
    def _():
        xn_ref[...] = _rms(x_ref[...], g_ref[...]).astype(BF16)

    tm = x_ref.shape[0]
    w = SWA_WIDTH
    for grp, (o_ref, dil) in enumerate(zip((o0_ref, o1_ref, o2_ref), dils)):
        @pl.when(j == grp)
        def _(o_ref=o_ref, dil=dil):
            xn = xn_ref[...]
            parts = (_headnorm(_dot(xn, wq_ref[...]), hg_ref[:, :w], SWA_HEAD_DIM),
                     _headnorm(_dot(xn, wk_ref[...]), hg_ref[:, w:], SWA_HEAD_DIM),
                     _dot(xn, wv_ref[...]))
            if dil == 1:
                for kind, y in enumerate(parts):
                    o_ref[0, :, kind * w:(kind + 1) * w] = y
            else:
                per = w // LANES
                for c in range(y_ref.shape[0]):
                    y_ref[c] = parts[c // per][:, (c % per) * LANES:(c % per + 1) * LANES]
                    for r in range(dil):
                        o_ref[r, :, c * LANES:(c + 1) * LANES] = y_ref[c, pl.ds(r, tm // dil, stride=dil), :]


def qkv_proj(x, g, w, layer, head_gain, batch, dils, tm):
    t = x.shape[0]
    seq = t // batch
    per_b = seq // tm
    gw = 3 * SWA_WIDTH
    wspec = lambda kind: pl.BlockSpec((None, D_MODEL, SWA_WIDTH), lambda i, j: (layer, 0, kind * N_SWA + j))
    return pl.pallas_call(
        functools.partial(_qkv_kernel, dils=dils),
        grid=(t // tm, N_SWA),
        in_specs=[pl.BlockSpec((tm, D_MODEL), lambda i, j: (i, 0)),
                  pl.BlockSpec((1, D_MODEL), lambda i, j: (0, 0)),
                  wspec(0), wspec(1), wspec(2),
                  pl.BlockSpec((None, 1, 2 * SWA_WIDTH), lambda i, j: (j, 0, 0))],
        out_specs=[pl.BlockSpec((None, d, tm // d, gw), lambda i, j: (i // per_b, 0, i % per_b, 0))
                   for d in dils],
        out_shape=[jax.ShapeDtypeStruct((batch, d, seq // d, gw), F32) for d in dils],
        scratch_shapes=[pltpu.VMEM((tm, D_MODEL), BF16), pltpu.VMEM((gw // LANES, tm, LANES), F32)],
        compiler_params=_cparams("parallel", "arbitrary"),
        name="qkv_proj",
    )(x, g, w, w, w, head_gain)


def _mem_kv_kernel(x_ref, g_ref, w_ref, kg_ref, o_ref):
    j = pl.program_id(1)
    y = _dot(_rms(x_ref[...], g_ref[...]).astype(BF16), w_ref[...])

    @pl.when(j == 0)
    def _():
        o_ref[...] = _headnorm(y, kg_ref[...], MEM_HEAD_DIM)

    @pl.when(j == 1)
    def _():
        o_ref[...] = y


def mem_kv(mem, g_in, w_kv, k_gain):
    t = mem.shape[0]
    return pl.pallas_call(
        _mem_kv_kernel,
        grid=(DEPTH, 2),
        in_specs=[pl.BlockSpec((t, D_MODEL), lambda l, j: (0, 0)),
                  pl.BlockSpec((None, 1, D_MODEL), lambda l, j: (l, 0, 0)),
                  pl.BlockSpec((None, D_MODEL, D_MODEL), lambda l, j: (l, 0, j)),
                  pl.BlockSpec((None, 1, D_MODEL), lambda l, j: (l, 0, 0))],
        out_specs=pl.BlockSpec((None, t, D_MODEL), lambda l, j: (l, 0, j)),
        out_shape=jax.ShapeDtypeStruct((DEPTH, t, 2 * D_MODEL), F32),
        compiler_params=_cparams("parallel", "arbitrary"),
        name="mem_kv",
    )(mem, g_in, w_kv, k_gain)


def _swiglu_kernel(x_ref, g_ref, wg_ref, wu_ref, wo_ref, o_ref, xn_ref, acc_ref):
    c = pl.program_id(1)

    @pl.when(c == 0)
    def _():
        xn_ref[...] = _rms(x_ref[...], g_ref[...]).astype(BF16)
        acc_ref[...] = jnp.zeros_like(acc_ref)

    xn = xn_ref[...]
    gate = _dot(xn, wg_ref[...])
    up = _dot(xn, wu_ref[...])
    act = gate * _sigmoid(gate) * up
    acc_ref[...] += _dot(act.astype(BF16), wo_ref[...])

    @pl.when(c == pl.num_programs(1) - 1)
    def _():
        o_ref[...] = x_ref[...] + acc_ref[...]


def swiglu_block(x, g, w_in, w_out, layer, tm):
    t = x.shape[0]
    nc = D_FF // FF_BLK
    return pl.pallas_call(
        _swiglu_kernel,
        grid=(t // tm, nc),
        in_specs=[pl.BlockSpec((tm, D_MODEL), lambda i, c: (i, 0)),
                  pl.BlockSpec((1, D_MODEL), lambda i, c: (0, 0)),
                  pl.BlockSpec((None, D_MODEL, FF_BLK), lambda i, c: (layer, 0, c)),
                  pl.BlockSpec((None, D_MODEL, FF_BLK), lambda i, c: (layer, 0, nc + c)),
                  pl.BlockSpec((None, FF_BLK, D_MODEL), lambda i, c: (layer, c, 0))],
        out_specs=pl.BlockSpec((tm, D_MODEL), lambda i, c: (i, 0)),
        out_shape=jax.ShapeDtypeStruct((t, D_MODEL), F32),
        scratch_shapes=[pltpu.VMEM((tm, D_MODEL), BF16), pltpu.VMEM((tm, D_MODEL), F32)],
        compiler_params=_cparams("parallel", "arbitrary"),
        name="swiglu",
    )(x, g, w_in, w_in, w_out)


def _mem_heads(q, key_head, value_head):
    outs = []
    for h in range(MEM_HEADS):
        sl = slice(h * MEM_HEAD_DIM, (h + 1) * MEM_HEAD_DIM)
        s = _dot_nt(q[:, sl].astype(BF16), key_head(h))
        m = jnp.max(s, -1, keepdims=True)
        p = jnp.exp(s - m)
        l = jnp.sum(p, -1, keepdims=True)
        outs.append(_dot(p.astype(BF16), value_head(h)) / l)
    return jnp.concatenate(outs, -1)


def _mem_attn_kernel(x_ref, g_ref, wq_ref, qg_ref, mk_ref, mv_ref, wo_ref, o_ref):
    x = x_ref[...]
    q = _dot(_rms(x, g_ref[...]).astype(BF16), wq_ref[...])
    q = _headnorm(q, qg_ref[...], MEM_HEAD_DIM) * (MEM_HEAD_DIM ** -0.5)
    head = lambda ref: lambda h: ref[:, h * MEM_HEAD_DIM:(h + 1) * MEM_HEAD_DIM].astype(BF16)
    o = _mem_heads(q, head(mk_ref), head(mv_ref))
    o_ref[...] = x + _dot(o.astype(BF16), wo_ref[...])


def mem_attn_prompt(x, g, w_q, q_gain, mkv, layer, w_o, batch, tm):
    t = x.shape[0]
    per_b = t // batch // tm
    wspec = pl.BlockSpec((None, D_MODEL, D_MODEL), lambda b, i: (layer, 0, 0))
    return pl.pallas_call(
        _mem_attn_kernel,
        grid=(batch, per_b),
        in_specs=[pl.BlockSpec((tm, D_MODEL), lambda b, i: (b * per_b + i, 0)),
                  pl.BlockSpec((1, D_MODEL), lambda b, i: (0, 0)),
                  wspec,
                  pl.BlockSpec((1, D_MODEL), lambda b, i: (0, 0)),
                  pl.BlockSpec((None, N_MEM, D_MODEL), lambda b, i: (layer, b, 0)),
                  pl.BlockSpec((None, N_MEM, D_MODEL), lambda b, i: (layer, b, 1)),
                  wspec],
        out_specs=pl.BlockSpec((tm, D_MODEL), lambda b, i: (b * per_b + i, 0)),
        out_shape=jax.ShapeDtypeStruct((t, D_MODEL), F32),
        compiler_params=_cparams("parallel", "arbitrary"),
        name="mem_attn_prompt",
    )(x, g, w_q, q_gain, mkv, mkv, w_o)


def _mem_attn_sample_kernel(x_ref, g_ref, wq_ref, qg_ref, mk_ref, mv_ref, wo_ref, o_ref,
                            q_scr, o_scr):
    b = pl.program_id(0)

    @pl.when(b == 0)
    def _():
        q = _dot(_rms(x_ref[...], g_ref[...]).astype(BF16), wq_ref[...])
        q_scr[...] = _headnorm(q, qg_ref[...], MEM_HEAD_DIM) * (MEM_HEAD_DIM ** -0.5)

    rows = pl.ds(pl.multiple_of(b * SAMPLE_ROWS, SAMPLE_ROWS), SAMPLE_ROWS)
    hd, nr, nk = MEM_HEAD_DIM, MEM_HEADS * SAMPLE_ROWS, N_MEM * MEM_HEADS
    q = q_scr[rows, :]
    qx = jnp.concatenate([q[:, h * hd:(h + 1) * hd] for h in range(MEM_HEADS)], 0).astype(BF16)
    s = _dot_nt(qx, mk_ref[...].reshape(nk, hd).astype(BF16))
    own = (lax.broadcasted_iota(jnp.int32, (nr, nk), 1) % MEM_HEADS
           == lax.broadcasted_iota(jnp.int32, (nr, nk), 0) // SAMPLE_ROWS)
    s = jnp.where(own, s, NEG_BIG)
    p = jnp.exp(s - jnp.max(s, -1, keepdims=True))
    o = _dot(p.astype(BF16), mv_ref[...].reshape(nk, hd).astype(BF16)) / jnp.sum(p, -1, keepdims=True)
    o_scr[rows, :] = jnp.concatenate([o[h * SAMPLE_ROWS:(h + 1) * SAMPLE_ROWS] for h in range(MEM_HEADS)], -1)

    @pl.when(b == pl.num_programs(0) - 1)
    def _():
        o_ref[...] = x_ref[...] + _dot(o_scr[...].astype(BF16), wo_ref[...])


def mem_attn_sample(x, g, w_q, q_gain, cache_k, cache_v, layer, w_o, batch):
    t = x.shape[0]
    const = lambda b: (0, 0)
    wspec = pl.BlockSpec((None, D_MODEL, D_MODEL), lambda b: (layer, 0, 0))
    cspec = pl.BlockSpec((None, None, N_MEM, MEM_HEADS, MEM_HEAD_DIM), lambda b: (layer, b, 0, 0, 0))
    return pl.pallas_call(
        _mem_attn_sample_kernel,
        grid=(batch,),
        in_specs=[pl.BlockSpec((t, D_MODEL), const),
                  pl.BlockSpec((1, D_MODEL), const),
                  wspec,
                  pl.BlockSpec((1, D_MODEL), const),
                  cspec, cspec, wspec],
        out_specs=pl.BlockSpec((t, D_MODEL), const),
        out_shape=jax.ShapeDtypeStruct((t, D_MODEL), F32),
        scratch_shapes=[pltpu.VMEM((t, D_MODEL), F32), pltpu.VMEM((t, D_MODEL), F32)],
        compiler_params=_cparams("arbitrary"),
        name="mem_attn_sample",
    )(x, g, w_q, q_gain, cache_k, cache_v, w_o)


Q_SUB = 2


def _swa_prompt_kernel(q_ref, kp_ref, kc_ref, vp_ref, vc_ref, o_ref, l_ref):
    i = pl.program_id(2)
    q = (q_ref[...] * (SWA_HEAD_DIM ** -0.5)).astype(BF16)
    k = jnp.concatenate([kp_ref[...], kc_ref[...]], 0).astype(BF16)
    v = jnp.concatenate([vp_ref[...], vc_ref[...]], 0).astype(BF16)
    qq = lax.broadcasted_iota(jnp.int32, (Q_BLK, 2 * Q_BLK), 0)
    kk = lax.broadcasted_iota(jnp.int32, (Q_BLK, 2 * Q_BLK), 1)
    band = (kk >= qq) & (kk <= qq + SWA_KEYS_BACK)
    for sub in range(Q_SUB):
        rows = slice(sub * Q_BLK, (sub + 1) * Q_BLK)
        win = slice(sub * Q_BLK, (sub + 2) * Q_BLK)
        valid = band if sub else band & ((kk >= Q_BLK) | (i > 0))
        outs, lses = [], []
        for h in range(SWA_HEADS):
            sl = slice(h * SWA_HEAD_DIM, (h + 1) * SWA_HEAD_DIM)
            s = jnp.where(valid, _dot_nt(q[rows, sl], k[win, sl]), NEG_BIG)
            m = jnp.max(s, -1, keepdims=True)
            p = jnp.exp(s - m)
            l = jnp.sum(p, -1, keepdims=True)
            outs.append(_dot(p.astype(BF16), v[win, sl]) / l)
            lses.append(jnp.broadcast_to(m + jnp.log(l), (Q_BLK, SWA_HEAD_DIM)))
        o_ref[rows, :] = jnp.concatenate(outs, -1)
        l_ref[rows, :] = jnp.concatenate(lses, -1)


def swa_prompt(qkv, group):
    batch, dil, rows, _ = qkv.shape
    blk = (None, None, Q_SUB * Q_BLK, SWA_WIDTH)
    pblk = (None, None, Q_BLK, SWA_WIDTH)
    prev = lambda i: jnp.maximum(i * Q_SUB - 1, 0)
    q_spec = pl.BlockSpec(blk, lambda b, r, i: (b, r, i, 0))
    kp_spec = pl.BlockSpec(pblk, lambda b, r, i: (b, r, prev(i), 1))
    kc_spec = pl.BlockSpec(blk, lambda b, r, i: (b, r, i, 1))
    vp_spec = pl.BlockSpec(pblk, lambda b, r, i: (b, r, prev(i), 2))
    vc_spec = pl.BlockSpec(blk, lambda b, r, i: (b, r, i, 2))
    shape = jax.ShapeDtypeStruct((batch, dil, rows, SWA_WIDTH), F32)
    return pl.pallas_call(
        _swa_prompt_kernel,
        grid=(batch, dil, rows // (Q_SUB * Q_BLK)),
        in_specs=[q_spec, kp_spec, kc_spec, vp_spec, vc_spec],
        out_specs=[q_spec, q_spec],
        out_shape=[shape, shape],
        compiler_params=_cparams("parallel", "parallel", "arbitrary"),
        name="swa_prompt_g%d" % group,
    )(qkv, qkv, qkv, qkv, qkv)


def _swa_rows_kernel(*refs, dil, n_layers):
    srcs, (ko_ref, vo_ref, buf) = refs[:2 * n_layers], refs[2 * n_layers:]
    layer = pl.program_id(0)
    pair = LANES // SWA_HEAD_DIM
    for li in range(n_layers):
        @pl.when(layer == li)
        def _(li=li):
            for src, dst in ((srcs[2 * li], ko_ref), (srcs[2 * li + 1], vo_ref)):
                for c in range(SWA_WIDTH // LANES):
                    for r in range(dil):
                        buf[pl.ds(r, src.shape[1], stride=dil), :] = src[r, :, c * LANES:(c + 1) * LANES]
                    dst[c * pair:(c + 1) * pair] = buf[...].T.reshape(pair, SWA_HEAD_DIM, buf.shape[0])


def swa_last_rows(qkvs):
    n_layers = len(qkvs)
    batch, dil, rows, _ = qkvs[0].shape
    keep = SWA_KEYS_BACK * dil
    pos = min(SWA_KEYS_BACK, 1024 // dil)
    nb = SWA_KEYS_BACK // pos
    first = (rows - SWA_KEYS_BACK) // pos
    in_specs, args = [], []
    for qkv in qkvs:
        for kind in (1, 2):
            in_specs.append(pl.BlockSpec((None, dil, pos, SWA_WIDTH),
                                         lambda l, b, a, kind=kind: (b, 0, first + a, kind)))
            args.append(qkv)
    o_spec = pl.BlockSpec((None, None, SWA_HEADS, SWA_HEAD_DIM, pos * dil), lambda l, b, a: (l, b, 0, 0, a))
    shape = jax.ShapeDtypeStruct((n_layers, batch, SWA_HEADS, SWA_HEAD_DIM, keep), F32)
    return pl.pallas_call(
        functools.partial(_swa_rows_kernel, dil=dil, n_layers=n_layers),
        grid=(n_layers, batch, nb),
        in_specs=in_specs,
        out_specs=[o_spec, o_spec],
        out_shape=[shape, shape],
        scratch_shapes=[pltpu.VMEM((pos * dil, LANES), F32)],
        compiler_params=_cparams("parallel", "parallel", "parallel"),
        name="swa_last_rows_d%d" % dil,
    )(*args)


def _swa_sample_kernel(q_ref, kn_ref, vn_ref, kt_ref, vt_ref, o_ref, l_ref, bias_c, bias_n, *, dil, n_new):
    nq, nh, hd = SAMPLE_ROWS, SWA_HEADS, SWA_HEAD_DIM
    nc = nh * nq
    rows = kt_ref.shape[-1]

    @pl.when(pl.program_id(0) == 0)
    def _():
        def bias(n, valid):
            i = lax.broadcasted_iota(jnp.int32, (nc, n), 0) % nq
            return jnp.where(valid(i, lax.broadcasted_iota(jnp.int32, (nc, n), 1)), 0.0, NEG_BIG)
        bias_c[...] = bias(rows, lambda i, c: (c >= i) & ((c - i) % dil == 0))
        bias_n[...] = bias(nq, lambda i, j: (j < n_new) & (j <= i) & ((i - j) % dil == 0))

    q = q_ref[...] * (hd ** -0.5)
    own = (lax.broadcasted_iota(jnp.int32, (nc, SWA_WIDTH), 1) // hd
           == lax.broadcasted_iota(jnp.int32, (nc, SWA_WIDTH), 0) // nq)
    qbd = jnp.where(own, jnp.concatenate([q] * nh, 0), 0.0).astype(BF16)
    s_c = _dot(qbd, kt_ref[...].reshape(SWA_WIDTH, rows).astype(BF16)) + bias_c[...]
    s_n = _dot_nt(qbd, kn_ref[...].astype(BF16)) + bias_n[...]
    m = jnp.maximum(jnp.max(s_c, -1, keepdims=True), jnp.max(s_n, -1, keepdims=True))
    p_c, p_n = jnp.exp(s_c - m), jnp.exp(s_n - m)
    l = jnp.sum(p_c, -1, keepdims=True) + jnp.sum(p_n, -1, keepdims=True)
    acc = (_dot_nt(p_c.astype(BF16), vt_ref[...].reshape(SWA_WIDTH, rows).astype(BF16))
           + _dot(p_n.astype(BF16), vn_ref[...].astype(BF16)))
    o = acc / l
    lse = jnp.broadcast_to(m + jnp.log(l), (nc, hd))
    o_ref[...] = jnp.concatenate([o[h * nq:(h + 1) * nq, h * hd:(h + 1) * hd] for h in range(nh)], -1)
    l_ref[...] = jnp.concatenate([lse[h * nq:(h + 1) * nq] for h in range(nh)], -1)


def swa_sample(qkv, cache_kt, cache_vt, layer, group, n_new):
    batch = qkv.shape[0]
    _, _, heads, hd, rows = cache_kt.shape
    dil = SWA_GROUPS[group][1]
    blk = (None, SAMPLE_ROWS, SWA_WIDTH)
    c_spec = pl.BlockSpec((None, None, heads, hd, rows), lambda b: (layer, b, 0, 0, 0))
    shape = jax.ShapeDtypeStruct((batch, SAMPLE_ROWS, SWA_WIDTH), F32)
    nc = SAMPLE_ROWS * heads
    return pl.pallas_call(
        functools.partial(_swa_sample_kernel, dil=dil, n_new=n_new),
        grid=(batch,),
        in_specs=[pl.BlockSpec(blk, lambda b: (b, 0, 0)),
                  pl.BlockSpec(blk, lambda b: (b, 0, 1)),
                  pl.BlockSpec(blk, lambda b: (b, 0, 2)),
                  c_spec, c_spec],
        out_specs=[pl.BlockSpec(blk, lambda b: (b, 0, 0))] * 2,
        out_shape=[shape, shape],
        scratch_shapes=[pltpu.VMEM((nc, rows), F32), pltpu.VMEM((nc, SAMPLE_ROWS), F32)],
        compiler_params=_cparams("arbitrary"),
        name="swa_sample_g%d" % group,
    )(qkv, qkv, qkv, cache_kt, cache_vt)


def _attn_out_kernel(x_ref, o0, o1, o2, l0, l1, l2, w_ref, out_ref, *scr, dils):
    tm = x_ref.shape[0]
    scr = list(scr)

    def rows(ref, dil):
        if dil == 1:
            return ref[0]
        buf = scr.pop()
        for c in range(buf.shape[0]):
            for r in range(dil):
                buf[c, pl.ds(r, tm // dil, stride=dil), :] = ref[r, :, c * LANES:(c + 1) * LANES]
        return jnp.concatenate([buf[c] for c in range(buf.shape[0])], -1)

    a, b, c = rows(l0, dils[0]), rows(l1, dils[1]), rows(l2, dils[2])
    m = jnp.maximum(jnp.maximum(a, b), c)
    ea, eb, ec = jnp.exp(a - m), jnp.exp(b - m), jnp.exp(c - m)
    o = (ea * rows(o0, dils[0]) + eb * rows(o1, dils[1]) + ec * rows(o2, dils[2])) / (ea + eb + ec)
    out_ref[...] = x_ref[...] + _dot(o.astype(BF16), w_ref[...])


def attn_out(x, outs, lses, w_o, layer, batch, tm):
    t = x.shape[0]
    per_b = t // batch // tm
    dils = tuple(o.shape[1] for o in outs)
    specs = [pl.BlockSpec((None, d, tm // d, SWA_WIDTH), lambda i: (i // per_b, 0, i % per_b, 0)) for d in dils]
    n_scr = 2 * sum(d > 1 for d in dils)
    return pl.pallas_call(
        functools.partial(_attn_out_kernel, dils=dils),
        grid=(t // tm,),
        in_specs=[pl.BlockSpec((tm, D_MODEL), lambda i: (i, 0))] + specs + specs
                 + [pl.BlockSpec((None, SWA_WIDTH, D_MODEL), lambda i: (layer, 0, 0))],
        out_specs=pl.BlockSpec((tm, D_MODEL), lambda i: (i, 0)),
        out_shape=jax.ShapeDtypeStruct((t, D_MODEL), F32),
        scratch_shapes=[pltpu.VMEM((SWA_WIDTH // LANES, tm, LANES), F32)] * n_scr,
        compiler_params=_cparams("parallel"),
        name="attn_out",
    )(x, *outs, *lses, w_o)


SCAN_PASSES = 8


def _ssm_param_kernel(lr_ref, li_ref, ls_ref, btr_ref, bti_ref, cr_ref, ci_ref,
                      win_ref, wout_ref, tt_ref, aqr_ref, aqi_ref, e_scr, *, q):
    g = SSM_GROUP
    lr, li = lr_ref[...], li_ref[...]
    step = jnp.exp(ls_ref[...])
    lo = lax.broadcasted_iota(jnp.int32, (1, LANES), 1) < SSM_STATE

    def powers(tau):
        mag = jnp.exp(tau * (lr * step))
        ang = tau * (li * step)
        return mag * jnp.cos(ang), mag * jnp.sin(ang)

    tau = lax.broadcasted_iota(jnp.int32, (q + 1, LANES), 0).astype(F32)
    c2, s2 = powers(tau)
    ar, ai = c2[1:2], s2[1:2]
    den = lr * lr + li * li
    fr = ((ar - 1.0) * lr + ai * li) / den
    fi = (ai * lr - (ar - 1.0) * li) / den
    btr, bti = btr_ref[...], bti_ref[...]
    bbr = fr * btr - fi * bti
    bbi = fr * bti + fi * btr
    bba = jnp.where(lo, bbr, bbi)
    bbb = jnp.where(lo, -bbi, bbr)
    pa = jnp.where(lo, c2, -s2)
    pb = jnp.where(lo, s2, c2)
    cr, ci = cr_ref[...], ci_ref[...]
    for t in range(q + 1):
        e_scr[t * g:(t + 1) * g, :] = cr * pa[t:t + 1] - ci * pb[t:t + 1]
    wout_ref[...] = e_scr[g:(q + 1) * g, :].astype(BF16)
    kt = _dot_nt(bba, e_scr[0:q * g, :], precision=lax.Precision.HIGHEST)
    for s in range(q):
        blk = kt if s == 0 else jnp.concatenate([jnp.zeros((g, g * s), F32), kt[:, :(q - s) * g]], 1)
        tt_ref[s * g:(s + 1) * g, :] = blk.astype(BF16)
    c2r, s2r = powers((q - 1.0) - tau[0:q])
    for s in range(q):
        win_ref[s * g:(s + 1) * g, :] = (c2r[s:s + 1] * bba + s2r[s:s + 1] * bbb).astype(BF16)
    pw = lax.broadcasted_iota(jnp.int32, (SCAN_PASSES, LANES), 0)
    cq, sq = powers((q * jnp.left_shift(1, pw)).astype(F32))
    aqr_ref[...] = cq
    aqi_ref[...] = jnp.where(lo, -sq, sq)


def ssm_params(lam_re, lam_im, log_step, b_re, b_im, c_re, c_im, q):
    gr, st, g = SSM_GROUPS, SSM_STATE, SSM_GROUP
    dup = lambda a: jnp.concatenate([a, a], -1)
    lr = dup(lam_re).reshape(gr, 1, LANES)
    li = dup(lam_im).reshape(gr, 1, LANES)
    ls = jnp.broadcast_to(log_step[:, None, None], (gr, 1, LANES))
    btr = dup(jnp.swapaxes(b_re, 1, 2))
    bti = dup(jnp.swapaxes(b_im, 1, 2))
    cr, ci = dup(c_re), dup(c_im)
    vec = pl.BlockSpec((None, 1, LANES), lambda i: (i, 0, 0))
    mat = pl.BlockSpec((None, g, LANES), lambda i: (i, 0, 0))
    big = pl.BlockSpec((None, q * g, LANES), lambda i: (i, 0, 0))
    pws = pl.BlockSpec((None, SCAN_PASSES, LANES), lambda i: (i, 0, 0))
    return pl.pallas_call(
        functools.partial(_ssm_param_kernel, q=q),
        grid=(gr,),
        in_specs=[vec, vec, vec, mat, mat, mat, mat],
        out_specs=[big, big, pl.BlockSpec((None, q * g, q * g), lambda i: (i, 0, 0)), pws, pws],
        out_shape=[jax.ShapeDtypeStruct((gr, q * g, LANES), BF16),
                   jax.ShapeDtypeStruct((gr, q * g, LANES), BF16),
                   jax.ShapeDtypeStruct((gr, q * g, q * g), BF16),
                   jax.ShapeDtypeStruct((gr, SCAN_PASSES, LANES), F32),
                   jax.ShapeDtypeStruct((gr, SCAN_PASSES, LANES), F32)],
        scratch_shapes=[pltpu.VMEM(((q + 1) * g, LANES), F32)],
        compiler_params=_cparams("parallel"),
        name="ssm_params_q%d" % q,
    )(lr, li, ls, btr, bti, cr, ci)


def _ssm_in_kernel(u_ref, win_ref, s_ref):
    s_ref[...] = _dot(u_ref[...], win_ref[...])


def _ssm_scan_kernel(s_ref, h0_ref, ar_ref, ai_ref, hp_ref, hf_ref, h_scr):
    @pl.when(pl.program_id(0) == 0)
    def _():
        h_scr[...] = h0_ref[...]

    def body(k, h):
        hp_ref[k] = h.astype(BF16)
        return ar_ref[...] * h + ai_ref[...] * pltpu.roll(h, SSM_STATE, 1) + s_ref[k]

    h = lax.fori_loop(0, s_ref.shape[0], body, h_scr[...])
    h_scr[...] = h
    hf_ref[...] = h


def _ssm_out_kernel(u_ref, tt_ref, hp_ref, wout_ref, y_ref):
    y_ref[...] = _dot(u_ref[...], tt_ref[...]) + _dot_nt(hp_ref[...], wout_ref[...])


def ssm_mix(h_bf16, h0_re, h0_im, params, batch, seq, q):
    win, wout, tt, aqr, aqi = params
    gr, st, g = SSM_GROUPS, SSM_STATE, SSM_GROUP
    nk = seq // q
    n = nk * batch
    qg = q * g
    u = h_bf16.reshape(batch, nk, q, gr, g).transpose(3, 1, 0, 2, 4).reshape(gr, n, qg)
    s = pl.pallas_call(
        _ssm_in_kernel,
        grid=(gr,),
        in_specs=[pl.BlockSpec((None, n, qg), lambda i: (i, 0, 0)),
                  pl.BlockSpec((None, qg, LANES), lambda i: (i, 0, 0))],
        out_specs=pl.BlockSpec((n, LANES), lambda i: (0, i)),
        out_shape=jax.ShapeDtypeStruct((n, gr * LANES), F32),
        compiler_params=_cparams("parallel"),
        name="ssm_in_q%d" % q,
    )(u, win)
    rows = batch * gr
    kc = min(nk, 32)
    tile = lambda a: jnp.broadcast_to(a[:, 0].reshape(1, gr, LANES), (batch, gr, LANES)).reshape(rows, LANES)
    h0 = jnp.concatenate([h0_re, h0_im], -1).reshape(rows, LANES)
    full = pl.BlockSpec((rows, LANES), lambda i: (0, 0))
    hp, hf = pl.pallas_call(
        _ssm_scan_kernel,
        grid=(nk // kc,),
        in_specs=[pl.BlockSpec((kc, rows, LANES), lambda i: (i, 0, 0)), full, full, full],
        out_specs=[pl.BlockSpec((kc, rows, LANES), lambda i: (i, 0, 0)), full],
        out_shape=[jax.ShapeDtypeStruct((nk, rows, LANES), BF16),
                   jax.ShapeDtypeStruct((rows, LANES), F32)],
        scratch_shapes=[pltpu.VMEM((rows, LANES), F32)],
        compiler_params=_cparams("arbitrary"),
        name="ssm_scan_q%d" % q,
    )(s.reshape(nk, rows, LANES), h0, tile(aqr), tile(aqi))
    y = pl.pallas_call(
        _ssm_out_kernel,
        grid=(gr,),
        in_specs=[pl.BlockSpec((None, n, qg), lambda i: (i, 0, 0)),
                  pl.BlockSpec((None, qg, qg), lambda i: (i, 0, 0)),
                  pl.BlockSpec((n, LANES), lambda i: (0, i)),
                  pl.BlockSpec((None, qg, LANES), lambda i: (i, 0, 0))],
        out_specs=pl.BlockSpec((None, n, qg), lambda i: (i, 0, 0)),
        out_shape=jax.ShapeDtypeStruct((gr, n, qg), F32),
        compiler_params=_cparams("parallel"),
        name="ssm_out_q%d" % q,
    )(u, tt, hp.reshape(n, gr * LANES), wout)
    y = y.reshape(gr, nk, batch, q, g).transpose(2, 1, 3, 0, 4).reshape(batch * seq, D_MODEL)
    hf = hf.reshape(batch, gr, LANES)
    return y, hf[..., :st], hf[..., st:]


SSM_TOKENS_BLK = 2048


GROUPS_PER_TILE = LANES // SSM_GROUP
N_LANE_TILES = D_MODEL // LANES


def _to_chunks_kernel(*refs, q):
    x_refs, (g_ref, u_ref, r_scr, ut_scr) = refs[:N_LANE_TILES], refs[N_LANE_TILES:]
    tb = r_scr.shape[0]
    nkb = tb // q
    ss = jnp.zeros((tb, 1), F32)
    for x_ref in x_refs:
        x = x_ref[...]
        ss = ss + jnp.sum(x * x, -1, keepdims=True)
    r_scr[...] = jnp.broadcast_to(lax.rsqrt(ss * (1.0 / D_MODEL) + RMS_EPS), (tb, LANES))
    for c, x_ref in enumerate(x_refs):
        gain = g_ref[:, c * LANES:(c + 1) * LANES]
        for s in range(q):
            rows = pl.ds(s, nkb, stride=q)
            m = x_ref[rows, :] * r_scr[rows, :] * gain
            ut_scr[:, s * SSM_GROUP:(s + 1) * SSM_GROUP, :] = m.T.reshape(GROUPS_PER_TILE, SSM_GROUP, nkb)
        for grp in range(GROUPS_PER_TILE):
            u_ref[c * GROUPS_PER_TILE + grp] = ut_scr[grp].T.astype(BF16)


def _from_chunks_kernel(y_ref, o_ref, yt_scr, o_scr, *, q):
    nkb = o_ref.shape[0] // q
    for c in range(N_LANE_TILES):
        for grp in range(GROUPS_PER_TILE):
            yt_scr[grp] = y_ref[c * GROUPS_PER_TILE + grp].T
        for t in range(q):
            o_scr[pl.ds(t, nkb, stride=q), :] = yt_scr[:, t * SSM_GROUP:(t + 1) * SSM_GROUP, :].reshape(LANES, nkb).T
        o_ref[:, c * LANES:(c + 1) * LANES] = o_scr[...]


def _ssm_group_kernel(u_ref, win_ref, wout_ref, tt_ref, ar_ref, ai_ref, y_ref, hf_ref, h_scr, *, nk):
    u = u_ref[...]
    h = _dot(u, win_ref[...])
    n = h.shape[0]
    k_idx = lax.broadcasted_iota(jnp.int32, (n, LANES), 0) % nk
    d = 1
    for j in range(SCAN_PASSES):
        if d >= nk:
            break
        hs = jnp.where(k_idx >= d, pltpu.roll(h, d, 0), 0.0)
        h = h + ar_ref[j:j + 1, :] * hs + ai_ref[j:j + 1, :] * pltpu.roll(hs, SSM_STATE, 1)
        d *= 2
    hp = jnp.where(k_idx >= 1, pltpu.roll(h, 1, 0), 0.0)
    y_ref[...] = _dot(u, tt_ref[...]) + _dot_nt(hp.astype(BF16), wout_ref[...])
    h_scr[...] = h
    hf_ref[...] = h_scr[pl.ds(nk - 1, n // nk, stride=nk), :]


def ssm_mix_prompt(x, g_mix, params, batch, seq, q):
    win, wout, tt, aqr, aqi = params
    gr, st = SSM_GROUPS, SSM_STATE
    nk = seq // q
    assert nk <= 2 ** SCAN_PASSES
    n = nk * batch
    qg = q * SSM_GROUP
    tb = SSM_TOKENS_BLK
    nkb = tb // q
    per_b = seq // tb
    tok = pl.BlockSpec((tb, D_MODEL), lambda b, i: (b * per_b + i, 0))
    chk = pl.BlockSpec((gr, nkb, qg), lambda b, i: (0, b * per_b + i, 0))
    lane_tiles = [pl.BlockSpec((tb, LANES), lambda b, i, c=c: (b * per_b + i, c)) for c in range(N_LANE_TILES)]
    u = pl.pallas_call(
        functools.partial(_to_chunks_kernel, q=q),
        grid=(batch, per_b),
        in_specs=lane_tiles + [pl.BlockSpec((1, D_MODEL), lambda b, i: (0, 0))],
        out_specs=chk,
        out_shape=jax.ShapeDtypeStruct((gr, n, qg), BF16),
        scratch_shapes=[pltpu.VMEM((tb, LANES), F32), pltpu.VMEM((GROUPS_PER_TILE, qg, nkb), F32)],
        compiler_params=_cparams("parallel", "parallel"),
        name="ssm_to_chunks",
    )(*([x] * N_LANE_TILES), g_mix)
    per_g = lambda r, c: pl.BlockSpec((None, r, c), lambda i: (i, 0, 0))
    y, hf = pl.pallas_call(
        functools.partial(_ssm_group_kernel, nk=nk),
        grid=(gr,),
        in_specs=[per_g(n, qg), per_g(qg, LANES), per_g(qg, LANES), per_g(qg, qg),
                  per_g(SCAN_PASSES, LANES), per_g(SCAN_PASSES, LANES)],
        out_specs=[per_g(n, qg), per_g(batch, LANES)],
        out_shape=[jax.ShapeDtypeStruct((gr, n, qg), F32), jax.ShapeDtypeStruct((gr, batch, LANES), F32)],
        scratch_shapes=[pltpu.VMEM((n, LANES), F32)],
        compiler_params=_cparams("parallel"),
        name="ssm_group",
    )(u, win, wout, tt, aqr, aqi)
    y = pl.pallas_call(
        functools.partial(_from_chunks_kernel, q=q),
        grid=(batch, per_b),
        in_specs=[chk],
        out_specs=tok,
        out_shape=jax.ShapeDtypeStruct((batch * seq, D_MODEL), F32),
        scratch_shapes=[pltpu.VMEM((GROUPS_PER_TILE, qg, nkb), F32), pltpu.VMEM((tb, LANES), F32)],
        compiler_params=_cparams("parallel", "parallel"),
        name="ssm_from_chunks",
    )(y)
    hf = jnp.swapaxes(hf, 0, 1)
    return y, hf[..., :st], hf[..., st:]


def _ssm_glu_kernel(x_ref, gm_ref, y_ref, d_ref, w_ref, b_ref, o_ref):
    x = x_ref[...]
    y = y_ref[...] + d_ref[...] * _rms(x, gm_ref[...])
    gl = _gelu_tanh(y)
    z = _dot(gl.astype(BF16), w_ref[...]) + b_ref[...]
    o_ref[...] = x + gl * _sigmoid(z)


def ssm_glu(x, g_mix, y, d_skip, w_glu, layer, b_glu, tm):
    t = x.shape[0]
    row = pl.BlockSpec((tm, D_MODEL), lambda i: (i, 0))
    vec = pl.BlockSpec((1, D_MODEL), lambda i: (0, 0))
    return pl.pallas_call(
        _ssm_glu_kernel,
        grid=(t // tm,),
        in_specs=[row, vec, row, vec, pl.BlockSpec((None, D_MODEL, D_MODEL), lambda i: (layer, 0, 0)), vec],
        out_specs=row,
        out_shape=jax.ShapeDtypeStruct((t, D_MODEL), F32),
        compiler_params=_cparams("parallel"),
        name="ssm_glu",
    )(x, g_mix, y, d_skip, w_glu, b_glu)


def _row(v):
    return v.reshape(1, -1).astype(F32)


def kernel(x_prompt, x_sample, mem_prompt, state_ssm_re, state_ssm_im, cache_swa0_k, cache_swa0_v, cache_swa1_k, cache_swa1_v, cache_swa2_k, cache_swa2_v, cache_mem_k, cache_mem_v, norm_mix_g, norm_mem_g, norm_memin_g, norm_ffn_g, ssm_lambda_re, ssm_lambda_im, ssm_b_re, ssm_b_im, ssm_c_re, ssm_c_im, ssm_d, ssm_log_step, ssm_w_glu, ssm_b_glu, attn_w_qkv, attn_q_norm_g, attn_k_norm_g, attn_w_o, mem_w_q, mem_w_kv, mem_q_norm_g, mem_k_norm_g, mem_w_o, ffn_w_in, ffn_w_out):
    pb, seq, _ = x_prompt.shape
    sb, dec, _ = x_sample.shape
    n_ssm = state_ssm_re.shape[0]
    n_attn = cache_swa0_k.shape[0]
    tm_p = 512
    tm_s = sb * SAMPLE_ROWS

    w_glu = ssm_w_glu.astype(BF16)
    w_qkv = attn_w_qkv.astype(BF16)
    w_ao = attn_w_o.astype(BF16)
    w_mq = mem_w_q.astype(BF16)
    w_mkv = mem_w_kv.astype(BF16)
    w_mo = mem_w_o.astype(BF16)
    w_fi = ffn_w_in.astype(BF16)
    w_fo = ffn_w_out.astype(BF16)

    head_gain = [jnp.concatenate([jnp.tile(attn_q_norm_g[j], (1, SWA_HEADS)),
                                  jnp.tile(attn_k_norm_g[j], (1, SWA_HEADS))], 1).reshape(N_SWA, 1, 2 * SWA_WIDTH)
                 for j in range(n_attn)]
    mem_q_gain = [_row(jnp.tile(mem_q_norm_g[i], MEM_HEADS)) for i in range(DEPTH)]
    mem_k_gain = jnp.tile(mem_k_norm_g, (1, MEM_HEADS)).reshape(DEPTH, 1, D_MODEL)

    mkv = mem_kv(mem_prompt.reshape(pb * N_MEM, D_MODEL), norm_memin_g.reshape(DEPTH, 1, D_MODEL),
                 w_mkv, mem_k_gain)
    mkv5 = mkv.reshape(DEPTH, pb, N_MEM, 2, MEM_HEADS, MEM_HEAD_DIM)
    p_mem_k, p_mem_v = mkv5[:, :, :, 0], mkv5[:, :, :, 1]
    mkv_b = mkv.reshape(DEPTH, pb * N_MEM, 2 * D_MODEL)

    ssm_p = [[ssm_params(ssm_lambda_re[j], ssm_lambda_im[j], ssm_log_step[j], ssm_b_re[j], ssm_b_im[j],
                         ssm_c_re[j], ssm_c_im[j], q) for q in (SSM_Q_PROMPT, dec)] for j in range(n_ssm)]

    rows_minor = lambda c: jnp.transpose(c, (0, 1, 3, 4, 2))
    caches_k = tuple(rows_minor(c) for c in (cache_swa0_k, cache_swa1_k, cache_swa2_k))
    caches_v = tuple(rows_minor(c) for c in (cache_swa0_v, cache_swa1_v, cache_swa2_v))
    dils = tuple(d for _, d in SWA_GROUPS)
    assert all(min(w, seq) == SWA_KEYS_BACK * d for w, d in SWA_GROUPS)

    xp = x_prompt.reshape(pb * seq, D_MODEL)
    xs = jnp.pad(x_sample, ((0, 0), (0, SAMPLE_ROWS - dec), (0, 0))).reshape(tm_s, D_MODEL)

    p_ssm_re, p_ssm_im, s_ssm_re, s_ssm_im = [], [], [], []
    p_qkv = [[] for _ in SWA_GROUPS]
    s_swa_k = [[] for _ in SWA_GROUPS]
    s_swa_v = [[] for _ in SWA_GROUPS]

    for i in range(DEPTH):
        j = i // 2
        g_mix = _row(norm_mix_g[i])
        if i % 2 == 0:
            d_skip, b_glu = _row(ssm_d[j]), _row(ssm_b_glu[j])
            y, fr, fi = ssm_mix_prompt(xp, g_mix, ssm_p[j][0], pb, seq, SSM_Q_PROMPT)
            p_ssm_re.append(fr)
            p_ssm_im.append(fi)
            xp = ssm_glu(xp, g_mix, y, d_skip, w_glu, j, b_glu, tm_p)
            hb = norm_cast(xs, g_mix, tm_s).reshape(sb, SAMPLE_ROWS, D_MODEL)[:, :dec].reshape(sb * dec, D_MODEL)
            y, fr, fi = ssm_mix(hb, state_ssm_re[j], state_ssm_im[j], ssm_p[j][1], sb, dec, dec)
            s_ssm_re.append(fr)
            s_ssm_im.append(fi)
            y = jnp.pad(y.reshape(sb, dec, D_MODEL), ((0, 0), (0, SAMPLE_ROWS - dec), (0, 0))).reshape(tm_s, D_MODEL)
            xs = ssm_glu(xs, g_mix, y, d_skip, w_glu, j, b_glu, tm_s)
        else:
            qkvs = qkv_proj(xp, g_mix, w_qkv, j, head_gain[j], pb, dils, tm_p)
            outs, lses = [], []
            for g in range(N_SWA):
                p_qkv[g].append(qkvs[g])
                o, lse = swa_prompt(qkvs[g], g)
                outs.append(o)
                lses.append(lse)
            xp = attn_out(xp, outs, lses, w_ao, j, pb, tm_p)
            qkvs = qkv_proj(xs, g_mix, w_qkv, j, head_gain[j], 1, (1,) * N_SWA, tm_s)
            outs, lses = [], []
            for g in range(N_SWA):
                qkv3 = qkvs[g].reshape(sb, SAMPLE_ROWS, 3 * SWA_WIDTH)
                qkv5 = qkv3.reshape(sb, SAMPLE_ROWS, 3, SWA_HEADS, SWA_HEAD_DIM)
                s_swa_k[g].append(qkv5[:, :dec, 1])
                s_swa_v[g].append(qkv5[:, :dec, 2])
                o, lse = swa_sample(qkv3, caches_k[g], caches_v[g], j, g, dec)
                outs.append(o.reshape(1, 1, tm_s, SWA_WIDTH))
                lses.append(lse.reshape(1, 1, tm_s, SWA_WIDTH))
            xs = attn_out(xs, outs, lses, w_ao, j, 1, tm_s)

        g_mem = _row(norm_mem_g[i])
        xp = mem_attn_prompt(xp, g_mem, w_mq, mem_q_gain[i], mkv_b, i, w_mo, pb, tm_p)
        xs = mem_attn_sample(xs, g_mem, w_mq, mem_q_gain[i], cache_mem_k, cache_mem_v, i, w_mo, sb)
        g_ffn = _row(norm_ffn_g[i])
        xp = swiglu_block(xp, g_ffn, w_fi, w_fo, i, 1024)
        xs = swiglu_block(xs, g_ffn, w_fi, w_fo, i, tm_s)

    rows_major = lambda c: jnp.transpose(c, (0, 1, 4, 2, 3))
    p_swa = [[rows_major(c) for c in swa_last_rows(p_qkv[g])] for g in range(N_SWA)]
    y_prompt = xp.reshape(pb, seq, D_MODEL)
    y_sample = xs.reshape(sb, SAMPLE_ROWS, D_MODEL)[:, :dec]
    st = lambda a: jnp.stack(a, 0)
    return (y_prompt, y_sample,
            st(p_ssm_re), st(p_ssm_im),
            p_swa[0][0], p_swa[0][1], p_swa[1][0], p_swa[1][1], p_swa[2][0], p_swa[2][1],
            p_mem_k, p_mem_v,
            st(s_ssm_re), st(s_ssm_im),
            st(s_swa_k[0]), st(s_swa_v[0]), st(s_swa_k[1]), st(s_swa_v[1]), st(s_swa_k[2]), st(s_swa_v[2]))
```

```python
import functools
import math

import jax
import jax.numpy as jnp
from jax import lax
from jax.experimental import pallas as pl
from jax.experimental.pallas import tpu as pltpu

F32 = jnp.float32
BF16 = jnp.bfloat16

D_MODEL = 1024
DEPTH = 4
SSM_GROUP = 16
SSM_GROUPS = D_MODEL // SSM_GROUP
SSM_STATE = 64
SWA_GROUPS = ((128, 1), (512, 4), (2048, 16))
N_SWA = len(SWA_GROUPS)
SWA_HEADS = 8
SWA_HEAD_DIM = 64
SWA_WIDTH = SWA_HEADS * SWA_HEAD_DIM
QKV_WIDTH = 3 * N_SWA * SWA_WIDTH
SWA_KEYS_BACK = 128
N_MEM = 256
MEM_HEADS = 4
MEM_HEAD_DIM = D_MODEL // MEM_HEADS
D_FF = -(-8 * D_MODEL // (3 * 256)) * 256
RMS_EPS = 1e-6
NEG_BIG = -1e30

LANES = 128
SUBLANES = 8
VMEM_LIMIT = 48 * 1024 * 1024
Q_BLK = 128
FF_BLK = 256
SAMPLE_ROWS = SUBLANES
SSM_Q_PROMPT = 16


def _cparams(*sem):
    return pltpu.CompilerParams(dimension_semantics=sem, vmem_limit_bytes=VMEM_LIMIT)


def _dot(a, b):
    return jnp.dot(a, b, preferred_element_type=F32)


def _dot_nt(a, b, precision=None):
    return lax.dot_general(a, b, (((1,), (1,)), ((), ())), preferred_element_type=F32,
                           precision=precision)


def _rms(x, g):
    return x * lax.rsqrt(jnp.mean(x * x, -1, keepdims=True) + RMS_EPS) * g


def _headnorm(y, g_row, hd):
    tm, n = y.shape
    outs = []
    if hd % LANES == 0:
        for c in range(n // hd):
            yc = y[:, c * hd:(c + 1) * hd]
            ms = jnp.mean(yc * yc, -1, keepdims=True)
            outs.append(yc * lax.rsqrt(ms + RMS_EPS))
    else:
        assert 2 * hd == LANES
        lo = lax.broadcasted_iota(jnp.int32, (tm, LANES), 1) < hd
        for c in range(n // LANES):
            yc = y[:, c * LANES:(c + 1) * LANES]
            sq = yc * yc
            s_lo = jnp.sum(jnp.where(lo, sq, 0.0), -1, keepdims=True)
            s_hi = jnp.sum(jnp.where(lo, 0.0, sq), -1, keepdims=True)
            ms = jnp.where(lo, s_lo, s_hi) * (1.0 / hd)
            outs.append(yc * lax.rsqrt(ms + RMS_EPS))
    return jnp.concatenate(outs, -1) * g_row


def _sigmoid(x):
    return 1.0 / (1.0 + jnp.exp(-x))


def _gelu_tanh(x):
    return 0.5 * x * (1.0 + jnp.tanh(math.sqrt(2.0 / math.pi) * (x + 0.044715 * (x * x * x))))


def _norm_cast_kernel(x_ref, g_ref, o_ref):
    o_ref[...] = _rms(x_ref[...], g_ref[...]).astype(BF16)


def norm_cast(x, g, tm):
    t = x.shape[0]
    return pl.pallas_call(
        _norm_cast_kernel,
        grid=(t // tm,),
        in_specs=[pl.BlockSpec((tm, D_MODEL), lambda i: (i, 0)),
                  pl.BlockSpec((1, D_MODEL), lambda i: (0, 0))],
        out_specs=pl.BlockSpec((tm, D_MODEL), lambda i: (i, 0)),
        out_shape=jax.ShapeDtypeStruct((t, D_MODEL), BF16),
        compiler_params=_cparams("parallel"),
        name="norm_cast",
    )(x, g)


def _qkv_kernel(x_ref, g_ref, w_ref, hg_ref, o0_ref, o1_ref, o2_ref, y_ref, *, dils):
    tm = x_ref.shape[0]
    w = SWA_WIDTH
    xn = _rms(x_ref[...], g_ref[...]).astype(BF16)
    for grp, (o_ref, dil) in enumerate(zip((o0_ref, o1_ref, o2_ref), dils)):
        col = lambda kind: slice((kind * N_SWA + grp) * w, (kind * N_SWA + grp + 1) * w)
        parts = (_headnorm(_dot(xn, w_ref[:, col(0)]), hg_ref[grp, :, :w], SWA_HEAD_DIM),
                 _headnorm(_dot(xn, w_ref[:, col(1)]), hg_ref[grp, :, w:], SWA_HEAD_DIM),
                 _dot(xn, w_ref[:, col(2)]))
        if dil == 1:
            for kind, y in enumerate(parts):
                o_ref[0, :, kind * w:(kind + 1) * w] = y
        else:
            per = w // LANES
            for c in range(y_ref.shape[0]):
                y_ref[c] = parts[c // per][:, (c % per) * LANES:(c % per + 1) * LANES]
                for r in range(dil):
                    o_ref[r, :, c * LANES:(c + 1) * LANES] = y_ref[c, pl.ds(r, tm // dil, stride=dil), :]


def qkv_proj(x, g, w, layer, head_gain, batch, dils, tm):
    t = x.shape[0]
    seq = t // batch
    per_b = seq // tm
    gw = 3 * SWA_WIDTH
    return pl.pallas_call(
        functools.partial(_qkv_kernel, dils=dils),
        grid=(t // tm,),
        in_specs=[pl.BlockSpec((tm, D_MODEL), lambda i: (i, 0)),
                  pl.BlockSpec((1, D_MODEL), lambda i: (0, 0)),
                  pl.BlockSpec((None, D_MODEL, QKV_WIDTH), lambda i: (layer, 0, 0), pipeline_mode=pl.Buffered(1)),
                  pl.BlockSpec((N_SWA, 1, 2 * SWA_WIDTH), lambda i: (0, 0, 0))],
        out_specs=[pl.BlockSpec((None, d, tm // d, gw), lambda i: (i // per_b, 0, i % per_b, 0))
                   for d in dils],
        out_shape=[jax.ShapeDtypeStruct((batch, d, seq // d, gw), F32) for d in dils],
        scratch_shapes=[pltpu.VMEM((gw // LANES, tm, LANES), F32)],
        compiler_params=_cparams("parallel"),
        name="qkv_proj",
    )(x, g, w, head_gain)


def _mem_kv_kernel(x_ref, g_ref, w_ref, kg_ref, o_ref):
    j = pl.program_id(1)
    y = _dot(_rms(x_ref[...], g_ref[...]).astype(BF16), w_ref[...])

    @pl.when(j == 0)
    def _():
        o_ref[...] = _headnorm(y, kg_ref[...], MEM_HEAD_DIM)

    @pl.when(j == 1)
    def _():
        o_ref[...] = y


def mem_kv(mem, g_in, w_kv, k_gain):
    t = mem.shape[0]
    return pl.pallas_call(
        _mem_kv_kernel,
        grid=(DEPTH, 2),
        in_specs=[pl.BlockSpec((t, D_MODEL), lambda l, j: (0, 0)),
                  pl.BlockSpec((None, 1, D_MODEL), lambda l, j: (l, 0, 0)),
                  pl.BlockSpec((None, D_MODEL, D_MODEL), lambda l, j: (l, 0, j)),
                  pl.BlockSpec((None, 1, D_MODEL), lambda l, j: (l, 0, 0))],
        out_specs=pl.BlockSpec((None, t, D_MODEL), lambda l, j: (l, 0, j)),
        out_shape=jax.ShapeDtypeStruct((DEPTH, t, 2 * D_MODEL), F32),
        compiler_params=_cparams("parallel", "arbitrary"),
        name="mem_kv",
    )(mem, g_in, w_kv, k_gain)


def _swiglu_kernel(x_ref, g_ref, wi_ref, wo_ref, o_ref):
    x = x_ref[...]
    xn = _rms(x, g_ref[...]).astype(BF16)
    acc = x
    for c in range(D_FF // FF_BLK):
        lo = c * FF_BLK
        gate = _dot(xn, wi_ref[:, lo:lo + FF_BLK])
        up = _dot(xn, wi_ref[:, D_FF + lo:D_FF + lo + FF_BLK])
        act = gate * _sigmoid(gate) * up
        acc = acc + _dot(act.astype(BF16), wo_ref[lo:lo + FF_BLK, :])
    o_ref[...] = acc


def swiglu_block(x, g, w_in, w_out, layer, tm):
    t = x.shape[0]
    once = pl.Buffered(1)
    return pl.pallas_call(
        _swiglu_kernel,
        grid=(t // tm,),
        in_specs=[pl.BlockSpec((tm, D_MODEL), lambda i: (i, 0)),
                  pl.BlockSpec((1, D_MODEL), lambda i: (0, 0)),
                  pl.BlockSpec((None, D_MODEL, 2 * D_FF), lambda i: (layer, 0, 0), pipeline_mode=once),
                  pl.BlockSpec((None, D_FF, D_MODEL), lambda i: (layer, 0, 0), pipeline_mode=once)],
        out_specs=pl.BlockSpec((tm, D_MODEL), lambda i: (i, 0)),
        out_shape=jax.ShapeDtypeStruct((t, D_MODEL), F32),
        compiler_params=_cparams("parallel"),
        name="swiglu",
    )(x, g, w_in, w_out)


def _mem_heads(q, key_head, value_head):
    outs = []
    for h in range(MEM_HEADS):
        sl = slice(h * MEM_HEAD_DIM, (h + 1) * MEM_HEAD_DIM)
        s = _dot_nt(q[:, sl].astype(BF16), key_head(h))
        m = jnp.max(s, -1, keepdims=True)
        p = jnp.exp(s - m)
        l = jnp.sum(p, -1, keepdims=True)
        outs.append(_dot(p.astype(BF16), value_head(h)) / l)
    return jnp.concatenate(outs, -1)


def _mem_attn_kernel(x_ref, g_ref, wq_ref, qg_ref, mk_ref, mv_ref, wo_ref, o_ref):
    x = x_ref[...]
    q = _dot(_rms(x, g_ref[...]).astype(BF16), wq_ref[...])
    q = _headnorm(q, qg_ref[...], MEM_HEAD_DIM) * (MEM_HEAD_DIM ** -0.5)
    head = lambda ref: lambda h: ref[:, h * MEM_HEAD_DIM:(h + 1) * MEM_HEAD_DIM].astype(BF16)
    o = _mem_heads(q, head(mk_ref), head(mv_ref))
    o_ref[...] = x + _dot(o.astype(BF16), wo_ref[...])


def mem_attn_prompt(x, g, w_q, q_gain, mkv, layer, w_o, batch, tm):
    t = x.shape[0]
    per_b = t // batch // tm
    wspec = pl.BlockSpec((None, D_MODEL, D_MODEL), lambda b, i: (layer, 0, 0))
    return pl.pallas_call(
        _mem_attn_kernel,
        grid=(batch, per_b),
        in_specs=[pl.BlockSpec((tm, D_MODEL), lambda b, i: (b * per_b + i, 0)),
                  pl.BlockSpec((1, D_MODEL), lambda b, i: (0, 0)),
                  wspec,
                  pl.BlockSpec((1, D_MODEL), lambda b, i: (0, 0)),
                  pl.BlockSpec((None, N_MEM, D_MODEL), lambda b, i: (layer, b, 0)),
                  pl.BlockSpec((None, N_MEM, D_MODEL), lambda b, i: (layer, b, 1)),
                  wspec],
        out_specs=pl.BlockSpec((tm, D_MODEL), lambda b, i: (b * per_b + i, 0)),
        out_shape=jax.ShapeDtypeStruct((t, D_MODEL), F32),
        compiler_params=_cparams("parallel", "arbitrary"),
        name="mem_attn_prompt",
    )(x, g, w_q, q_gain, mkv, mkv, w_o)


def _mem_attn_sample_kernel(x_ref, g_ref, wq_ref, qg_ref, mk_ref, mv_ref, wo_ref, o_ref,
                            q_scr, o_scr):
    b = pl.program_id(0)

    @pl.when(b == 0)
    def _():
        q = _dot(_rms(x_ref[...], g_ref[...]).astype(BF16), wq_ref[...])
        q_scr[...] = _headnorm(q, qg_ref[...], MEM_HEAD_DIM) * (MEM_HEAD_DIM ** -0.5)

    rows = pl.ds(pl.multiple_of(b * SAMPLE_ROWS, SAMPLE_ROWS), SAMPLE_ROWS)
    hd, nr, nk = MEM_HEAD_DIM, MEM_HEADS * SAMPLE_ROWS, N_MEM * MEM_HEADS
    q = q_scr[rows, :]
    qx = jnp.concatenate([q[:, h * hd:(h + 1) * hd] for h in range(MEM_HEADS)], 0).astype(BF16)
    s = _dot_nt(qx, mk_ref[...].reshape(nk, hd).astype(BF16))
    own = (lax.broadcasted_iota(jnp.int32, (nr, nk), 1) % MEM_HEADS
           == lax.broadcasted_iota(jnp.int32, (nr, nk), 0) // SAMPLE_ROWS)
    s = jnp.where(own, s, NEG_BIG)
    p = jnp.exp(s - jnp.max(s, -1, keepdims=True))
    o = _dot(p.astype(BF16), mv_ref[...].reshape(nk, hd).astype(BF16)) / jnp.sum(p, -1, keepdims=True)
    o_scr[rows, :] = jnp.concatenate([o[h * SAMPLE_ROWS:(h + 1) * SAMPLE_ROWS] for h in range(MEM_HEADS)], -1)

    @pl.when(b == pl.num_programs(0) - 1)
    def _():
        o_ref[...] = x_ref[...] + _dot(o_scr[...].astype(BF16), wo_ref[...])


def mem_attn_sample(x, g, w_q, q_gain, cache_k, cache_v, layer, w_o, batch):
    t = x.shape[0]
    const = lambda b: (0, 0)
    wspec = pl.BlockSpec((None, D_MODEL, D_MODEL), lambda b: (layer, 0, 0))
    cspec = pl.BlockSpec((None, None, N_MEM, MEM_HEADS, MEM_HEAD_DIM), lambda b: (layer, b, 0, 0, 0))
    return pl.pallas_call(
        _mem_attn_sample_kernel,
        grid=(batch,),
        in_specs=[pl.BlockSpec((t, D_MODEL), const),
                  pl.BlockSpec((1, D_MODEL), const),
                  wspec,
                  pl.BlockSpec((1, D_MODEL), const),
                  cspec, cspec, wspec],
        out_specs=pl.BlockSpec((t, D_MODEL), const),
        out_shape=jax.ShapeDtypeStruct((t, D_MODEL), F32),
        scratch_shapes=[pltpu.VMEM((t, D_MODEL), F32), pltpu.VMEM((t, D_MODEL), F32)],
        compiler_params=_cparams("arbitrary"),
        name="mem_attn_sample",
    )(x, g, w_q, q_gain, cache_k, cache_v, w_o)


Q_SUB = 2


def _swa_prompt_kernel(q_ref, kp_ref, kc_ref, vp_ref, vc_ref, o_ref, l_ref):
    i = pl.program_id(2)
    q = (q_ref[...] * (SWA_HEAD_DIM ** -0.5)).astype(BF16)
    k = jnp.concatenate([kp_ref[...], kc_ref[...]], 0).astype(BF16)
    v = jnp.concatenate([vp_ref[...], vc_ref[...]], 0).astype(BF16)
    qq = lax.broadcasted_iota(jnp.int32, (Q_BLK, 2 * Q_BLK), 0)
    kk = lax.broadcasted_iota(jnp.int32, (Q_BLK, 2 * Q_BLK), 1)
    band = (kk >= qq) & (kk <= qq + SWA_KEYS_BACK)
    head_lane = lax.broadcasted_iota(jnp.int32, (Q_BLK, LANES), 1)
    for sub in range(Q_SUB):
        rows = slice(sub * Q_BLK, (sub + 1) * Q_BLK)
        win = slice(sub * Q_BLK, (sub + 2) * Q_BLK)
        valid = band if sub else band & ((kk >= Q_BLK) | (i > 0))
        outs = []
        lse = jnp.zeros((Q_BLK, LANES), F32)
        for h in range(SWA_HEADS):
            sl = slice(h * SWA_HEAD_DIM, (h + 1) * SWA_HEAD_DIM)
            s = jnp.where(valid, _dot_nt(q[rows, sl], k[win, sl]), NEG_BIG)
            m = jnp.max(s, -1, keepdims=True)
            p = jnp.exp(s - m)
            l = jnp.sum(p, -1, keepdims=True)
            outs.append(_dot(p.astype(BF16), v[win, sl]) / l)
            lse = jnp.where(head_lane == h, m + jnp.log(l), lse)
        o_ref[rows, :] = jnp.concatenate(outs, -1)
        l_ref[rows, :] = lse


def swa_prompt(qkv, group):
    batch, dil, rows, _ = qkv.shape
    blk = (None, None, Q_SUB * Q_BLK, SWA_WIDTH)
    pblk = (None, None, Q_BLK, SWA_WIDTH)
    prev = lambda i: jnp.maximum(i * Q_SUB - 1, 0)
    q_spec = pl.BlockSpec(blk, lambda b, r, i: (b, r, i, 0))
    kp_spec = pl.BlockSpec(pblk, lambda b, r, i: (b, r, prev(i), 1))
    kc_spec = pl.BlockSpec(blk, lambda b, r, i: (b, r, i, 1))
    vp_spec = pl.BlockSpec(pblk, lambda b, r, i: (b, r, prev(i), 2))
    vc_spec = pl.BlockSpec(blk, lambda b, r, i: (b, r, i, 2))
    l_spec = pl.BlockSpec((None, None, Q_SUB * Q_BLK, LANES), lambda b, r, i: (b, r, i, 0))
    return pl.pallas_call(
        _swa_prompt_kernel,
        grid=(batch, dil, rows // (Q_SUB * Q_BLK)),
        in_specs=[q_spec, kp_spec, kc_spec, vp_spec, vc_spec],
        out_specs=[q_spec, l_spec],
        out_shape=[jax.ShapeDtypeStruct((batch, dil, rows, SWA_WIDTH), F32),
                   jax.ShapeDtypeStruct((batch, dil, rows, LANES), F32)],
        compiler_params=_cparams("parallel", "parallel", "arbitrary"),
        name="swa_prompt_g%d" % group,
    )(qkv, qkv, qkv, qkv, qkv)


def _swa_rows_kernel(*refs, dil, n_layers):
    srcs, (ko_ref, vo_ref, buf) = refs[:2 * n_layers], refs[2 * n_layers:]
    layer = pl.program_id(0)
    pair = LANES // SWA_HEAD_DIM
    for li in range(n_layers):
        @pl.when(layer == li)
        def _(li=li):
            for src, dst in ((srcs[2 * li], ko_ref), (srcs[2 * li + 1], vo_ref)):
                for c in range(SWA_WIDTH // LANES):
                    for r in range(dil):
                        buf[pl.ds(r, src.shape[1], stride=dil), :] = src[r, :, c * LANES:(c + 1) * LANES]
                    dst[c * pair:(c + 1) * pair] = buf[...].T.reshape(pair, SWA_HEAD_DIM, buf.shape[0])


def swa_last_rows(qkvs):
    n_layers = len(qkvs)
    batch, dil, rows, _ = qkvs[0].shape
    keep = SWA_KEYS_BACK * dil
    pos = min(SWA_KEYS_BACK, 1024 // dil)
    nb = SWA_KEYS_BACK // pos
    first = (rows - SWA_KEYS_BACK) // pos
    in_specs, args = [], []
    for qkv in qkvs:
        for kind in (1, 2):
            in_specs.append(pl.BlockSpec((None, dil, pos, SWA_WIDTH),
                                         lambda l, b, a, kind=kind: (b, 0, first + a, kind)))
            args.append(qkv)
    o_spec = pl.BlockSpec((None, None, SWA_HEADS, SWA_HEAD_DIM, pos * dil), lambda l, b, a: (l, b, 0, 0, a))
    shape = jax.ShapeDtypeStruct((n_layers, batch, SWA_HEADS, SWA_HEAD_DIM, keep), F32)
    return pl.pallas_call(
        functools.partial(_swa_rows_kernel, dil=dil, n_layers=n_layers),
        grid=(n_layers, batch, nb),
        in_specs=in_specs,
        out_specs=[o_spec, o_spec],
        out_shape=[shape, shape],
        scratch_shapes=[pltpu.VMEM((pos * dil, LANES), F32)],
        compiler_params=_cparams("parallel", "parallel", "parallel"),
        name="swa_last_rows_d%d" % dil,
    )(*args)


def _swa_sample_kernel(q_ref, kn_ref, vn_ref, kt_ref, vt_ref, o_ref, l_ref, bias_c, bias_n, *, dil, n_new):
    nq, nh, hd = SAMPLE_ROWS, SWA_HEADS, SWA_HEAD_DIM
    nc = nh * nq
    rows = kt_ref.shape[-1]

    @pl.when(pl.program_id(0) == 0)
    def _():
        def bias(n, valid):
            i = lax.broadcasted_iota(jnp.int32, (nc, n), 0) % nq
            return jnp.where(valid(i, lax.broadcasted_iota(jnp.int32, (nc, n), 1)), 0.0, NEG_BIG)
        bias_c[...] = bias(rows, lambda i, c: (c >= i) & ((c - i) % dil == 0))
        bias_n[...] = bias(nq, lambda i, j: (j < n_new) & (j <= i) & ((i - j) % dil == 0))

    q = q_ref[...] * (hd ** -0.5)
    own = (lax.broadcasted_iota(jnp.int32, (nc, SWA_WIDTH), 1) // hd
           == lax.broadcasted_iota(jnp.int32, (nc, SWA_WIDTH), 0) // nq)
    qbd = jnp.where(own, jnp.concatenate([q] * nh, 0), 0.0).astype(BF16)
    s_c = _dot(qbd, kt_ref[...].reshape(SWA_WIDTH, rows).astype(BF16)) + bias_c[...]
    s_n = _dot_nt(qbd, kn_ref[...].astype(BF16)) + bias_n[...]
    m = jnp.maximum(jnp.max(s_c, -1, keepdims=True), jnp.max(s_n, -1, keepdims=True))
    p_c, p_n = jnp.exp(s_c - m), jnp.exp(s_n - m)
    l = jnp.sum(p_c, -1, keepdims=True) + jnp.sum(p_n, -1, keepdims=True)
    acc = (_dot_nt(p_c.astype(BF16), vt_ref[...].reshape(SWA_WIDTH, rows).astype(BF16))
           + _dot(p_n.astype(BF16), vn_ref[...].astype(BF16)))
    o = acc / l
    o_ref[...] = jnp.concatenate([o[h * nq:(h + 1) * nq, h * hd:(h + 1) * hd] for h in range(nh)], -1)
    lse_col = m + jnp.log(l)
    head_lane = lax.broadcasted_iota(jnp.int32, (nq, LANES), 1)
    lse = jnp.zeros((nq, LANES), F32)
    for h in range(nh):
        lse = jnp.where(head_lane == h, lse_col[h * nq:(h + 1) * nq], lse)
    l_ref[...] = lse


def swa_sample(qkv, cache_kt, cache_vt, layer, group, n_new):
    batch = qkv.shape[0]
    _, _, heads, hd, rows = cache_kt.shape
    dil = SWA_GROUPS[group][1]
    blk = (None, SAMPLE_ROWS, SWA_WIDTH)
    c_spec = pl.BlockSpec((None, None, heads, hd, rows), lambda b: (layer, b, 0, 0, 0))
    shape = jax.ShapeDtypeStruct((batch, SAMPLE_ROWS, SWA_WIDTH), F32)
    nc = SAMPLE_ROWS * heads
    return pl.pallas_call(
        functools.partial(_swa_sample_kernel, dil=dil, n_new=n_new),
        grid=(batch,),
        in_specs=[pl.BlockSpec(blk, lambda b: (b, 0, 0)),
                  pl.BlockSpec(blk, lambda b: (b, 0, 1)),
                  pl.BlockSpec(blk, lambda b: (b, 0, 2)),
                  c_spec, c_spec],
        out_specs=[pl.BlockSpec(blk, lambda b: (b, 0, 0)),
                   pl.BlockSpec((None, SAMPLE_ROWS, LANES), lambda b: (b, 0, 0))],
        out_shape=[shape, jax.ShapeDtypeStruct((batch, SAMPLE_ROWS, LANES), F32)],
        scratch_shapes=[pltpu.VMEM((nc, rows), F32), pltpu.VMEM((nc, SAMPLE_ROWS), F32)],
        compiler_params=_cparams("arbitrary"),
        name="swa_sample_g%d" % group,
    )(qkv, qkv, qkv, cache_kt, cache_vt)


def _attn_out_kernel(x_ref, o0, o1, o2, l0, l1, l2, w_ref, out_ref, *scr, dils):
    tm = x_ref.shape[0]
    scr = list(scr)

    def rows(ref, dil):
        if dil == 1:
            return ref[0]
        buf = scr.pop()
        for c in range(buf.shape[0]):
            for r in range(dil):
                buf[c, pl.ds(r, tm // dil, stride=dil), :] = ref[r, :, c * LANES:(c + 1) * LANES]
        return jnp.concatenate([buf[c] for c in range(buf.shape[0])], -1)

    a, b, c = rows(l0, dils[0]), rows(l1, dils[1]), rows(l2, dils[2])
    m = jnp.maximum(jnp.maximum(a, b), c)
    ea, eb, ec = jnp.exp(a - m), jnp.exp(b - m), jnp.exp(c - m)
    den = ea + eb + ec
    spread = (lax.broadcasted_iota(jnp.int32, (LANES, SWA_WIDTH), 1) // SWA_HEAD_DIM
              == lax.broadcasted_iota(jnp.int32, (LANES, SWA_WIDTH), 0)).astype(BF16)

    def per_lane(wgt):
        hi = wgt.astype(BF16)
        lo = (wgt - hi.astype(F32)).astype(BF16)
        return _dot(hi, spread) + _dot(lo, spread)

    o = (per_lane(ea / den) * rows(o0, dils[0]) + per_lane(eb / den) * rows(o1, dils[1])
         + per_lane(ec / den) * rows(o2, dils[2]))
    out_ref[...] = x_ref[...] + _dot(o.astype(BF16), w_ref[...])


def attn_out(x, outs, lses, w_o, layer, batch, tm):
    t = x.shape[0]
    per_b = t // batch // tm
    dils = tuple(o.shape[1] for o in outs)
    spec = lambda d, w: pl.BlockSpec((None, d, tm // d, w), lambda i: (i // per_b, 0, i % per_b, 0))
    n_scr = sum(d > 1 for d in dils)
    return pl.pallas_call(
        functools.partial(_attn_out_kernel, dils=dils),
        grid=(t // tm,),
        in_specs=[pl.BlockSpec((tm, D_MODEL), lambda i: (i, 0))]
                 + [spec(d, SWA_WIDTH) for d in dils] + [spec(d, LANES) for d in dils]
                 + [pl.BlockSpec((None, SWA_WIDTH, D_MODEL), lambda i: (layer, 0, 0))],
        out_specs=pl.BlockSpec((tm, D_MODEL), lambda i: (i, 0)),
        out_shape=jax.ShapeDtypeStruct((t, D_MODEL), F32),
        scratch_shapes=[pltpu.VMEM((SWA_WIDTH // LANES, tm, LANES), F32)] * n_scr
                       + [pltpu.VMEM((1, tm, LANES), F32)] * n_scr,
        compiler_params=_cparams("parallel"),
        name="attn_out",
    )(x, *outs, *lses, w_o)


SCAN_PASSES = 8
GROUPS_PER_STEP = 8


def _ssm_param_kernel(*refs, q):
    *per_group, e_scr = refs
    for i in range(GROUPS_PER_STEP):
        _ssm_param_group(*[r.at[i] for r in per_group], e_scr, q=q)


def _ssm_param_group(lr_ref, li_ref, ls_ref, btr_ref, bti_ref, cr_ref, ci_ref,
                     win_ref, wout_ref, tt_ref, aqr_ref, aqi_ref, e_scr, *, q):
    g = SSM_GROUP
    lr, li = lr_ref[...], li_ref[...]
    step = jnp.exp(ls_ref[...])
    lo = lax.broadcasted_iota(jnp.int32, (1, LANES), 1) < SSM_STATE

    def powers(tau):
        mag = jnp.exp(tau * (lr * step))
        ang = tau * (li * step)
        return mag * jnp.cos(ang), mag * jnp.sin(ang)

    tau = lax.broadcasted_iota(jnp.int32, (q + 1, LANES), 0).astype(F32)
    c2, s2 = powers(tau)
    ar, ai = c2[1:2], s2[1:2]
    den = lr * lr + li * li
    fr = ((ar - 1.0) * lr + ai * li) / den
    fi = (ai * lr - (ar - 1.0) * li) / den
    btr, bti = btr_ref[...], bti_ref[...]
    bbr = fr * btr - fi * bti
    bbi = fr * bti + fi * btr
    bba = jnp.where(lo, bbr, bbi)
    bbb = jnp.where(lo, -bbi, bbr)
    pa = jnp.where(lo, c2, -s2)
    pb = jnp.where(lo, s2, c2)
    cr, ci = cr_ref[...], ci_ref[...]
    for t in range(q + 1):
        e_scr[t * g:(t + 1) * g, :] = cr * pa[t:t + 1] - ci * pb[t:t + 1]
    wout_ref[...] = e_scr[g:(q + 1) * g, :].astype(BF16)
    kt = _dot_nt(bba, e_scr[0:q * g, :], precision=lax.Precision.HIGHEST)
    for s in range(q):
        blk = kt if s == 0 else jnp.concatenate([jnp.zeros((g, g * s), F32), kt[:, :(q - s) * g]], 1)
        tt_ref[s * g:(s + 1) * g, :] = blk.astype(BF16)
    c2r, s2r = powers((q - 1.0) - tau[0:q])
    for s in range(q):
        win_ref[s * g:(s + 1) * g, :] = (c2r[s:s + 1] * bba + s2r[s:s + 1] * bbb).astype(BF16)
    pw = lax.broadcasted_iota(jnp.int32, (SCAN_PASSES, LANES), 0)
    cq, sq = powers((q * jnp.left_shift(1, pw)).astype(F32))
    aqr_ref[...] = cq
    aqi_ref[...] = jnp.where(lo, -sq, sq)


def ssm_params(lam_re, lam_im, log_step, b_re, b_im, c_re, c_im, q):
    gr, st, g = SSM_GROUPS, SSM_STATE, SSM_GROUP
    dup = lambda a: jnp.concatenate([a, a], -1)
    lr = dup(lam_re).reshape(gr, 1, LANES)
    li = dup(lam_im).reshape(gr, 1, LANES)
    ls = jnp.broadcast_to(log_step[:, None, None], (gr, 1, LANES))
    btr = dup(jnp.swapaxes(b_re, 1, 2))
    bti = dup(jnp.swapaxes(b_im, 1, 2))
    cr, ci = dup(c_re), dup(c_im)
    gs = GROUPS_PER_STEP
    spec = lambda r, c: pl.BlockSpec((gs, r, c), lambda i: (i, 0, 0))
    vec, mat, big, pws = spec(1, LANES), spec(g, LANES), spec(q * g, LANES), spec(SCAN_PASSES, LANES)
    return pl.pallas_call(
        functools.partial(_ssm_param_kernel, q=q),
        grid=(gr // gs,),
        in_specs=[vec, vec, vec, mat, mat, mat, mat],
        out_specs=[big, big, spec(q * g, q * g), pws, pws],
        out_shape=[jax.ShapeDtypeStruct((gr, q * g, LANES), BF16),
                   jax.ShapeDtypeStruct((gr, q * g, LANES), BF16),
                   jax.ShapeDtypeStruct((gr, q * g, q * g), BF16),
                   jax.ShapeDtypeStruct((gr, SCAN_PASSES, LANES), F32),
                   jax.ShapeDtypeStruct((gr, SCAN_PASSES, LANES), F32)],
        scratch_shapes=[pltpu.VMEM(((q + 1) * g, LANES), F32)],
        compiler_params=_cparams("parallel"),
        name="ssm_params_q%d" % q,
    )(lr, li, ls, btr, bti, cr, ci)


def _ssm_in_kernel(u_ref, win_ref, s_ref):
    for i in range(GROUPS_PER_STEP):
        s_ref[:, i * LANES:(i + 1) * LANES] = _dot(u_ref[i], win_ref[i])


def _ssm_scan_kernel(s_ref, h0_ref, ar_ref, ai_ref, hp_ref, hf_ref, h_scr):
    @pl.when(pl.program_id(0) == 0)
    def _():
        h_scr[...] = h0_ref[...]

    def body(k, h):
        hp_ref[k] = h.astype(BF16)
        return ar_ref[...] * h + ai_ref[...] * pltpu.roll(h, SSM_STATE, 1) + s_ref[k]

    h = lax.fori_loop(0, s_ref.shape[0], body, h_scr[...])
    h_scr[...] = h
    hf_ref[...] = h


def _ssm_out_kernel(u_ref, tt_ref, hp_ref, wout_ref, y_ref):
    for i in range(GROUPS_PER_STEP):
        y_ref[i] = _dot(u_ref[i], tt_ref[i]) + _dot_nt(hp_ref[:, i * LANES:(i + 1) * LANES], wout_ref[i])


def ssm_mix(h_bf16, h0_re, h0_im, params, batch, seq, q):
    win, wout, tt, aqr, aqi = params
    gr, st, g = SSM_GROUPS, SSM_STATE, SSM_GROUP
    nk = seq // q
    n = nk * batch
    qg = q * g
    u = h_bf16.reshape(batch, nk, q, gr, g).transpose(3, 1, 0, 2, 4).reshape(gr, n, qg)
    gs = GROUPS_PER_STEP
    s = pl.pallas_call(
        _ssm_in_kernel,
        grid=(gr // gs,),
        in_specs=[pl.BlockSpec((gs, n, qg), lambda i: (i, 0, 0)),
                  pl.BlockSpec((gs, qg, LANES), lambda i: (i, 0, 0))],
        out_specs=pl.BlockSpec((n, gs * LANES), lambda i: (0, i)),
        out_shape=jax.ShapeDtypeStruct((n, gr * LANES), F32),
        compiler_params=_cparams("parallel"),
        name="ssm_in_q%d" % q,
    )(u, win)
    rows = batch * gr
    kc = min(nk, 32)
    tile = lambda a: jnp.broadcast_to(a[:, 0].reshape(1, gr, LANES), (batch, gr, LANES)).reshape(rows, LANES)
    h0 = jnp.concatenate([h0_re, h0_im], -1).reshape(rows, LANES)
    full = pl.BlockSpec((rows, LANES), lambda i: (0, 0))
    hp, hf = pl.pallas_call(
        _ssm_scan_kernel,
        grid=(nk // kc,),
        in_specs=[pl.BlockSpec((kc, rows, LANES), lambda i: (i, 0, 0)), full, full, full],
        out_specs=[pl.BlockSpec((kc, rows, LANES), lambda i: (i, 0, 0)), full],
        out_shape=[jax.ShapeDtypeStruct((nk, rows, LANES), BF16),
                   jax.ShapeDtypeStruct((rows, LANES), F32)],
        scratch_shapes=[pltpu.VMEM((rows, LANES), F32)],
        compiler_params=_cparams("arbitrary"),
        name="ssm_scan_q%d" % q,
    )(s.reshape(nk, rows, LANES), h0, tile(aqr), tile(aqi))
    y = pl.pallas_call(
        _ssm_out_kernel,
        grid=(gr // gs,),
        in_specs=[pl.BlockSpec((gs, n, qg), lambda i: (i, 0, 0)),
                  pl.BlockSpec((gs, qg, qg), lambda i: (i, 0, 0)),
                  pl.BlockSpec((n, gs * LANES), lambda i: (0, i)),
                  pl.BlockSpec((gs, qg, LANES), lambda i: (i, 0, 0))],
        out_specs=pl.BlockSpec((gs, n, qg), lambda i: (i, 0, 0)),
        out_shape=jax.ShapeDtypeStruct((gr, n, qg), F32),
        compiler_params=_cparams("parallel"),
        name="ssm_out_q%d" % q,
    )(u, tt, hp.reshape(n, gr * LANES), wout)
    y = y.reshape(gr, nk, batch, q, g).transpose(2, 1, 3, 0, 4).reshape(batch * seq, D_MODEL)
    hf = hf.reshape(batch, gr, LANES)
    return y, hf[..., :st], hf[..., st:]


SSM_TOKENS_BLK = 2048


GROUPS_PER_TILE = LANES // SSM_GROUP
N_LANE_TILES = D_MODEL // LANES


def _to_chunks_kernel(*refs, q):
    x_refs, (g_ref, u_ref, r_scr, ut_scr) = refs[:N_LANE_TILES], refs[N_LANE_TILES:]
    tb = r_scr.shape[0]
    nkb = tb // q
    ss = jnp.zeros((tb, 1), F32)
    for x_ref in x_refs:
        x = x_ref[...]
        ss = ss + jnp.sum(x * x, -1, keepdims=True)
    r_scr[...] = jnp.broadcast_to(lax.rsqrt(ss * (1.0 / D_MODEL) + RMS_EPS), (tb, LANES))
    for c, x_ref in enumerate(x_refs):
        gain = g_ref[:, c * LANES:(c + 1) * LANES]
        for s in range(q):
            rows = pl.ds(s, nkb, stride=q)
            m = x_ref[rows, :] * r_scr[rows, :] * gain
            ut_scr[:, s * SSM_GROUP:(s + 1) * SSM_GROUP, :] = m.T.reshape(GROUPS_PER_TILE, SSM_GROUP, nkb)
        for grp in range(GROUPS_PER_TILE):
            u_ref[c * GROUPS_PER_TILE + grp] = ut_scr[grp].T.astype(BF16)


def _from_chunks_kernel(y_ref, o_ref, yt_scr, o_scr, *, q):
    nkb = o_ref.shape[0] // q
    for c in range(N_LANE_TILES):
        for grp in range(GROUPS_PER_TILE):
            yt_scr[grp] = y_ref[c * GROUPS_PER_TILE + grp].T
        for t in range(q):
            o_scr[pl.ds(t, nkb, stride=q), :] = yt_scr[:, t * SSM_GROUP:(t + 1) * SSM_GROUP, :].reshape(LANES, nkb).T
        o_ref[:, c * LANES:(c + 1) * LANES] = o_scr[...]


def _ssm_group_kernel(u_ref, win_ref, wout_ref, tt_ref, ar_ref, ai_ref, y_ref, hf_ref, h_scr, *, nk):
    u = u_ref[...]
    h = _dot(u, win_ref[...])
    n = h.shape[0]
    k_idx = lax.broadcasted_iota(jnp.int32, (n, LANES), 0) % nk
    d = 1
    for j in range(SCAN_PASSES):
        if d >= nk:
            break
        hs = jnp.where(k_idx >= d, pltpu.roll(h, d, 0), 0.0)
        h = h + ar_ref[j:j + 1, :] * hs + ai_ref[j:j + 1, :] * pltpu.roll(hs, SSM_STATE, 1)
        d *= 2
    hp = jnp.where(k_idx >= 1, pltpu.roll(h, 1, 0), 0.0)
    y_ref[...] = _dot(u, tt_ref[...]) + _dot_nt(hp.astype(BF16), wout_ref[...])
    h_scr[...] = h
    hf_ref[...] = h_scr[pl.ds(nk - 1, n // nk, stride=nk), :]


def ssm_mix_prompt(x, g_mix, params, batch, seq, q):
    win, wout, tt, aqr, aqi = params
    gr, st = SSM_GROUPS, SSM_STATE
    nk = seq // q
    assert nk <= 2 ** SCAN_PASSES
    n = nk * batch
    qg = q * SSM_GROUP
    tb = SSM_TOKENS_BLK
    nkb = tb // q
    per_b = seq // tb
    tok = pl.BlockSpec((tb, D_MODEL), lambda b, i: (b * per_b + i, 0))
    chk = pl.BlockSpec((gr, nkb, qg), lambda b, i: (0, b * per_b + i, 0))
    lane_tiles = [pl.BlockSpec((tb, LANES), lambda b, i, c=c: (b * per_b + i, c)) for c in range(N_LANE_TILES)]
    u = pl.pallas_call(
        functools.partial(_to_chunks_kernel, q=q),
        grid=(batch, per_b),
        in_specs=lane_tiles + [pl.BlockSpec((1, D_MODEL), lambda b, i: (0, 0))],
        out_specs=chk,
        out_shape=jax.ShapeDtypeStruct((gr, n, qg), BF16),
        scratch_shapes=[pltpu.VMEM((tb, LANES), F32), pltpu.VMEM((GROUPS_PER_TILE, qg, nkb), F32)],
        compiler_params=_cparams("parallel", "parallel"),
        name="ssm_to_chunks",
    )(*([x] * N_LANE_TILES), g_mix)
    per_g = lambda r, c: pl.BlockSpec((None, r, c), lambda i: (i, 0, 0))
    y, hf = pl.pallas_call(
        functools.partial(_ssm_group_kernel, nk=nk),
        grid=(gr,),
        in_specs=[per_g(n, qg), per_g(qg, LANES), per_g(qg, LANES), per_g(qg, qg),
                  per_g(SCAN_PASSES, LANES), per_g(SCAN_PASSES, LANES)],
        out_specs=[per_g(n, qg), per_g(batch, LANES)],
        out_shape=[jax.ShapeDtypeStruct((gr, n, qg), F32), jax.ShapeDtypeStruct((gr, batch, LANES), F32)],
        scratch_shapes=[pltpu.VMEM((n, LANES), F32)],
        compiler_params=_cparams("parallel"),
        name="ssm_group",
    )(u, win, wout, tt, aqr, aqi)
    y = pl.pallas_call(
        functools.partial(_from_chunks_kernel, q=q),
        grid=(batch, per_b),
        in_specs=[chk],
        out_specs=tok,
        out_shape=jax.ShapeDtypeStruct((batch * seq, D_MODEL), F32),
        scratch_shapes=[pltpu.VMEM((GROUPS_PER_TILE, qg, nkb), F32), pltpu.VMEM((tb, LANES), F32)],
        compiler_params=_cparams("parallel", "parallel"),
        name="ssm_from_chunks",
    )(y)
    hf = jnp.swapaxes(hf, 0, 1)
    return y, hf[..., :st], hf[..., st:]


def _ssm_glu_kernel(x_ref, gm_ref, y_ref, d_ref, w_ref, b_ref, o_ref):
    x = x_ref[...]
    y = y_ref[...] + d_ref[...] * _rms(x, gm_ref[...])
    gl = _gelu_tanh(y)
    z = _dot(gl.astype(BF16), w_ref[...]) + b_ref[...]
    o_ref[...] = x + gl * _sigmoid(z)


def ssm_glu(x, g_mix, y, d_skip, w_glu, layer, b_glu, tm):
    t = x.shape[0]
    row = pl.BlockSpec((tm, D_MODEL), lambda i: (i, 0))
    vec = pl.BlockSpec((1, D_MODEL), lambda i: (0, 0))
    return pl.pallas_call(
        _ssm_glu_kernel,
        grid=(t // tm,),
        in_specs=[row, vec, row, vec, pl.BlockSpec((None, D_MODEL, D_MODEL), lambda i: (layer, 0, 0)), vec],
        out_specs=row,
        out_shape=jax.ShapeDtypeStruct((t, D_MODEL), F32),
        compiler_params=_cparams("parallel"),
        name="ssm_glu",
    )(x, g_mix, y, d_skip, w_glu, b_glu)


def _row(v):
    return v.reshape(1, -1).astype(F32)


def kernel(x_prompt, x_sample, mem_prompt, state_ssm_re, state_ssm_im, cache_swa0_k, cache_swa0_v, cache_swa1_k, cache_swa1_v, cache_swa2_k, cache_swa2_v, cache_mem_k, cache_mem_v, norm_mix_g, norm_mem_g, norm_memin_g, norm_ffn_g, ssm_lambda_re, ssm_lambda_im, ssm_b_re, ssm_b_im, ssm_c_re, ssm_c_im, ssm_d, ssm_log_step, ssm_w_glu, ssm_b_glu, attn_w_qkv, attn_q_norm_g, attn_k_norm_g, attn_w_o, mem_w_q, mem_w_kv, mem_q_norm_g, mem_k_norm_g, mem_w_o, ffn_w_in, ffn_w_out):
    pb, seq, _ = x_prompt.shape
    sb, dec, _ = x_sample.shape
    n_ssm = state_ssm_re.shape[0]
    n_attn = cache_swa0_k.shape[0]
    tm_p = 512
    tm_s = sb * SAMPLE_ROWS

    w_glu = ssm_w_glu.astype(BF16)
    w_qkv = attn_w_qkv.astype(BF16)
    w_ao = attn_w_o.astype(BF16)
    w_mq = mem_w_q.astype(BF16)
    w_mkv = mem_w_kv.astype(BF16)
    w_mo = mem_w_o.astype(BF16)
    w_fi = ffn_w_in.astype(BF16)
    w_fo = ffn_w_out.astype(BF16)

    head_gain = [jnp.concatenate([jnp.tile(attn_q_norm_g[j], (1, SWA_HEADS)),
                                  jnp.tile(attn_k_norm_g[j], (1, SWA_HEADS))], 1).reshape(N_SWA, 1, 2 * SWA_WIDTH)
                 for j in range(n_attn)]
    mem_q_gain = [_row(jnp.tile(mem_q_norm_g[i], MEM_HEADS)) for i in range(DEPTH)]
    mem_k_gain = jnp.tile(mem_k_norm_g, (1, MEM_HEADS)).reshape(DEPTH, 1, D_MODEL)

    mkv = mem_kv(mem_prompt.reshape(pb * N_MEM, D_MODEL), norm_memin_g.reshape(DEPTH, 1, D_MODEL),
                 w_mkv, mem_k_gain)
    mkv5 = mkv.reshape(DEPTH, pb, N_MEM, 2, MEM_HEADS, MEM_HEAD_DIM)
    p_mem_k, p_mem_v = mkv5[:, :, :, 0], mkv5[:, :, :, 1]
    mkv_b = mkv.reshape(DEPTH, pb * N_MEM, 2 * D_MODEL)

    ssm_p = [[ssm_params(ssm_lambda_re[j], ssm_lambda_im[j], ssm_log_step[j], ssm_b_re[j], ssm_b_im[j],
                         ssm_c_re[j], ssm_c_im[j], q) for q in (SSM_Q_PROMPT, dec)] for j in range(n_ssm)]

    rows_minor = lambda c: jnp.transpose(c, (0, 1, 3, 4, 2))
    caches_k = tuple(rows_minor(c) for c in (cache_swa0_k, cache_swa1_k, cache_swa2_k))
    caches_v = tuple(rows_minor(c) for c in (cache_swa0_v, cache_swa1_v, cache_swa2_v))
    dils = tuple(d for _, d in SWA_GROUPS)
    assert all(min(w, seq) == SWA_KEYS_BACK * d for w, d in SWA_GROUPS)

    xp = x_prompt.reshape(pb * seq, D_MODEL)
    xs = jnp.pad(x_sample, ((0, 0), (0, SAMPLE_ROWS - dec), (0, 0))).reshape(tm_s, D_MODEL)

    p_ssm_re, p_ssm_im, s_ssm_re, s_ssm_im = [], [], [], []
    p_qkv = [[] for _ in SWA_GROUPS]
    s_swa_k = [[] for _ in SWA_GROUPS]
    s_swa_v = [[] for _ in SWA_GROUPS]

    for i in range(DEPTH):
        j = i // 2
        g_mix = _row(norm_mix_g[i])
        if i % 2 == 0:
            d_skip, b_glu = _row(ssm_d[j]), _row(ssm_b_glu[j])
            y, fr, fi = ssm_mix_prompt(xp, g_mix, ssm_p[j][0], pb, seq, SSM_Q_PROMPT)
            p_ssm_re.append(fr)
            p_ssm_im.append(fi)
            xp = ssm_glu(xp, g_mix, y, d_skip, w_glu, j, b_glu, tm_p)
            hb = norm_cast(xs, g_mix, tm_s).reshape(sb, SAMPLE_ROWS, D_MODEL)[:, :dec].reshape(sb * dec, D_MODEL)
            y, fr, fi = ssm_mix(hb, state_ssm_re[j], state_ssm_im[j], ssm_p[j][1], sb, dec, dec)
            s_ssm_re.append(fr)
            s_ssm_im.append(fi)
            y = jnp.pad(y.reshape(sb, dec, D_MODEL), ((0, 0), (0, SAMPLE_ROWS - dec), (0, 0))).reshape(tm_s, D_MODEL)
            xs = ssm_glu(xs, g_mix, y, d_skip, w_glu, j, b_glu, tm_s)
        else:
            qkvs = qkv_proj(xp, g_mix, w_qkv, j, head_gain[j], pb, dils, tm_p)
            outs, lses = [], []
            for g in range(N_SWA):
                p_qkv[g].append(qkvs[g])
                o, lse = swa_prompt(qkvs[g], g)
                outs.append(o)
                lses.append(lse)
            xp = attn_out(xp, outs, lses, w_ao, j, pb, tm_p)
            qkvs = qkv_proj(xs, g_mix, w_qkv, j, head_gain[j], 1, (1,) * N_SWA, tm_s)
            outs, lses = [], []
            for g in range(N_SWA):
                qkv3 = qkvs[g].reshape(sb, SAMPLE_ROWS, 3 * SWA_WIDTH)
                qkv5 = qkv3.reshape(sb, SAMPLE_ROWS, 3, SWA_HEADS, SWA_HEAD_DIM)
                s_swa_k[g].append(qkv5[:, :dec, 1])
                s_swa_v[g].append(qkv5[:, :dec, 2])
                o, lse = swa_sample(qkv3, caches_k[g], caches_v[g], j, g, dec)
                outs.append(o.reshape(1, 1, tm_s, SWA_WIDTH))
                lses.append(lse.reshape(1, 1, tm_s, LANES))
            xs = attn_out(xs, outs, lses, w_ao, j, 1, tm_s)

        g_mem = _row(norm_mem_g[i])
        xp = mem_attn_prompt(xp, g_mem, w_mq, mem_q_gain[i], mkv_b, i, w_mo, pb, tm_p)
        xs = mem_attn_sample(xs, g_mem, w_mq, mem_q_gain[i], cache_mem_k, cache_mem_v, i, w_mo, sb)
        g_ffn = _row(norm_ffn_g[i])
        xp = swiglu_block(xp, g_ffn, w_fi, w_fo, i, 1024)
        xs = swiglu_block(xs, g_ffn, w_fi, w_fo, i, tm_s)

    rows_major = lambda c: jnp.transpose(c, (0, 1, 4, 2, 3))
    p_swa = [[rows_major(c) for c in swa_last_rows(p_qkv[g])] for g in range(N_SWA)]
    y_prompt = xp.reshape(pb, seq, D_MODEL)
    y_sample = xs.reshape(sb, SAMPLE_ROWS, D_MODEL)[:, :dec]
    st = lambda a: jnp.stack(a, 0)
    return (y_prompt, y_sample,
            st(p_ssm_re), st(p_ssm_im),
            p_swa[0][0], p_swa[0][1], p_swa[1][0], p_swa[1][1], p_swa[2][0], p_swa[2][1],
            p_mem_k, p_mem_v,
            st(s_ssm_re), st(s_ssm_im),
            st(s_swa_k[0]), st(s_swa_v[0]), st(s_swa_k[1]), st(s_swa_v[1]), st(s_swa_k[2]), st(s_swa_v[2]))
```

```python
import functools
import math

import jax
import jax.numpy as jnp
from jax import lax
from jax.experimental import pallas as pl
from jax.experimental.pallas import tpu as pltpu

F32 = jnp.float32
BF16 = jnp.bfloat16

D_MODEL = 1024
DEPTH = 4
SSM_GROUP = 16
SSM_GROUPS = D_MODEL // SSM_GROUP
SSM_STATE = 64
SWA_GROUPS = ((128, 1), (512, 4), (2048, 16))
N_SWA = len(SWA_GROUPS)
SWA_HEADS = 8
SWA_HEAD_DIM = 64
SWA_WIDTH = SWA_HEADS * SWA_HEAD_DIM
QKV_WIDTH = 3 * N_SWA * SWA_WIDTH
SWA_KEYS_BACK = 128
N_MEM = 256
MEM_HEADS = 4
MEM_HEAD_DIM = D_MODEL // MEM_HEADS
D_FF = -(-8 * D_MODEL // (3 * 256)) * 256
RMS_EPS = 1e-6
NEG_BIG = -1e30

LANES = 128
SUBLANES = 8
VMEM_LIMIT = 48 * 1024 * 1024
Q_BLK = 128
FF_BLK = 256
SAMPLE_ROWS = SUBLANES
SSM_Q_PROMPT = 16


def _cparams(*sem):
    return pltpu.CompilerParams(dimension_semantics=sem, vmem_limit_bytes=VMEM_LIMIT)


def _dot(a, b):
    return jnp.dot(a, b, preferred_element_type=F32)


def _dot_nt(a, b, precision=None):
    return lax.dot_general(a, b, (((1,), (1,)), ((), ())), preferred_element_type=F32,
                           precision=precision)


def _rms(x, g):
    return x * lax.rsqrt(jnp.mean(x * x, -1, keepdims=True) + RMS_EPS) * g


def _headnorm(y, g_row, hd):
    tm, n = y.shape
    outs = []
    if hd % LANES == 0:
        for c in range(n // hd):
            yc = y[:, c * hd:(c + 1) * hd]
            ms = jnp.mean(yc * yc, -1, keepdims=True)
            outs.append(yc * lax.rsqrt(ms + RMS_EPS))
    else:
        assert 2 * hd == LANES
        lo = lax.broadcasted_iota(jnp.int32, (tm, LANES), 1) < hd
        for c in range(n // LANES):
            yc = y[:, c * LANES:(c + 1) * LANES]
            sq = yc * yc
            s_lo = jnp.sum(jnp.where(lo, sq, 0.0), -1, keepdims=True)
            s_hi = jnp.sum(jnp.where(lo, 0.0, sq), -1, keepdims=True)
            ms = jnp.where(lo, s_lo, s_hi) * (1.0 / hd)
            outs.append(yc * lax.rsqrt(ms + RMS_EPS))
    return jnp.concatenate(outs, -1) * g_row


def _sigmoid(x):
    return 1.0 / (1.0 + jnp.exp(-x))


def _gelu_tanh(x):
    return 0.5 * x * (1.0 + jnp.tanh(math.sqrt(2.0 / math.pi) * (x + 0.044715 * (x * x * x))))


def _norm_cast_kernel(x_ref, g_ref, o_ref):
    o_ref[...] = _rms(x_ref[...], g_ref[...]).astype(BF16)


def norm_cast(x, g, tm):
    t = x.shape[0]
    return pl.pallas_call(
        _norm_cast_kernel,
        grid=(t // tm,),
        in_specs=[pl.BlockSpec((tm, D_MODEL), lambda i: (i, 0)),
                  pl.BlockSpec((1, D_MODEL), lambda i: (0, 0))],
        out_specs=pl.BlockSpec((tm, D_MODEL), lambda i: (i, 0)),
        out_shape=jax.ShapeDtypeStruct((t, D_MODEL), BF16),
        compiler_params=_cparams("parallel"),
        name="norm_cast",
    )(x, g)


def _qkv_kernel(x_ref, g_ref, w_ref, hg_ref, o0_ref, o1_ref, o2_ref, y_ref, *, dils):
    tm = x_ref.shape[0]
    w = SWA_WIDTH
    xn = _rms(x_ref[...], g_ref[...]).astype(BF16)
    for grp, (o_ref, dil) in enumerate(zip((o0_ref, o1_ref, o2_ref), dils)):
        col = lambda kind: slice((kind * N_SWA + grp) * w, (kind * N_SWA + grp + 1) * w)
        parts = (_headnorm(_dot(xn, w_ref[:, col(0)]), hg_ref[grp, :, :w], SWA_HEAD_DIM),
                 _headnorm(_dot(xn, w_ref[:, col(1)]), hg_ref[grp, :, w:], SWA_HEAD_DIM),
                 _dot(xn, w_ref[:, col(2)]))
        if dil == 1:
            for kind, y in enumerate(parts):
                o_ref[0, :, kind * w:(kind + 1) * w] = y
        else:
            per = w // LANES
            for c in range(y_ref.shape[0]):
                y_ref[c] = parts[c // per][:, (c % per) * LANES:(c % per + 1) * LANES]
                for r in range(dil):
                    o_ref[r, :, c * LANES:(c + 1) * LANES] = y_ref[c, pl.ds(r, tm // dil, stride=dil), :]


def qkv_proj(x, g, w, layer, head_gain, batch, dils, tm):
    t = x.shape[0]
    seq = t // batch
    per_b = seq // tm
    gw = 3 * SWA_WIDTH
    return pl.pallas_call(
        functools.partial(_qkv_kernel, dils=dils),
        grid=(t // tm,),
        in_specs=[pl.BlockSpec((tm, D_MODEL), lambda i: (i, 0)),
                  pl.BlockSpec((1, D_MODEL), lambda i: (0, 0)),
                  pl.BlockSpec((None, D_MODEL, QKV_WIDTH), lambda i: (layer, 0, 0), pipeline_mode=pl.Buffered(1)),
                  pl.BlockSpec((N_SWA, 1, 2 * SWA_WIDTH), lambda i: (0, 0, 0))],
        out_specs=[pl.BlockSpec((None, d, tm // d, gw), lambda i: (i // per_b, 0, i % per_b, 0))
                   for d in dils],
        out_shape=[jax.ShapeDtypeStruct((batch, d, seq // d, gw), F32) for d in dils],
        scratch_shapes=[pltpu.VMEM((gw // LANES, tm, LANES), F32)],
        compiler_params=_cparams("parallel"),
        name="qkv_proj",
    )(x, g, w, head_gain)


def _mem_kv_kernel(x_ref, g_ref, w_ref, kg_ref, o_ref):
    j = pl.program_id(1)
    y = _dot(_rms(x_ref[...], g_ref[...]).astype(BF16), w_ref[...])

    @pl.when(j == 0)
    def _():
        o_ref[...] = _headnorm(y, kg_ref[...], MEM_HEAD_DIM)

    @pl.when(j == 1)
    def _():
        o_ref[...] = y


def mem_kv(mem, g_in, w_kv, k_gain):
    t = mem.shape[0]
    return pl.pallas_call(
        _mem_kv_kernel,
        grid=(DEPTH, 2),
        in_specs=[pl.BlockSpec((t, D_MODEL), lambda l, j: (0, 0)),
                  pl.BlockSpec((None, 1, D_MODEL), lambda l, j: (l, 0, 0)),
                  pl.BlockSpec((None, D_MODEL, D_MODEL), lambda l, j: (l, 0, j)),
                  pl.BlockSpec((None, 1, D_MODEL), lambda l, j: (l, 0, 0))],
        out_specs=pl.BlockSpec((None, t, D_MODEL), lambda l, j: (l, 0, j)),
        out_shape=jax.ShapeDtypeStruct((DEPTH, t, 2 * D_MODEL), F32),
        compiler_params=_cparams("parallel", "arbitrary"),
        name="mem_kv",
    )(mem, g_in, w_kv, k_gain)


def _swiglu_kernel(x_ref, g_ref, wi_ref, wo_ref, o_ref):
    x = x_ref[...]
    xn = _rms(x, g_ref[...]).astype(BF16)
    acc = x
    for c in range(D_FF // FF_BLK):
        lo = c * FF_BLK
        gate = _dot(xn, wi_ref[:, lo:lo + FF_BLK])
        up = _dot(xn, wi_ref[:, D_FF + lo:D_FF + lo + FF_BLK])
        act = gate * _sigmoid(gate) * up
        acc = acc + _dot(act.astype(BF16), wo_ref[lo:lo + FF_BLK, :])
    o_ref[...] = acc


def swiglu_block(x, g, w_in, w_out, layer, tm):
    t = x.shape[0]
    once = pl.Buffered(1)
    return pl.pallas_call(
        _swiglu_kernel,
        grid=(t // tm,),
        in_specs=[pl.BlockSpec((tm, D_MODEL), lambda i: (i, 0)),
                  pl.BlockSpec((1, D_MODEL), lambda i: (0, 0)),
                  pl.BlockSpec((None, D_MODEL, 2 * D_FF), lambda i: (layer, 0, 0), pipeline_mode=once),
                  pl.BlockSpec((None, D_FF, D_MODEL), lambda i: (layer, 0, 0), pipeline_mode=once)],
        out_specs=pl.BlockSpec((tm, D_MODEL), lambda i: (i, 0)),
        out_shape=jax.ShapeDtypeStruct((t, D_MODEL), F32),
        compiler_params=_cparams("parallel"),
        name="swiglu",
    )(x, g, w_in, w_out)


def _mem_heads(q, key_head, value_head):
    outs = []
    for h in range(MEM_HEADS):
        sl = slice(h * MEM_HEAD_DIM, (h + 1) * MEM_HEAD_DIM)
        s = _dot_nt(q[:, sl].astype(BF16), key_head(h))
        m = jnp.max(s, -1, keepdims=True)
        p = jnp.exp(s - m)
        l = jnp.sum(p, -1, keepdims=True)
        outs.append(_dot(p.astype(BF16), value_head(h)) / l)
    return jnp.concatenate(outs, -1)


def _mem_attn_kernel(x_ref, g_ref, wq_ref, qg_ref, mk_ref, mv_ref, wo_ref, o_ref):
    x = x_ref[...]
    q = _dot(_rms(x, g_ref[...]).astype(BF16), wq_ref[...])
    q = _headnorm(q, qg_ref[...], MEM_HEAD_DIM) * (MEM_HEAD_DIM ** -0.5)
    head = lambda ref: lambda h: ref[:, h * MEM_HEAD_DIM:(h + 1) * MEM_HEAD_DIM].astype(BF16)
    o = _mem_heads(q, head(mk_ref), head(mv_ref))
    o_ref[...] = x + _dot(o.astype(BF16), wo_ref[...])


def mem_attn_prompt(x, g, w_q, q_gain, mkv, layer, w_o, batch, tm):
    t = x.shape[0]
    per_b = t // batch // tm
    wspec = pl.BlockSpec((None, D_MODEL, D_MODEL), lambda b, i: (layer, 0, 0), pipeline_mode=pl.Buffered(1))
    return pl.pallas_call(
        _mem_attn_kernel,
        grid=(batch, per_b),
        in_specs=[pl.BlockSpec((tm, D_MODEL), lambda b, i: (b * per_b + i, 0)),
                  pl.BlockSpec((1, D_MODEL), lambda b, i: (0, 0)),
                  wspec,
                  pl.BlockSpec((1, D_MODEL), lambda b, i: (0, 0)),
                  pl.BlockSpec((None, N_MEM, D_MODEL), lambda b, i: (layer, b, 0)),
                  pl.BlockSpec((None, N_MEM, D_MODEL), lambda b, i: (layer, b, 1)),
                  wspec],
        out_specs=pl.BlockSpec((tm, D_MODEL), lambda b, i: (b * per_b + i, 0)),
        out_shape=jax.ShapeDtypeStruct((t, D_MODEL), F32),
        compiler_params=_cparams("parallel", "arbitrary"),
        name="mem_attn_prompt",
    )(x, g, w_q, q_gain, mkv, mkv, w_o)


def _mem_attn_sample_kernel(x_ref, g_ref, wq_ref, qg_ref, mk_ref, mv_ref, wo_ref, o_ref,
                            q_scr, o_scr):
    b = pl.program_id(0)

    @pl.when(b == 0)
    def _():
        q = _dot(_rms(x_ref[...], g_ref[...]).astype(BF16), wq_ref[...])
        q_scr[...] = _headnorm(q, qg_ref[...], MEM_HEAD_DIM) * (MEM_HEAD_DIM ** -0.5)

    hd, nr, nk = MEM_HEAD_DIM, MEM_HEADS * SAMPLE_ROWS, N_MEM * MEM_HEADS
    own = (lax.broadcasted_iota(jnp.int32, (nr, nk), 1) % MEM_HEADS
           == lax.broadcasted_iota(jnp.int32, (nr, nk), 0) // SAMPLE_ROWS)
    for bi in range(mk_ref.shape[0]):
        rows = pl.ds(pl.multiple_of((b * mk_ref.shape[0] + bi) * SAMPLE_ROWS, SAMPLE_ROWS), SAMPLE_ROWS)
        q = q_scr[rows, :]
        qx = jnp.concatenate([q[:, h * hd:(h + 1) * hd] for h in range(MEM_HEADS)], 0).astype(BF16)
        s = jnp.where(own, _dot_nt(qx, mk_ref[bi].reshape(nk, hd).astype(BF16)), NEG_BIG)
        p = jnp.exp(s - jnp.max(s, -1, keepdims=True))
        o = _dot(p.astype(BF16), mv_ref[bi].reshape(nk, hd).astype(BF16)) / jnp.sum(p, -1, keepdims=True)
        o_scr[rows, :] = jnp.concatenate([o[h * SAMPLE_ROWS:(h + 1) * SAMPLE_ROWS] for h in range(MEM_HEADS)], -1)

    @pl.when(b == pl.num_programs(0) - 1)
    def _():
        o_ref[...] = x_ref[...] + _dot(o_scr[...].astype(BF16), wo_ref[...])


def mem_attn_sample(x, g, w_q, q_gain, cache_k, cache_v, layer, w_o, batch):
    t = x.shape[0]
    const = lambda b: (0, 0)
    wspec = pl.BlockSpec((None, D_MODEL, D_MODEL), lambda b: (layer, 0, 0))
    per_step = 4
    cspec = pl.BlockSpec((None, per_step, N_MEM, MEM_HEADS, MEM_HEAD_DIM), lambda b: (layer, b, 0, 0, 0))
    return pl.pallas_call(
        _mem_attn_sample_kernel,
        grid=(batch // per_step,),
        in_specs=[pl.BlockSpec((t, D_MODEL), const),
                  pl.BlockSpec((1, D_MODEL), const),
                  wspec,
                  pl.BlockSpec((1, D_MODEL), const),
                  cspec, cspec, wspec],
        out_specs=pl.BlockSpec((t, D_MODEL), const),
        out_shape=jax.ShapeDtypeStruct((t, D_MODEL), F32),
        scratch_shapes=[pltpu.VMEM((t, D_MODEL), F32), pltpu.VMEM((t, D_MODEL), F32)],
        compiler_params=_cparams("arbitrary"),
        name="mem_attn_sample",
    )(x, g, w_q, q_gain, cache_k, cache_v, w_o)


MAX_Q_SUB = 4


def _swa_prompt_kernel(q_ref, kp_ref, kc_ref, vp_ref, vc_ref, o_ref, l_ref):
    for res in range(q_ref.shape[0]):
        _swa_prompt_rows(*[r.at[res] for r in (q_ref, kp_ref, kc_ref, vp_ref, vc_ref, o_ref, l_ref)])


def _swa_prompt_rows(q_ref, kp_ref, kc_ref, vp_ref, vc_ref, o_ref, l_ref):
    i = pl.program_id(2)
    q = (q_ref[...] * (SWA_HEAD_DIM ** -0.5)).astype(BF16)
    k = jnp.concatenate([kp_ref[...], kc_ref[...]], 0).astype(BF16)
    v = jnp.concatenate([vp_ref[...], vc_ref[...]], 0).astype(BF16)
    qq = lax.broadcasted_iota(jnp.int32, (Q_BLK, 2 * Q_BLK), 0)
    kk = lax.broadcasted_iota(jnp.int32, (Q_BLK, 2 * Q_BLK), 1)
    band = (kk >= qq) & (kk <= qq + SWA_KEYS_BACK)
    head_lane = lax.broadcasted_iota(jnp.int32, (Q_BLK, LANES), 1)
    head_of_lane = head_lane // SWA_HEAD_DIM
    pair_ids = range(LANES // SWA_HEAD_DIM)
    for sub in range(q_ref.shape[0] // Q_BLK):
        rows = slice(sub * Q_BLK, (sub + 1) * Q_BLK)
        win = slice(sub * Q_BLK, (sub + 2) * Q_BLK)
        valid = band if sub else band & ((kk >= Q_BLK) | (i > 0))
        outs = []
        lse = jnp.zeros((Q_BLK, LANES), F32)
        for c in range(SWA_WIDTH // LANES):
            tile = slice(c * LANES, (c + 1) * LANES)
            ql, kl, vl = q[rows, tile], k[win, tile], v[win, tile]
            pair = []
            for e in pair_ids:
                qh = jnp.where(head_of_lane == e, ql, jnp.zeros_like(ql))
                s = jnp.where(valid, _dot_nt(qh, kl), NEG_BIG)
                m = jnp.max(s, -1, keepdims=True)
                p = jnp.exp(s - m)
                l = jnp.sum(p, -1, keepdims=True)
                pair.append(_dot(p.astype(BF16), vl) / l)
                lse = jnp.where(head_lane == c * len(pair_ids) + e, m + jnp.log(l), lse)
            outs.append(jnp.where(head_of_lane == 0, pair[0], pair[1]))
        o_ref[rows, :] = jnp.concatenate(outs, -1)
        l_ref[rows, :] = lse


def swa_prompt(qkv, group):
    batch, dil, rows, _ = qkv.shape
    q_sub = min(MAX_Q_SUB, rows // Q_BLK)
    n_res = min(dil, MAX_Q_SUB // q_sub)
    blk = (None, n_res, q_sub * Q_BLK, SWA_WIDTH)
    pblk = (None, n_res, Q_BLK, SWA_WIDTH)
    prev = lambda i: jnp.maximum(i * q_sub - 1, 0)
    q_spec = pl.BlockSpec(blk, lambda b, r, i: (b, r, i, 0))
    kp_spec = pl.BlockSpec(pblk, lambda b, r, i: (b, r, prev(i), 1))
    kc_spec = pl.BlockSpec(blk, lambda b, r, i: (b, r, i, 1))
    vp_spec = pl.BlockSpec(pblk, lambda b, r, i: (b, r, prev(i), 2))
    vc_spec = pl.BlockSpec(blk, lambda b, r, i: (b, r, i, 2))
    l_spec = pl.BlockSpec((None, n_res, q_sub * Q_BLK, LANES), lambda b, r, i: (b, r, i, 0))
    return pl.pallas_call(
        _swa_prompt_kernel,
        grid=(batch, dil // n_res, rows // (q_sub * Q_BLK)),
        in_specs=[q_spec, kp_spec, kc_spec, vp_spec, vc_spec],
        out_specs=[q_spec, l_spec],
        out_shape=[jax.ShapeDtypeStruct((batch, dil, rows, SWA_WIDTH), F32),
                   jax.ShapeDtypeStruct((batch, dil, rows, LANES), F32)],
        compiler_params=_cparams("parallel", "parallel", "arbitrary"),
        name="swa_prompt_g%d" % group,
    )(qkv, qkv, qkv, qkv, qkv)


def _swa_rows_kernel(*refs, dil, n_layers):
    srcs, (ko_ref, vo_ref, buf) = refs[:2 * n_layers], refs[2 * n_layers:]
    layer = pl.program_id(0)
    pair = LANES // SWA_HEAD_DIM
    for li in range(n_layers):
        @pl.when(layer == li)
        def _(li=li):
            for src, dst in ((srcs[2 * li], ko_ref), (srcs[2 * li + 1], vo_ref)):
                for c in range(SWA_WIDTH // LANES):
                    for r in range(dil):
                        buf[pl.ds(r, src.shape[1], stride=dil), :] = src[r, :, c * LANES:(c + 1) * LANES]
                    dst[c * pair:(c + 1) * pair] = buf[...].T.reshape(pair, SWA_HEAD_DIM, buf.shape[0])


def swa_last_rows(qkvs):
    n_layers = len(qkvs)
    batch, dil, rows, _ = qkvs[0].shape
    keep = SWA_KEYS_BACK * dil
    pos = min(SWA_KEYS_BACK, 1024 // dil)
    nb = SWA_KEYS_BACK // pos
    first = (rows - SWA_KEYS_BACK) // pos
    in_specs, args = [], []
    for qkv in qkvs:
        for kind in (1, 2):
            in_specs.append(pl.BlockSpec((None, dil, pos, SWA_WIDTH),
                                         lambda l, b, a, kind=kind: (b, 0, first + a, kind)))
            args.append(qkv)
    o_spec = pl.BlockSpec((None, None, SWA_HEADS, SWA_HEAD_DIM, pos * dil), lambda l, b, a: (l, b, 0, 0, a))
    shape = jax.ShapeDtypeStruct((n_layers, batch, SWA_HEADS, SWA_HEAD_DIM, keep), F32)
    return pl.pallas_call(
        functools.partial(_swa_rows_kernel, dil=dil, n_layers=n_layers),
        grid=(n_layers, batch, nb),
        in_specs=in_specs,
        out_specs=[o_spec, o_spec],
        out_shape=[shape, shape],
        scratch_shapes=[pltpu.VMEM((pos * dil, LANES), F32)],
        compiler_params=_cparams("parallel", "parallel", "parallel"),
        name="swa_last_rows_d%d" % dil,
    )(*args)


def _swa_sample_kernel(q_ref, kn_ref, vn_ref, kt_ref, vt_ref, o_ref, l_ref, bias_c, bias_n, *, dil, n_new):
    nq, nh, hd = SAMPLE_ROWS, SWA_HEADS, SWA_HEAD_DIM
    nc = nh * nq
    rows = kt_ref.shape[-1]

    @pl.when(pl.program_id(0) == 0)
    def _():
        def bias(n, valid):
            i = lax.broadcasted_iota(jnp.int32, (nc, n), 0) % nq
            return jnp.where(valid(i, lax.broadcasted_iota(jnp.int32, (nc, n), 1)), 0.0, NEG_BIG)
        bias_c[...] = bias(rows, lambda i, c: (c >= i) & ((c - i) % dil == 0))
        bias_n[...] = bias(nq, lambda i, j: (j < n_new) & (j <= i) & ((i - j) % dil == 0))

    own = (lax.broadcasted_iota(jnp.int32, (nc, SWA_WIDTH), 1) // hd
           == lax.broadcasted_iota(jnp.int32, (nc, SWA_WIDTH), 0) // nq)
    head_lane = lax.broadcasted_iota(jnp.int32, (nq, LANES), 1)
    for b in range(q_ref.shape[0]):
        q = q_ref[b] * (hd ** -0.5)
        qbd = jnp.where(own, jnp.concatenate([q] * nh, 0), 0.0).astype(BF16)
        s_c = _dot(qbd, kt_ref[b].reshape(SWA_WIDTH, rows).astype(BF16)) + bias_c[...]
        s_n = _dot_nt(qbd, kn_ref[b].astype(BF16)) + bias_n[...]
        m = jnp.maximum(jnp.max(s_c, -1, keepdims=True), jnp.max(s_n, -1, keepdims=True))
        p_c, p_n = jnp.exp(s_c - m), jnp.exp(s_n - m)
        l = jnp.sum(p_c, -1, keepdims=True) + jnp.sum(p_n, -1, keepdims=True)
        acc = (_dot_nt(p_c.astype(BF16), vt_ref[b].reshape(SWA_WIDTH, rows).astype(BF16))
               + _dot(p_n.astype(BF16), vn_ref[b].astype(BF16)))
        o = acc / l
        o_ref[b] = jnp.concatenate([o[h * nq:(h + 1) * nq, h * hd:(h + 1) * hd] for h in range(nh)], -1)
        lse_col = m + jnp.log(l)
        lse = jnp.zeros((nq, LANES), F32)
        for h in range(nh):
            lse = jnp.where(head_lane == h, lse_col[h * nq:(h + 1) * nq], lse)
        l_ref[b] = lse


def swa_sample(qkv, cache_kt, cache_vt, layer, group, n_new):
    batch = qkv.shape[0]
    _, _, heads, hd, rows = cache_kt.shape
    dil = SWA_GROUPS[group][1]
    per_step = max(1, min(8, 2048 // rows))
    blk = (per_step, SAMPLE_ROWS, SWA_WIDTH)
    c_spec = pl.BlockSpec((None, per_step, heads, hd, rows), lambda b: (layer, b, 0, 0, 0))
    shape = jax.ShapeDtypeStruct((batch, SAMPLE_ROWS, SWA_WIDTH), F32)
    nc = SAMPLE_ROWS * heads
    return pl.pallas_call(
        functools.partial(_swa_sample_kernel, dil=dil, n_new=n_new),
        grid=(batch // per_step,),
        in_specs=[pl.BlockSpec(blk, lambda b: (b, 0, 0)),
                  pl.BlockSpec(blk, lambda b: (b, 0, 1)),
                  pl.BlockSpec(blk, lambda b: (b, 0, 2)),
                  c_spec, c_spec],
        out_specs=[pl.BlockSpec(blk, lambda b: (b, 0, 0)),
                   pl.BlockSpec((per_step, SAMPLE_ROWS, LANES), lambda b: (b, 0, 0))],
        out_shape=[shape, jax.ShapeDtypeStruct((batch, SAMPLE_ROWS, LANES), F32)],
        scratch_shapes=[pltpu.VMEM((nc, rows), F32), pltpu.VMEM((nc, SAMPLE_ROWS), F32)],
        compiler_params=_cparams("arbitrary"),
        name="swa_sample_g%d" % group,
    )(qkv, qkv, qkv, cache_kt, cache_vt)


def _attn_out_kernel(x_ref, o0, o1, o2, l0, l1, l2, w_ref, out_ref, *scr, dils):
    tm = x_ref.shape[0]
    scr = list(scr)

    def rows(ref, dil):
        if dil == 1:
            return ref[0]
        buf = scr.pop()
        for c in range(buf.shape[0]):
            for r in range(dil):
                buf[c, pl.ds(r, tm // dil, stride=dil), :] = ref[r, :, c * LANES:(c + 1) * LANES]
        return jnp.concatenate([buf[c] for c in range(buf.shape[0])], -1)

    a, b, c = rows(l0, dils[0]), rows(l1, dils[1]), rows(l2, dils[2])
    m = jnp.maximum(jnp.maximum(a, b), c)
    ea, eb, ec = jnp.exp(a - m), jnp.exp(b - m), jnp.exp(c - m)
    den = ea + eb + ec
    spread = (lax.broadcasted_iota(jnp.int32, (LANES, SWA_WIDTH), 1) // SWA_HEAD_DIM
              == lax.broadcasted_iota(jnp.int32, (LANES, SWA_WIDTH), 0)).astype(BF16)

    def per_lane(wgt):
        hi = wgt.astype(BF16)
        lo = (wgt - hi.astype(F32)).astype(BF16)
        return _dot(hi, spread) + _dot(lo, spread)

    o = (per_lane(ea / den) * rows(o0, dils[0]) + per_lane(eb / den) * rows(o1, dils[1])
         + per_lane(ec / den) * rows(o2, dils[2]))
    out_ref[...] = x_ref[...] + _dot(o.astype(BF16), w_ref[...])


def attn_out(x, outs, lses, w_o, layer, batch, tm):
    t = x.shape[0]
    per_b = t // batch // tm
    dils = tuple(o.shape[1] for o in outs)
    spec = lambda d, w: pl.BlockSpec((None, d, tm // d, w), lambda i: (i // per_b, 0, i % per_b, 0))
    n_scr = sum(d > 1 for d in dils)
    return pl.pallas_call(
        functools.partial(_attn_out_kernel, dils=dils),
        grid=(t // tm,),
        in_specs=[pl.BlockSpec((tm, D_MODEL), lambda i: (i, 0))]
                 + [spec(d, SWA_WIDTH) for d in dils] + [spec(d, LANES) for d in dils]
                 + [pl.BlockSpec((None, SWA_WIDTH, D_MODEL), lambda i: (layer, 0, 0), pipeline_mode=pl.Buffered(1))],
        out_specs=pl.BlockSpec((tm, D_MODEL), lambda i: (i, 0)),
        out_shape=jax.ShapeDtypeStruct((t, D_MODEL), F32),
        scratch_shapes=[pltpu.VMEM((SWA_WIDTH // LANES, tm, LANES), F32)] * n_scr
                       + [pltpu.VMEM((1, tm, LANES), F32)] * n_scr,
        compiler_params=_cparams("parallel"),
        name="attn_out",
    )(x, *outs, *lses, w_o)


SCAN_PASSES = 8
GROUPS_PER_STEP = 8


def _ssm_param_kernel(*refs, q):
    *per_group, e_scr = refs
    for i in range(GROUPS_PER_STEP):
        _ssm_param_group(*[r.at[i] for r in per_group], e_scr, q=q)


def _ssm_param_group(lr_ref, li_ref, ls_ref, btr_ref, bti_ref, cr_ref, ci_ref,
                     win_ref, wout_ref, tt_ref, aqr_ref, aqi_ref, e_scr, *, q):
    g = SSM_GROUP
    lr, li = lr_ref[...], li_ref[...]
    step = jnp.exp(ls_ref[...])
    lo = lax.broadcasted_iota(jnp.int32, (1, LANES), 1) < SSM_STATE

    def powers(tau):
        mag = jnp.exp(tau * (lr * step))
        ang = tau * (li * step)
        return mag * jnp.cos(ang), mag * jnp.sin(ang)

    tau = lax.broadcasted_iota(jnp.int32, (q + 1, LANES), 0).astype(F32)
    c2, s2 = powers(tau)
    ar, ai = c2[1:2], s2[1:2]
    den = lr * lr + li * li
    fr = ((ar - 1.0) * lr + ai * li) / den
    fi = (ai * lr - (ar - 1.0) * li) / den
    btr, bti = btr_ref[...], bti_ref[...]
    bbr = fr * btr - fi * bti
    bbi = fr * bti + fi * btr
    bba = jnp.where(lo, bbr, bbi)
    bbb = jnp.where(lo, -bbi, bbr)
    pa = jnp.where(lo, c2, -s2)
    pb = jnp.where(lo, s2, c2)
    cr, ci = cr_ref[...], ci_ref[...]
    for t in range(q + 1):
        e_scr[t * g:(t + 1) * g, :] = cr * pa[t:t + 1] - ci * pb[t:t + 1]
    wout_ref[...] = e_scr[g:(q + 1) * g, :].astype(BF16)
    kt = _dot_nt(bba, e_scr[0:q * g, :], precision=lax.Precision.HIGHEST)
    for s in range(q):
        blk = kt if s == 0 else jnp.concatenate([jnp.zeros((g, g * s), F32), kt[:, :(q - s) * g]], 1)
        tt_ref[s * g:(s + 1) * g, :] = blk.astype(BF16)
    c2r, s2r = powers((q - 1.0) - tau[0:q])
    for s in range(q):
        win_ref[s * g:(s + 1) * g, :] = (c2r[s:s + 1] * bba + s2r[s:s + 1] * bbb).astype(BF16)
    pw = lax.broadcasted_iota(jnp.int32, (SCAN_PASSES, LANES), 0)
    cq, sq = powers((q * jnp.left_shift(1, pw)).astype(F32))
    aqr_ref[...] = cq
    aqi_ref[...] = jnp.where(lo, -sq, sq)


def ssm_params(lam_re, lam_im, log_step, b_re, b_im, c_re, c_im, q):
    gr, st, g = SSM_GROUPS, SSM_STATE, SSM_GROUP
    dup = lambda a: jnp.concatenate([a, a], -1)
    lr = dup(lam_re).reshape(gr, 1, LANES)
    li = dup(lam_im).reshape(gr, 1, LANES)
    ls = jnp.broadcast_to(log_step[:, None, None], (gr, 1, LANES))
    btr = dup(jnp.swapaxes(b_re, 1, 2))
    bti = dup(jnp.swapaxes(b_im, 1, 2))
    cr, ci = dup(c_re), dup(c_im)
    gs = GROUPS_PER_STEP
    spec = lambda r, c: pl.BlockSpec((gs, r, c), lambda i: (i, 0, 0))
    vec, mat, big, pws = spec(1, LANES), spec(g, LANES), spec(q * g, LANES), spec(SCAN_PASSES, LANES)
    return pl.pallas_call(
        functools.partial(_ssm_param_kernel, q=q),
        grid=(gr // gs,),
        in_specs=[vec, vec, vec, mat, mat, mat, mat],
        out_specs=[big, big, spec(q * g, q * g), pws, pws],
        out_shape=[jax.ShapeDtypeStruct((gr, q * g, LANES), BF16),
                   jax.ShapeDtypeStruct((gr, q * g, LANES), BF16),
                   jax.ShapeDtypeStruct((gr, q * g, q * g), BF16),
                   jax.ShapeDtypeStruct((gr, SCAN_PASSES, LANES), F32),
                   jax.ShapeDtypeStruct((gr, SCAN_PASSES, LANES), F32)],
        scratch_shapes=[pltpu.VMEM(((q + 1) * g, LANES), F32)],
        compiler_params=_cparams("parallel"),
        name="ssm_params_q%d" % q,
    )(lr, li, ls, btr, bti, cr, ci)


def _ssm_in_kernel(u_ref, win_ref, s_ref):
    for i in range(GROUPS_PER_STEP):
        s_ref[:, i * LANES:(i + 1) * LANES] = _dot(u_ref[i], win_ref[i])


def _ssm_scan_kernel(s_ref, h0_ref, ar_ref, ai_ref, hp_ref, hf_ref, h_scr):
    @pl.when(pl.program_id(0) == 0)
    def _():
        h_scr[...] = h0_ref[...]

    def body(k, h):
        hp_ref[k] = h.astype(BF16)
        return ar_ref[...] * h + ai_ref[...] * pltpu.roll(h, SSM_STATE, 1) + s_ref[k]

    h = lax.fori_loop(0, s_ref.shape[0], body, h_scr[...])
    h_scr[...] = h
    hf_ref[...] = h


def _ssm_out_kernel(u_ref, tt_ref, hp_ref, wout_ref, y_ref):
    for i in range(GROUPS_PER_STEP):
        y_ref[i] = _dot(u_ref[i], tt_ref[i]) + _dot_nt(hp_ref[:, i * LANES:(i + 1) * LANES], wout_ref[i])


def ssm_mix(h_bf16, h0_re, h0_im, params, batch, seq, q):
    win, wout, tt, aqr, aqi = params
    gr, st, g = SSM_GROUPS, SSM_STATE, SSM_GROUP
    nk = seq // q
    n = nk * batch
    qg = q * g
    u = h_bf16.reshape(batch, nk, q, gr, g).transpose(3, 1, 0, 2, 4).reshape(gr, n, qg)
    gs = GROUPS_PER_STEP
    s = pl.pallas_call(
        _ssm_in_kernel,
        grid=(gr // gs,),
        in_specs=[pl.BlockSpec((gs, n, qg), lambda i: (i, 0, 0)),
                  pl.BlockSpec((gs, qg, LANES), lambda i: (i, 0, 0))],
        out_specs=pl.BlockSpec((n, gs * LANES), lambda i: (0, i)),
        out_shape=jax.ShapeDtypeStruct((n, gr * LANES), F32),
        compiler_params=_cparams("parallel"),
        name="ssm_in_q%d" % q,
    )(u, win)
    rows = batch * gr
    kc = min(nk, 32)
    tile = lambda a: jnp.broadcast_to(a[:, 0].reshape(1, gr, LANES), (batch, gr, LANES)).reshape(rows, LANES)
    h0 = jnp.concatenate([h0_re, h0_im], -1).reshape(rows, LANES)
    full = pl.BlockSpec((rows, LANES), lambda i: (0, 0))
    hp, hf = pl.pallas_call(
        _ssm_scan_kernel,
        grid=(nk // kc,),
        in_specs=[pl.BlockSpec((kc, rows, LANES), lambda i: (i, 0, 0)), full, full, full],
        out_specs=[pl.BlockSpec((kc, rows, LANES), lambda i: (i, 0, 0)), full],
        out_shape=[jax.ShapeDtypeStruct((nk, rows, LANES), BF16),
                   jax.ShapeDtypeStruct((rows, LANES), F32)],
        scratch_shapes=[pltpu.VMEM((rows, LANES), F32)],
        compiler_params=_cparams("arbitrary"),
        name="ssm_scan_q%d" % q,
    )(s.reshape(nk, rows, LANES), h0, tile(aqr), tile(aqi))
    y = pl.pallas_call(
        _ssm_out_kernel,
        grid=(gr // gs,),
        in_specs=[pl.BlockSpec((gs, n, qg), lambda i: (i, 0, 0)),
                  pl.BlockSpec((gs, qg, qg), lambda i: (i, 0, 0)),
                  pl.BlockSpec((n, gs * LANES), lambda i: (0, i)),
                  pl.BlockSpec((gs, qg, LANES), lambda i: (i, 0, 0))],
        out_specs=pl.BlockSpec((gs, n, qg), lambda i: (i, 0, 0)),
        out_shape=jax.ShapeDtypeStruct((gr, n, qg), F32),
        compiler_params=_cparams("parallel"),
        name="ssm_out_q%d" % q,
    )(u, tt, hp.reshape(n, gr * LANES), wout)
    y = y.reshape(gr, nk, batch, q, g).transpose(2, 1, 3, 0, 4).reshape(batch * seq, D_MODEL)
    hf = hf.reshape(batch, gr, LANES)
    return y, hf[..., :st], hf[..., st:]


SSM_TOKENS_BLK = 2048


GROUPS_PER_TILE = LANES // SSM_GROUP
N_LANE_TILES = D_MODEL // LANES


def _to_chunks_kernel(*refs, q):
    x_refs, (g_ref, u_ref, r_scr, ut_scr) = refs[:N_LANE_TILES], refs[N_LANE_TILES:]
    tb = r_scr.shape[0]
    nkb = tb // q
    ss = jnp.zeros((tb, 1), F32)
    for x_ref in x_refs:
        x = x_ref[...]
        ss = ss + jnp.sum(x * x, -1, keepdims=True)
    r_scr[...] = jnp.broadcast_to(lax.rsqrt(ss * (1.0 / D_MODEL) + RMS_EPS), (tb, LANES))
    for c, x_ref in enumerate(x_refs):
        gain = g_ref[:, c * LANES:(c + 1) * LANES]
        for s in range(q):
            rows = pl.ds(s, nkb, stride=q)
            m = x_ref[rows, :] * r_scr[rows, :] * gain
            ut_scr[:, s * SSM_GROUP:(s + 1) * SSM_GROUP, :] = m.T.reshape(GROUPS_PER_TILE, SSM_GROUP, nkb)
        for grp in range(GROUPS_PER_TILE):
            u_ref[c * GROUPS_PER_TILE + grp] = ut_scr[grp].T.astype(BF16)


def _from_chunks_kernel(y_ref, o_ref, yt_scr, o_scr, *, q):
    nkb = o_ref.shape[0] // q
    for c in range(N_LANE_TILES):
        for grp in range(GROUPS_PER_TILE):
            yt_scr[grp] = y_ref[c * GROUPS_PER_TILE + grp].T
        for t in range(q):
            o_scr[pl.ds(t, nkb, stride=q), :] = yt_scr[:, t * SSM_GROUP:(t + 1) * SSM_GROUP, :].reshape(LANES, nkb).T
        o_ref[:, c * LANES:(c + 1) * LANES] = o_scr[...]


def _ssm_group_kernel(u_ref, win_ref, wout_ref, tt_ref, ar_ref, ai_ref, y_ref, hf_ref, h_scr, *, nk):
    u = u_ref[...]
    h = _dot(u, win_ref[...])
    n = h.shape[0]
    k_idx = lax.broadcasted_iota(jnp.int32, (n, LANES), 0) % nk
    d = 1
    for j in range(SCAN_PASSES):
        if d >= nk:
            break
        hs = jnp.where(k_idx >= d, pltpu.roll(h, d, 0), 0.0)
        h = h + ar_ref[j:j + 1, :] * hs + ai_ref[j:j + 1, :] * pltpu.roll(hs, SSM_STATE, 1)
        d *= 2
    hp = jnp.where(k_idx >= 1, pltpu.roll(h, 1, 0), 0.0)
    y_ref[...] = _dot(u, tt_ref[...]) + _dot_nt(hp.astype(BF16), wout_ref[...])
    h_scr[...] = h
    hf_ref[...] = h_scr[pl.ds(nk - 1, n // nk, stride=nk), :]


def ssm_mix_prompt(x, g_mix, params, batch, seq, q):
    win, wout, tt, aqr, aqi = params
    gr, st = SSM_GROUPS, SSM_STATE
    nk = seq // q
    assert nk <= 2 ** SCAN_PASSES
    n = nk * batch
    qg = q * SSM_GROUP
    tb = SSM_TOKENS_BLK
    nkb = tb // q
    per_b = seq // tb
    tok = pl.BlockSpec((tb, D_MODEL), lambda b, i: (b * per_b + i, 0))
    chk = pl.BlockSpec((gr, nkb, qg), lambda b, i: (0, b * per_b + i, 0))
    lane_tiles = [pl.BlockSpec((tb, LANES), lambda b, i, c=c: (b * per_b + i, c)) for c in range(N_LANE_TILES)]
    u = pl.pallas_call(
        functools.partial(_to_chunks_kernel, q=q),
        grid=(batch, per_b),
        in_specs=lane_tiles + [pl.BlockSpec((1, D_MODEL), lambda b, i: (0, 0))],
        out_specs=chk,
        out_shape=jax.ShapeDtypeStruct((gr, n, qg), BF16),
        scratch_shapes=[pltpu.VMEM((tb, LANES), F32), pltpu.VMEM((GROUPS_PER_TILE, qg, nkb), F32)],
        compiler_params=_cparams("parallel", "parallel"),
        name="ssm_to_chunks",
    )(*([x] * N_LANE_TILES), g_mix)
    per_g = lambda r, c: pl.BlockSpec((None, r, c), lambda i: (i, 0, 0))
    y, hf = pl.pallas_call(
        functools.partial(_ssm_group_kernel, nk=nk),
        grid=(gr,),
        in_specs=[per_g(n, qg), per_g(qg, LANES), per_g(qg, LANES), per_g(qg, qg),
                  per_g(SCAN_PASSES, LANES), per_g(SCAN_PASSES, LANES)],
        out_specs=[per_g(n, qg), per_g(batch, LANES)],
        out_shape=[jax.ShapeDtypeStruct((gr, n, qg), F32), jax.ShapeDtypeStruct((gr, batch, LANES), F32)],
        scratch_shapes=[pltpu.VMEM((n, LANES), F32)],
        compiler_params=_cparams("parallel"),
        name="ssm_group",
    )(u, win, wout, tt, aqr, aqi)
    y = pl.pallas_call(
        functools.partial(_from_chunks_kernel, q=q),
        grid=(batch, per_b),
        in_specs=[chk],
        out_specs=tok,
        out_shape=jax.ShapeDtypeStruct((batch * seq, D_MODEL), F32),
        scratch_shapes=[pltpu.VMEM((GROUPS_PER_TILE, qg, nkb), F32), pltpu.VMEM((tb, LANES), F32)],
        compiler_params=_cparams("parallel", "parallel"),
        name="ssm_from_chunks",
    )(y)
    hf = jnp.swapaxes(hf, 0, 1)
    return y, hf[..., :st], hf[..., st:]


def _ssm_glu_kernel(x_ref, gm_ref, y_ref, d_ref, w_ref, b_ref, o_ref):
    x = x_ref[...]
    y = y_ref[...] + d_ref[...] * _rms(x, gm_ref[...])
    gl = _gelu_tanh(y)
    z = _dot(gl.astype(BF16), w_ref[...]) + b_ref[...]
    o_ref[...] = x + gl * _sigmoid(z)


def ssm_glu(x, g_mix, y, d_skip, w_glu, layer, b_glu, tm):
    t = x.shape[0]
    row = pl.BlockSpec((tm, D_MODEL), lambda i: (i, 0))
    vec = pl.BlockSpec((1, D_MODEL), lambda i: (0, 0))
    return pl.pallas_call(
        _ssm_glu_kernel,
        grid=(t // tm,),
        in_specs=[row, vec, row, vec,
                  pl.BlockSpec((None, D_MODEL, D_MODEL), lambda i: (layer, 0, 0), pipeline_mode=pl.Buffered(1)), vec],
        out_specs=row,
        out_shape=jax.ShapeDtypeStruct((t, D_MODEL), F32),
        compiler_params=_cparams("parallel"),
        name="ssm_glu",
    )(x, g_mix, y, d_skip, w_glu, b_glu)


def _row(v):
    return v.reshape(1, -1).astype(F32)


def kernel(x_prompt, x_sample, mem_prompt, state_ssm_re, state_ssm_im, cache_swa0_k, cache_swa0_v, cache_swa1_k, cache_swa1_v, cache_swa2_k, cache_swa2_v, cache_mem_k, cache_mem_v, norm_mix_g, norm_mem_g, norm_memin_g, norm_ffn_g, ssm_lambda_re, ssm_lambda_im, ssm_b_re, ssm_b_im, ssm_c_re, ssm_c_im, ssm_d, ssm_log_step, ssm_w_glu, ssm_b_glu, attn_w_qkv, attn_q_norm_g, attn_k_norm_g, attn_w_o, mem_w_q, mem_w_kv, mem_q_norm_g, mem_k_norm_g, mem_w_o, ffn_w_in, ffn_w_out):
    pb, seq, _ = x_prompt.shape
    sb, dec, _ = x_sample.shape
    n_ssm = state_ssm_re.shape[0]
    n_attn = cache_swa0_k.shape[0]
    tm_p = 512
    tm_s = sb * SAMPLE_ROWS

    w_glu = ssm_w_glu.astype(BF16)
    w_qkv = attn_w_qkv.astype(BF16)
    w_ao = attn_w_o.astype(BF16)
    w_mq = mem_w_q.astype(BF16)
    w_mkv = mem_w_kv.astype(BF16)
    w_mo = mem_w_o.astype(BF16)
    w_fi = ffn_w_in.astype(BF16)
    w_fo = ffn_w_out.astype(BF16)

    head_gain = [jnp.concatenate([jnp.tile(attn_q_norm_g[j], (1, SWA_HEADS)),
                                  jnp.tile(attn_k_norm_g[j], (1, SWA_HEADS))], 1).reshape(N_SWA, 1, 2 * SWA_WIDTH)
                 for j in range(n_attn)]
    mem_q_gain = [_row(jnp.tile(mem_q_norm_g[i], MEM_HEADS)) for i in range(DEPTH)]
    mem_k_gain = jnp.tile(mem_k_norm_g, (1, MEM_HEADS)).reshape(DEPTH, 1, D_MODEL)

    mkv = mem_kv(mem_prompt.reshape(pb * N_MEM, D_MODEL), norm_memin_g.reshape(DEPTH, 1, D_MODEL),
                 w_mkv, mem_k_gain)
    mkv5 = mkv.reshape(DEPTH, pb, N_MEM, 2, MEM_HEADS, MEM_HEAD_DIM)
    p_mem_k, p_mem_v = mkv5[:, :, :, 0], mkv5[:, :, :, 1]
    mkv_b = mkv.reshape(DEPTH, pb * N_MEM, 2 * D_MODEL)

    ssm_p = [[ssm_params(ssm_lambda_re[j], ssm_lambda_im[j], ssm_log_step[j], ssm_b_re[j], ssm_b_im[j],
                         ssm_c_re[j], ssm_c_im[j], q) for q in (SSM_Q_PROMPT, dec)] for j in range(n_ssm)]

    rows_minor = lambda c: jnp.transpose(c, (0, 1, 3, 4, 2))
    caches_k = tuple(rows_minor(c) for c in (cache_swa0_k, cache_swa1_k, cache_swa2_k))
    caches_v = tuple(rows_minor(c) for c in (cache_swa0_v, cache_swa1_v, cache_swa2_v))
    dils = tuple(d for _, d in SWA_GROUPS)
    assert all(min(w, seq) == SWA_KEYS_BACK * d for w, d in SWA_GROUPS)

    xp = x_prompt.reshape(pb * seq, D_MODEL)
    xs = jnp.pad(x_sample, ((0, 0), (0, SAMPLE_ROWS - dec), (0, 0))).reshape(tm_s, D_MODEL)

    p_ssm_re, p_ssm_im, s_ssm_re, s_ssm_im = [], [], [], []
    p_qkv = [[] for _ in SWA_GROUPS]
    s_swa_k = [[] for _ in SWA_GROUPS]
    s_swa_v = [[] for _ in SWA_GROUPS]

    for i in range(DEPTH):
        j = i // 2
        g_mix = _row(norm_mix_g[i])
        if i % 2 == 0:
            d_skip, b_glu = _row(ssm_d[j]), _row(ssm_b_glu[j])
            y, fr, fi = ssm_mix_prompt(xp, g_mix, ssm_p[j][0], pb, seq, SSM_Q_PROMPT)
            p_ssm_re.append(fr)
            p_ssm_im.append(fi)
            xp = ssm_glu(xp, g_mix, y, d_skip, w_glu, j, b_glu, 2 * tm_p)
            hb = norm_cast(xs, g_mix, tm_s).reshape(sb, SAMPLE_ROWS, D_MODEL)[:, :dec].reshape(sb * dec, D_MODEL)
            y, fr, fi = ssm_mix(hb, state_ssm_re[j], state_ssm_im[j], ssm_p[j][1], sb, dec, dec)
            s_ssm_re.append(fr)
            s_ssm_im.append(fi)
            y = jnp.pad(y.reshape(sb, dec, D_MODEL), ((0, 0), (0, SAMPLE_ROWS - dec), (0, 0))).reshape(tm_s, D_MODEL)
            xs = ssm_glu(xs, g_mix, y, d_skip, w_glu, j, b_glu, tm_s)
        else:
            qkvs = qkv_proj(xp, g_mix, w_qkv, j, head_gain[j], pb, dils, tm_p)
            outs, lses = [], []
            for g in range(N_SWA):
                p_qkv[g].append(qkvs[g])
                o, lse = swa_prompt(qkvs[g], g)
                outs.append(o)
                lses.append(lse)
            xp = attn_out(xp, outs, lses, w_ao, j, pb, 2 * tm_p)
            qkvs = qkv_proj(xs, g_mix, w_qkv, j, head_gain[j], 1, (1,) * N_SWA, tm_s)
            outs, lses = [], []
            for g in range(N_SWA):
                qkv3 = qkvs[g].reshape(sb, SAMPLE_ROWS, 3 * SWA_WIDTH)
                qkv5 = qkv3.reshape(sb, SAMPLE_ROWS, 3, SWA_HEADS, SWA_HEAD_DIM)
                s_swa_k[g].append(qkv5[:, :dec, 1])
                s_swa_v[g].append(qkv5[:, :dec, 2])
                o, lse = swa_sample(qkv3, caches_k[g], caches_v[g], j, g, dec)
                outs.append(o.reshape(1, 1, tm_s, SWA_WIDTH))
                lses.append(lse.reshape(1, 1, tm_s, LANES))
            xs = attn_out(xs, outs, lses, w_ao, j, 1, tm_s)

        g_mem = _row(norm_mem_g[i])
        xp = mem_attn_prompt(xp, g_mem, w_mq, mem_q_gain[i], mkv_b, i, w_mo, pb, 1024)
        xs = mem_attn_sample(xs, g_mem, w_mq, mem_q_gain[i], cache_mem_k, cache_mem_v, i, w_mo, sb)
        g_ffn = _row(norm_ffn_g[i])
        xp = swiglu_block(xp, g_ffn, w_fi, w_fo, i, 1024)
        xs = swiglu_block(xs, g_ffn, w_fi, w_fo, i, tm_s)

    rows_major = lambda c: jnp.transpose(c, (0, 1, 4, 2, 3))
    p_swa = [[rows_major(c) for c in swa_last_rows(p_qkv[g])] for g in range(N_SWA)]
    y_prompt = xp.reshape(pb, seq, D_MODEL)
    y_sample = xs.reshape(sb, SAMPLE_ROWS, D_MODEL)[:, :dec]
    st = lambda a: jnp.stack(a, 0)
    return (y_prompt, y_sample,
            st(p_ssm_re), st(p_ssm_im),
            p_swa[0][0], p_swa[0][1], p_swa[1][0], p_swa[1][1], p_swa[2][0], p_swa[2][1],
            p_mem_k, p_mem_v,
            st(s_ssm_re), st(s_ssm_im),
            st(s_swa_k[0]), st(s_swa_v[0]), st(s_swa_k[1]), st(s_swa_v[1]), st(s_swa_k[2]), st(s_swa_v[2]))
```

```python
import functools
import math

import jax
import jax.numpy as jnp
from jax import lax
from jax.experimental import pallas as pl
from jax.experimental.pallas import tpu as pltpu

F32 = jnp.float32
BF16 = jnp.bfloat16

D_MODEL = 1024
DEPTH = 4
SSM_GROUP = 16
SSM_GROUPS = D_MODEL // SSM_GROUP
SSM_STATE = 64
SWA_GROUPS = ((128, 1), (512, 4), (2048, 16))
N_SWA = len(SWA_GROUPS)
SWA_HEADS = 8
SWA_HEAD_DIM = 64
SWA_WIDTH = SWA_HEADS * SWA_HEAD_DIM
QKV_WIDTH = 3 * N_SWA * SWA_WIDTH
SWA_KEYS_BACK = 128
N_MEM = 256
MEM_HEADS = 4
MEM_HEAD_DIM = D_MODEL // MEM_HEADS
D_FF = -(-8 * D_MODEL // (3 * 256)) * 256
RMS_EPS = 1e-6
NEG_BIG = -1e30

LANES = 128
SUBLANES = 8
VMEM_LIMIT = 48 * 1024 * 1024
Q_BLK = 128
FF_BLK = 256
SAMPLE_ROWS = SUBLANES
SSM_Q_PROMPT = 16


def _cparams(*sem):
    return pltpu.CompilerParams(dimension_semantics=sem, vmem_limit_bytes=VMEM_LIMIT)


def _dot(a, b):
    return jnp.dot(a, b, preferred_element_type=F32)


def _dot_nt(a, b, precision=None):
    return lax.dot_general(a, b, (((1,), (1,)), ((), ())), preferred_element_type=F32,
                           precision=precision)


def _rms(x, g):
    return x * lax.rsqrt(jnp.mean(x * x, -1, keepdims=True) + RMS_EPS) * g


def _headnorm(y, g_row, hd):
    tm, n = y.shape
    outs = []
    if hd % LANES == 0:
        for c in range(n // hd):
            yc = y[:, c * hd:(c + 1) * hd]
            ms = jnp.mean(yc * yc, -1, keepdims=True)
            outs.append(yc * lax.rsqrt(ms + RMS_EPS))
    else:
        assert 2 * hd == LANES
        lo = lax.broadcasted_iota(jnp.int32, (tm, LANES), 1) < hd
        for c in range(n // LANES):
            yc = y[:, c * LANES:(c + 1) * LANES]
            sq = yc * yc
            s_lo = jnp.sum(jnp.where(lo, sq, 0.0), -1, keepdims=True)
            s_hi = jnp.sum(jnp.where(lo, 0.0, sq), -1, keepdims=True)
            ms = jnp.where(lo, s_lo, s_hi) * (1.0 / hd)
            outs.append(yc * lax.rsqrt(ms + RMS_EPS))
    return jnp.concatenate(outs, -1) * g_row


def _sigmoid(x):
    return 1.0 / (1.0 + jnp.exp(-x))


def _gelu_tanh(x):
    return 0.5 * x * (1.0 + jnp.tanh(math.sqrt(2.0 / math.pi) * (x + 0.044715 * (x * x * x))))


def _norm_cast_kernel(x_ref, g_ref, o_ref):
    o_ref[...] = _rms(x_ref[...], g_ref[...]).astype(BF16)


def norm_cast(x, g, tm):
    t = x.shape[0]
    return pl.pallas_call(
        _norm_cast_kernel,
        grid=(t // tm,),
        in_specs=[pl.BlockSpec((tm, D_MODEL), lambda i: (i, 0)),
                  pl.BlockSpec((1, D_MODEL), lambda i: (0, 0))],
        out_specs=pl.BlockSpec((tm, D_MODEL), lambda i: (i, 0)),
        out_shape=jax.ShapeDtypeStruct((t, D_MODEL), BF16),
        compiler_params=_cparams("parallel"),
        name="norm_cast",
    )(x, g)


def _qkv_kernel(x_ref, g_ref, w_ref, hg_ref, o0_ref, o1_ref, o2_ref, y_ref, *, dils):
    tm = x_ref.shape[0]
    w = SWA_WIDTH
    xn = _rms(x_ref[...], g_ref[...]).astype(BF16)
    for grp, (o_ref, dil) in enumerate(zip((o0_ref, o1_ref, o2_ref), dils)):
        col = lambda kind: slice((kind * N_SWA + grp) * w, (kind * N_SWA + grp + 1) * w)
        parts = (_headnorm(_dot(xn, w_ref[:, col(0)]), hg_ref[grp, :, :w], SWA_HEAD_DIM),
                 _headnorm(_dot(xn, w_ref[:, col(1)]), hg_ref[grp, :, w:], SWA_HEAD_DIM),
                 _dot(xn, w_ref[:, col(2)]))
        if dil == 1:
            for kind, y in enumerate(parts):
                o_ref[0, :, kind * w:(kind + 1) * w] = y
        else:
            per = w // LANES
            for c in range(y_ref.shape[0]):
                y_ref[c] = parts[c // per][:, (c % per) * LANES:(c % per + 1) * LANES]
                for r in range(dil):
                    o_ref[r, :, c * LANES:(c + 1) * LANES] = y_ref[c, pl.ds(r, tm // dil, stride=dil), :]


def qkv_proj(x, g, w, layer, head_gain, batch, dils, tm):
    t = x.shape[0]
    seq = t // batch
    per_b = seq // tm
    gw = 3 * SWA_WIDTH
    return pl.pallas_call(
        functools.partial(_qkv_kernel, dils=dils),
        grid=(t // tm,),
        in_specs=[pl.BlockSpec((tm, D_MODEL), lambda i: (i, 0)),
                  pl.BlockSpec((1, D_MODEL), lambda i: (0, 0)),
                  pl.BlockSpec((None, D_MODEL, QKV_WIDTH), lambda i: (layer, 0, 0), pipeline_mode=pl.Buffered(1)),
                  pl.BlockSpec((N_SWA, 1, 2 * SWA_WIDTH), lambda i: (0, 0, 0))],
        out_specs=[pl.BlockSpec((None, d, tm // d, gw), lambda i: (i // per_b, 0, i % per_b, 0))
                   for d in dils],
        out_shape=[jax.ShapeDtypeStruct((batch, d, seq // d, gw), F32) for d in dils],
        scratch_shapes=[pltpu.VMEM((gw // LANES, tm, LANES), F32)],
        compiler_params=_cparams("parallel"),
        name="qkv_proj",
    )(x, g, w, head_gain)


def _mem_kv_kernel(x_ref, g_ref, w_ref, kg_ref, o_ref, ko_ref, vo_ref):
    j = pl.program_id(1)
    y = _dot(_rms(x_ref[...], g_ref[...]).astype(BF16), w_ref[...])

    def emit(val, by_head_ref):
        o_ref[...] = val
        for h in range(MEM_HEADS):
            by_head_ref[:, h, :] = val[:, h * MEM_HEAD_DIM:(h + 1) * MEM_HEAD_DIM]

    @pl.when(j == 0)
    def _():
        emit(_headnorm(y, kg_ref[...], MEM_HEAD_DIM), ko_ref)

    @pl.when(j == 1)
    def _():
        emit(y, vo_ref)


def mem_kv(mem, g_in, w_kv, k_gain):
    t = mem.shape[0]
    by_head = pl.BlockSpec((None, t, MEM_HEADS, MEM_HEAD_DIM), lambda l, j: (l, 0, 0, 0))
    by_head_shape = jax.ShapeDtypeStruct((DEPTH, t, MEM_HEADS, MEM_HEAD_DIM), F32)
    return pl.pallas_call(
        _mem_kv_kernel,
        grid=(DEPTH, 2),
        in_specs=[pl.BlockSpec((t, D_MODEL), lambda l, j: (0, 0)),
                  pl.BlockSpec((None, 1, D_MODEL), lambda l, j: (l, 0, 0)),
                  pl.BlockSpec((None, D_MODEL, D_MODEL), lambda l, j: (l, 0, j)),
                  pl.BlockSpec((None, 1, D_MODEL), lambda l, j: (l, 0, 0))],
        out_specs=[pl.BlockSpec((None, t, D_MODEL), lambda l, j: (l, 0, j)), by_head, by_head],
        out_shape=[jax.ShapeDtypeStruct((DEPTH, t, 2 * D_MODEL), F32), by_head_shape, by_head_shape],
        compiler_params=_cparams("parallel", "arbitrary"),
        name="mem_kv",
    )(mem, g_in, w_kv, k_gain)


def _swiglu_kernel(x_ref, g_ref, wi_ref, wo_ref, o_ref):
    x = x_ref[...]
    xn = _rms(x, g_ref[...]).astype(BF16)
    acc = x
    for lo in range(0, D_FF, FF_BLK):
        hi = min(lo + FF_BLK, D_FF)
        gate = _dot(xn, wi_ref[:, lo:hi])
        up = _dot(xn, wi_ref[:, D_FF + lo:D_FF + hi])
        act = gate * _sigmoid(gate) * up
        acc = acc + _dot(act.astype(BF16), wo_ref[lo:hi, :])
    o_ref[...] = acc


def swiglu_block(x, g, w_in, w_out, layer, tm):
    t = x.shape[0]
    once = pl.Buffered(1)
    return pl.pallas_call(
        _swiglu_kernel,
        grid=(t // tm,),
        in_specs=[pl.BlockSpec((tm, D_MODEL), lambda i: (i, 0)),
                  pl.BlockSpec((1, D_MODEL), lambda i: (0, 0)),
                  pl.BlockSpec((None, D_MODEL, 2 * D_FF), lambda i: (layer, 0, 0), pipeline_mode=once),
                  pl.BlockSpec((None, D_FF, D_MODEL), lambda i: (layer, 0, 0), pipeline_mode=once)],
        out_specs=pl.BlockSpec((tm, D_MODEL), lambda i: (i, 0)),
        out_shape=jax.ShapeDtypeStruct((t, D_MODEL), F32),
        compiler_params=_cparams("parallel"),
        name="swiglu",
    )(x, g, w_in, w_out)


def _mem_heads(q, key_head, value_head):
    outs = []
    for h in range(MEM_HEADS):
        sl = slice(h * MEM_HEAD_DIM, (h + 1) * MEM_HEAD_DIM)
        s = _dot_nt(q[:, sl].astype(BF16), key_head(h))
        m = jnp.max(s, -1, keepdims=True)
        p = jnp.exp(s - m)
        l = jnp.sum(p, -1, keepdims=True)
        outs.append(_dot(p.astype(BF16), value_head(h)) / l)
    return jnp.concatenate(outs, -1)


def _mem_attn_kernel(x_ref, g_ref, wq_ref, qg_ref, mk_ref, mv_ref, wo_ref, o_ref):
    x = x_ref[...]
    q = _dot(_rms(x, g_ref[...]).astype(BF16), wq_ref[...])
    q = _headnorm(q, qg_ref[...], MEM_HEAD_DIM) * (MEM_HEAD_DIM ** -0.5)
    head = lambda ref: lambda h: ref[:, h * MEM_HEAD_DIM:(h + 1) * MEM_HEAD_DIM].astype(BF16)
    o = _mem_heads(q, head(mk_ref), head(mv_ref))
    o_ref[...] = x + _dot(o.astype(BF16), wo_ref[...])


def mem_attn_prompt(x, g, w_q, q_gain, mkv, layer, w_o, batch, tm):
    t = x.shape[0]
    per_b = t // batch // tm
    wspec = pl.BlockSpec((None, D_MODEL, D_MODEL), lambda b, i: (layer, 0, 0), pipeline_mode=pl.Buffered(1))
    return pl.pallas_call(
        _mem_attn_kernel,
        grid=(batch, per_b),
        in_specs=[pl.BlockSpec((tm, D_MODEL), lambda b, i: (b * per_b + i, 0)),
                  pl.BlockSpec((1, D_MODEL), lambda b, i: (0, 0)),
                  wspec,
                  pl.BlockSpec((1, D_MODEL), lambda b, i: (0, 0)),
                  pl.BlockSpec((None, N_MEM, D_MODEL), lambda b, i: (layer, b, 0)),
                  pl.BlockSpec((None, N_MEM, D_MODEL), lambda b, i: (layer, b, 1)),
                  wspec],
        out_specs=pl.BlockSpec((tm, D_MODEL), lambda b, i: (b * per_b + i, 0)),
        out_shape=jax.ShapeDtypeStruct((t, D_MODEL), F32),
        compiler_params=_cparams("parallel", "arbitrary"),
        name="mem_attn_prompt",
    )(x, g, w_q, q_gain, mkv, mkv, w_o)


def _mem_attn_sample_kernel(x_ref, g_ref, wq_ref, qg_ref, mk_ref, mv_ref, wo_ref, o_ref,
                            q_scr, o_scr):
    b = pl.program_id(0)

    @pl.when(b == 0)
    def _():
        q = _dot(_rms(x_ref[...], g_ref[...]).astype(BF16), wq_ref[...])
        q_scr[...] = _headnorm(q, qg_ref[...], MEM_HEAD_DIM) * (MEM_HEAD_DIM ** -0.5)

    hd, nr, nk = MEM_HEAD_DIM, MEM_HEADS * SAMPLE_ROWS, N_MEM * MEM_HEADS
    own = (lax.broadcasted_iota(jnp.int32, (nr, nk), 1) % MEM_HEADS
           == lax.broadcasted_iota(jnp.int32, (nr, nk), 0) // SAMPLE_ROWS)
    for bi in range(mk_ref.shape[0]):
        rows = pl.ds(pl.multiple_of((b * mk_ref.shape[0] + bi) * SAMPLE_ROWS, SAMPLE_ROWS), SAMPLE_ROWS)
        q = q_scr[rows, :]
        qx = jnp.concatenate([q[:, h * hd:(h + 1) * hd] for h in range(MEM_HEADS)], 0).astype(BF16)
        s = jnp.where(own, _dot_nt(qx, mk_ref[bi].reshape(nk, hd).astype(BF16)), NEG_BIG)
        p = jnp.exp(s - jnp.max(s, -1, keepdims=True))
        o = _dot(p.astype(BF16), mv_ref[bi].reshape(nk, hd).astype(BF16)) / jnp.sum(p, -1, keepdims=True)
        o_scr[rows, :] = jnp.concatenate([o[h * SAMPLE_ROWS:(h + 1) * SAMPLE_ROWS] for h in range(MEM_HEADS)], -1)

    @pl.when(b == pl.num_programs(0) - 1)
    def _():
        o_ref[...] = x_ref[...] + _dot(o_scr[...].astype(BF16), wo_ref[...])


def mem_attn_sample(x, g, w_q, q_gain, cache_k, cache_v, layer, w_o, batch):
    t = x.shape[0]
    const = lambda b: (0, 0)
    wspec = pl.BlockSpec((None, D_MODEL, D_MODEL), lambda b: (layer, 0, 0))
    per_step = 4
    cspec = pl.BlockSpec((None, per_step, N_MEM, MEM_HEADS, MEM_HEAD_DIM), lambda b: (layer, b, 0, 0, 0))
    return pl.pallas_call(
        _mem_attn_sample_kernel,
        grid=(batch // per_step,),
        in_specs=[pl.BlockSpec((t, D_MODEL), const),
                  pl.BlockSpec((1, D_MODEL), const),
                  wspec,
                  pl.BlockSpec((1, D_MODEL), const),
                  cspec, cspec, wspec],
        out_specs=pl.BlockSpec((t, D_MODEL), const),
        out_shape=jax.ShapeDtypeStruct((t, D_MODEL), F32),
        scratch_shapes=[pltpu.VMEM((t, D_MODEL), F32), pltpu.VMEM((t, D_MODEL), F32)],
        compiler_params=_cparams("arbitrary"),
        name="mem_attn_sample",
    )(x, g, w_q, q_gain, cache_k, cache_v, w_o)


MAX_Q_SUB = 4


def _swa_prompt_kernel(q_ref, kp_ref, kc_ref, vp_ref, vc_ref, o_ref, l_ref):
    for res in range(q_ref.shape[0]):
        _swa_prompt_rows(*[r.at[res] for r in (q_ref, kp_ref, kc_ref, vp_ref, vc_ref, o_ref, l_ref)])


def _swa_prompt_rows(q_ref, kp_ref, kc_ref, vp_ref, vc_ref, o_ref, l_ref):
    i = pl.program_id(2)
    q = (q_ref[...] * (SWA_HEAD_DIM ** -0.5)).astype(BF16)
    k = jnp.concatenate([kp_ref[...], kc_ref[...]], 0).astype(BF16)
    v = jnp.concatenate([vp_ref[...], vc_ref[...]], 0).astype(BF16)
    qq = lax.broadcasted_iota(jnp.int32, (Q_BLK, 2 * Q_BLK), 0)
    kk = lax.broadcasted_iota(jnp.int32, (Q_BLK, 2 * Q_BLK), 1)
    band = (kk >= qq) & (kk <= qq + SWA_KEYS_BACK)
    head_lane = lax.broadcasted_iota(jnp.int32, (Q_BLK, LANES), 1)
    head_of_lane = head_lane // SWA_HEAD_DIM
    pair_ids = range(LANES // SWA_HEAD_DIM)
    for sub in range(q_ref.shape[0] // Q_BLK):
        rows = slice(sub * Q_BLK, (sub + 1) * Q_BLK)
        win = slice(sub * Q_BLK, (sub + 2) * Q_BLK)
        valid = band if sub else band & ((kk >= Q_BLK) | (i > 0))
        outs = []
        lse = jnp.zeros((Q_BLK, LANES), F32)
        for c in range(SWA_WIDTH // LANES):
            tile = slice(c * LANES, (c + 1) * LANES)
            ql, kl, vl = q[rows, tile], k[win, tile], v[win, tile]
            pair = []
            for e in pair_ids:
                qh = jnp.where(head_of_lane == e, ql, jnp.zeros_like(ql))
                s = jnp.where(valid, _dot_nt(qh, kl), NEG_BIG)
                m = jnp.max(s, -1, keepdims=True)
                p = jnp.exp(s - m)
                l = jnp.sum(p, -1, keepdims=True)
                pair.append(_dot(p.astype(BF16), vl) / l)
                lse = jnp.where(head_lane == c * len(pair_ids) + e, m + jnp.log(l), lse)
            outs.append(jnp.where(head_of_lane == 0, pair[0], pair[1]))
        o_ref[rows, :] = jnp.concatenate(outs, -1)
        l_ref[rows, :] = lse


def swa_prompt(qkv, group):
    batch, dil, rows, _ = qkv.shape
    q_sub = min(MAX_Q_SUB, rows // Q_BLK)
    n_res = min(dil, MAX_Q_SUB // q_sub)
    blk = (None, n_res, q_sub * Q_BLK, SWA_WIDTH)
    pblk = (None, n_res, Q_BLK, SWA_WIDTH)
    prev = lambda i: jnp.maximum(i * q_sub - 1, 0)
    q_spec = pl.BlockSpec(blk, lambda b, r, i: (b, r, i, 0))
    kp_spec = pl.BlockSpec(pblk, lambda b, r, i: (b, r, prev(i), 1))
    kc_spec = pl.BlockSpec(blk, lambda b, r, i: (b, r, i, 1))
    vp_spec = pl.BlockSpec(pblk, lambda b, r, i: (b, r, prev(i), 2))
    vc_spec = pl.BlockSpec(blk, lambda b, r, i: (b, r, i, 2))
    l_spec = pl.BlockSpec((None, n_res, q_sub * Q_BLK, LANES), lambda b, r, i: (b, r, i, 0))
    return pl.pallas_call(
        _swa_prompt_kernel,
        grid=(batch, dil // n_res, rows // (q_sub * Q_BLK)),
        in_specs=[q_spec, kp_spec, kc_spec, vp_spec, vc_spec],
        out_specs=[q_spec, l_spec],
        out_shape=[jax.ShapeDtypeStruct((batch, dil, rows, SWA_WIDTH), F32),
                   jax.ShapeDtypeStruct((batch, dil, rows, LANES), F32)],
        compiler_params=_cparams("parallel", "parallel", "arbitrary"),
        name="swa_prompt_g%d" % group,
    )(qkv, qkv, qkv, qkv, qkv)


def _swa_rows_kernel(*refs, dil, n_layers):
    srcs, (ko_ref, vo_ref, buf) = refs[:2 * n_layers], refs[2 * n_layers:]
    layer = pl.program_id(0)
    pair = LANES // SWA_HEAD_DIM
    for li in range(n_layers):
        @pl.when(layer == li)
        def _(li=li):
            for src, dst in ((srcs[2 * li], ko_ref), (srcs[2 * li + 1], vo_ref)):
                for c in range(SWA_WIDTH // LANES):
                    for r in range(dil):
                        buf[pl.ds(r, src.shape[1], stride=dil), :] = src[r, :, c * LANES:(c + 1) * LANES]
                    dst[c * pair:(c + 1) * pair] = buf[...].T.reshape(pair, SWA_HEAD_DIM, buf.shape[0])


def swa_last_rows(qkvs):
    n_layers = len(qkvs)
    batch, dil, rows, _ = qkvs[0].shape
    keep = SWA_KEYS_BACK * dil
    pos = min(SWA_KEYS_BACK, 1024 // dil)
    nb = SWA_KEYS_BACK // pos
    first = (rows - SWA_KEYS_BACK) // pos
    in_specs, args = [], []
    for qkv in qkvs:
        for kind in (1, 2):
            in_specs.append(pl.BlockSpec((None, dil, pos, SWA_WIDTH),
                                         lambda l, b, a, kind=kind: (b, 0, first + a, kind)))
            args.append(qkv)
    o_spec = pl.BlockSpec((None, None, SWA_HEADS, SWA_HEAD_DIM, pos * dil), lambda l, b, a: (l, b, 0, 0, a))
    shape = jax.ShapeDtypeStruct((n_layers, batch, SWA_HEADS, SWA_HEAD_DIM, keep), F32)
    return pl.pallas_call(
        functools.partial(_swa_rows_kernel, dil=dil, n_layers=n_layers),
        grid=(n_layers, batch, nb),
        in_specs=in_specs,
        out_specs=[o_spec, o_spec],
        out_shape=[shape, shape],
        scratch_shapes=[pltpu.VMEM((pos * dil, LANES), F32)],
        compiler_params=_cparams("parallel", "parallel", "parallel"),
        name="swa_last_rows_d%d" % dil,
    )(*args)


def _swa_sample_kernel(q_ref, kn_ref, vn_ref, kt_ref, vt_ref, o_ref, l_ref, bias_c, bias_n, *, dil, n_new):
    nq, nh, hd = SAMPLE_ROWS, SWA_HEADS, SWA_HEAD_DIM
    nc = nh * nq
    rows = kt_ref.shape[-1]

    @pl.when(pl.program_id(0) == 0)
    def _():
        def bias(n, valid):
            i = lax.broadcasted_iota(jnp.int32, (nc, n), 0) % nq
            return jnp.where(valid(i, lax.broadcasted_iota(jnp.int32, (nc, n), 1)), 0.0, NEG_BIG)
        bias_c[...] = bias(rows, lambda i, c: (c >= i) & ((c - i) % dil == 0))
        bias_n[...] = bias(nq, lambda i, j: (j < n_new) & (j <= i) & ((i - j) % dil == 0))

    own = (lax.broadcasted_iota(jnp.int32, (nc, SWA_WIDTH), 1) // hd
           == lax.broadcasted_iota(jnp.int32, (nc, SWA_WIDTH), 0) // nq)
    head_lane = lax.broadcasted_iota(jnp.int32, (nq, LANES), 1)
    for b in range(q_ref.shape[0]):
        q = q_ref[b] * (hd ** -0.5)
        qbd = jnp.where(own, jnp.concatenate([q] * nh, 0), 0.0).astype(BF16)
        s_c = _dot(qbd, kt_ref[b].reshape(SWA_WIDTH, rows).astype(BF16)) + bias_c[...]
        s_n = _dot_nt(qbd, kn_ref[b].astype(BF16)) + bias_n[...]
        m = jnp.maximum(jnp.max(s_c, -1, keepdims=True), jnp.max(s_n, -1, keepdims=True))
        p_c, p_n = jnp.exp(s_c - m), jnp.exp(s_n - m)
        l = jnp.sum(p_c, -1, keepdims=True) + jnp.sum(p_n, -1, keepdims=True)
        acc = (_dot_nt(p_c.astype(BF16), vt_ref[b].reshape(SWA_WIDTH, rows).astype(BF16))
               + _dot(p_n.astype(BF16), vn_ref[b].astype(BF16)))
        o = acc / l
        o_ref[b] = jnp.concatenate([o[h * nq:(h + 1) * nq, h * hd:(h + 1) * hd] for h in range(nh)], -1)
        lse_col = m + jnp.log(l)
        lse = jnp.zeros((nq, LANES), F32)
        for h in range(nh):
            lse = jnp.where(head_lane == h, lse_col[h * nq:(h + 1) * nq], lse)
        l_ref[b] = lse


def swa_sample(qkv, cache_kt, cache_vt, layer, group, n_new):
    batch = qkv.shape[0]
    _, _, heads, hd, rows = cache_kt.shape
    dil = SWA_GROUPS[group][1]
    per_step = max(1, min(8, 2048 // rows))
    blk = (per_step, SAMPLE_ROWS, SWA_WIDTH)
    c_spec = pl.BlockSpec((None, per_step, heads, hd, rows), lambda b: (layer, b, 0, 0, 0))
    shape = jax.ShapeDtypeStruct((batch, SAMPLE_ROWS, SWA_WIDTH), F32)
    nc = SAMPLE_ROWS * heads
    return pl.pallas_call(
        functools.partial(_swa_sample_kernel, dil=dil, n_new=n_new),
        grid=(batch // per_step,),
        in_specs=[pl.BlockSpec(blk, lambda b: (b, 0, 0)),
                  pl.BlockSpec(blk, lambda b: (b, 0, 1)),
                  pl.BlockSpec(blk, lambda b: (b, 0, 2)),
                  c_spec, c_spec],
        out_specs=[pl.BlockSpec(blk, lambda b: (b, 0, 0)),
                   pl.BlockSpec((per_step, SAMPLE_ROWS, LANES), lambda b: (b, 0, 0))],
        out_shape=[shape, jax.ShapeDtypeStruct((batch, SAMPLE_ROWS, LANES), F32)],
        scratch_shapes=[pltpu.VMEM((nc, rows), F32), pltpu.VMEM((nc, SAMPLE_ROWS), F32)],
        compiler_params=_cparams("arbitrary"),
        name="swa_sample_g%d" % group,
    )(qkv, qkv, qkv, cache_kt, cache_vt)


def _attn_out_kernel(x_ref, o0, o1, o2, l0, l1, l2, w_ref, out_ref, *scr, dils):
    tm = x_ref.shape[0]
    scr = list(scr)

    def rows(ref, dil):
        if dil == 1:
            return ref[0]
        buf = scr.pop()
        for c in range(buf.shape[0]):
            for r in range(dil):
                buf[c, pl.ds(r, tm // dil, stride=dil), :] = ref[r, :, c * LANES:(c + 1) * LANES]
        return jnp.concatenate([buf[c] for c in range(buf.shape[0])], -1)

    a, b, c = rows(l0, dils[0]), rows(l1, dils[1]), rows(l2, dils[2])
    m = jnp.maximum(jnp.maximum(a, b), c)
    ea, eb, ec = jnp.exp(a - m), jnp.exp(b - m), jnp.exp(c - m)
    den = ea + eb + ec
    spread = (lax.broadcasted_iota(jnp.int32, (LANES, SWA_WIDTH), 1) // SWA_HEAD_DIM
              == lax.broadcasted_iota(jnp.int32, (LANES, SWA_WIDTH), 0)).astype(BF16)

    def per_lane(wgt):
        hi = wgt.astype(BF16)
        lo = (wgt - hi.astype(F32)).astype(BF16)
        return _dot(hi, spread) + _dot(lo, spread)

    o = (per_lane(ea / den) * rows(o0, dils[0]) + per_lane(eb / den) * rows(o1, dils[1])
         + per_lane(ec / den) * rows(o2, dils[2]))
    out_ref[...] = x_ref[...] + _dot(o.astype(BF16), w_ref[...])


def attn_out(x, outs, lses, w_o, layer, batch, tm):
    t = x.shape[0]
    per_b = t // batch // tm
    dils = tuple(o.shape[1] for o in outs)
    spec = lambda d, w: pl.BlockSpec((None, d, tm // d, w), lambda i: (i // per_b, 0, i % per_b, 0))
    n_scr = sum(d > 1 for d in dils)
    return pl.pallas_call(
        functools.partial(_attn_out_kernel, dils=dils),
        grid=(t // tm,),
        in_specs=[pl.BlockSpec((tm, D_MODEL), lambda i: (i, 0))]
                 + [spec(d, SWA_WIDTH) for d in dils] + [spec(d, LANES) for d in dils]
                 + [pl.BlockSpec((None, SWA_WIDTH, D_MODEL), lambda i: (layer, 0, 0), pipeline_mode=pl.Buffered(1))],
        out_specs=pl.BlockSpec((tm, D_MODEL), lambda i: (i, 0)),
        out_shape=jax.ShapeDtypeStruct((t, D_MODEL), F32),
        scratch_shapes=[pltpu.VMEM((SWA_WIDTH // LANES, tm, LANES), F32)] * n_scr
                       + [pltpu.VMEM((1, tm, LANES), F32)] * n_scr,
        compiler_params=_cparams("parallel"),
        name="attn_out",
    )(x, *outs, *lses, w_o)


SCAN_PASSES = 8
GROUPS_PER_STEP = 8


def _ssm_param_kernel(*refs, q):
    *per_group, e_scr = refs
    for i in range(GROUPS_PER_STEP):
        _ssm_param_group(*[r.at[i] for r in per_group], e_scr, q=q)


def _ssm_param_group(lr_ref, li_ref, ls_ref, btr_ref, bti_ref, cr_ref, ci_ref,
                     win_ref, wout_ref, tt_ref, aqr_ref, aqi_ref, e_scr, *, q):
    g = SSM_GROUP
    lr, li = lr_ref[...], li_ref[...]
    step = jnp.exp(ls_ref[...])
    lo = lax.broadcasted_iota(jnp.int32, (1, LANES), 1) < SSM_STATE

    def powers(tau):
        mag = jnp.exp(tau * (lr * step))
        ang = tau * (li * step)
        return mag * jnp.cos(ang), mag * jnp.sin(ang)

    tau = lax.broadcasted_iota(jnp.int32, (q + 1, LANES), 0).astype(F32)
    c2, s2 = powers(tau)
    ar, ai = c2[1:2], s2[1:2]
    den = lr * lr + li * li
    fr = ((ar - 1.0) * lr + ai * li) / den
    fi = (ai * lr - (ar - 1.0) * li) / den
    btr, bti = btr_ref[...], bti_ref[...]
    bbr = fr * btr - fi * bti
    bbi = fr * bti + fi * btr
    bba = jnp.where(lo, bbr, bbi)
    bbb = jnp.where(lo, -bbi, bbr)
    pa = jnp.where(lo, c2, -s2)
    pb = jnp.where(lo, s2, c2)
    cr, ci = cr_ref[...], ci_ref[...]
    for t in range(q + 1):
        e_scr[t * g:(t + 1) * g, :] = cr * pa[t:t + 1] - ci * pb[t:t + 1]
    wout_ref[...] = e_scr[g:(q + 1) * g, :].astype(BF16)
    kt = _dot_nt(bba, e_scr[0:q * g, :], precision=lax.Precision.HIGHEST)
    for s in range(q):
        blk = kt if s == 0 else jnp.concatenate([jnp.zeros((g, g * s), F32), kt[:, :(q - s) * g]], 1)
        tt_ref[s * g:(s + 1) * g, :] = blk.astype(BF16)
    c2r, s2r = powers((q - 1.0) - tau[0:q])
    for s in range(q):
        win_ref[s * g:(s + 1) * g, :] = (c2r[s:s + 1] * bba + s2r[s:s + 1] * bbb).astype(BF16)
    pw = lax.broadcasted_iota(jnp.int32, (SCAN_PASSES, LANES), 0)
    cq, sq = powers((q * jnp.left_shift(1, pw)).astype(F32))
    aqr_ref[...] = cq
    aqi_ref[...] = jnp.where(lo, -sq, sq)


def ssm_params(lam_re, lam_im, log_step, b_re, b_im, c_re, c_im, q):
    gr, st, g = SSM_GROUPS, SSM_STATE, SSM_GROUP
    dup = lambda a: jnp.concatenate([a, a], -1)
    lr = dup(lam_re).reshape(gr, 1, LANES)
    li = dup(lam_im).reshape(gr, 1, LANES)
    ls = jnp.broadcast_to(log_step[:, None, None], (gr, 1, LANES))
    btr = dup(jnp.swapaxes(b_re, 1, 2))
    bti = dup(jnp.swapaxes(b_im, 1, 2))
    cr, ci = dup(c_re), dup(c_im)
    gs = GROUPS_PER_STEP
    spec = lambda r, c: pl.BlockSpec((gs, r, c), lambda i: (i, 0, 0))
    vec, mat, big, pws = spec(1, LANES), spec(g, LANES), spec(q * g, LANES), spec(SCAN_PASSES, LANES)
    return pl.pallas_call(
        functools.partial(_ssm_param_kernel, q=q),
        grid=(gr // gs,),
        in_specs=[vec, vec, vec, mat, mat, mat, mat],
        out_specs=[big, big, spec(q * g, q * g), pws, pws],
        out_shape=[jax.ShapeDtypeStruct((gr, q * g, LANES), BF16),
                   jax.ShapeDtypeStruct((gr, q * g, LANES), BF16),
                   jax.ShapeDtypeStruct((gr, q * g, q * g), BF16),
                   jax.ShapeDtypeStruct((gr, SCAN_PASSES, LANES), F32),
                   jax.ShapeDtypeStruct((gr, SCAN_PASSES, LANES), F32)],
        scratch_shapes=[pltpu.VMEM(((q + 1) * g, LANES), F32)],
        compiler_params=_cparams("parallel"),
        name="ssm_params_q%d" % q,
    )(lr, li, ls, btr, bti, cr, ci)


def _ssm_in_kernel(u_ref, win_ref, s_ref):
    for i in range(GROUPS_PER_STEP):
        s_ref[:, i * LANES:(i + 1) * LANES] = _dot(u_ref[i], win_ref[i])


def _ssm_scan_kernel(s_ref, h0_ref, ar_ref, ai_ref, hp_ref, hf_ref, h_scr):
    @pl.when(pl.program_id(0) == 0)
    def _():
        h_scr[...] = h0_ref[...]

    def body(k, h):
        hp_ref[k] = h.astype(BF16)
        return ar_ref[...] * h + ai_ref[...] * pltpu.roll(h, SSM_STATE, 1) + s_ref[k]

    h = lax.fori_loop(0, s_ref.shape[0], body, h_scr[...])
    h_scr[...] = h
    hf_ref[...] = h


def _ssm_out_kernel(u_ref, tt_ref, hp_ref, wout_ref, y_ref):
    for i in range(GROUPS_PER_STEP):
        y_ref[i] = _dot(u_ref[i], tt_ref[i]) + _dot_nt(hp_ref[:, i * LANES:(i + 1) * LANES], wout_ref[i])


def ssm_mix(h_bf16, h0_re, h0_im, params, batch, seq, q):
    win, wout, tt, aqr, aqi = params
    gr, st, g = SSM_GROUPS, SSM_STATE, SSM_GROUP
    nk = seq // q
    n = nk * batch
    qg = q * g
    u = h_bf16.reshape(batch, nk, q, gr, g).transpose(3, 1, 0, 2, 4).reshape(gr, n, qg)
    gs = GROUPS_PER_STEP
    s = pl.pallas_call(
        _ssm_in_kernel,
        grid=(gr // gs,),
        in_specs=[pl.BlockSpec((gs, n, qg), lambda i: (i, 0, 0)),
                  pl.BlockSpec((gs, qg, LANES), lambda i: (i, 0, 0))],
        out_specs=pl.BlockSpec((n, gs * LANES), lambda i: (0, i)),
        out_shape=jax.ShapeDtypeStruct((n, gr * LANES), F32),
        compiler_params=_cparams("parallel"),
        name="ssm_in_q%d" % q,
    )(u, win)
    rows = batch * gr
    kc = min(nk, 32)
    tile = lambda a: jnp.broadcast_to(a[:, 0].reshape(1, gr, LANES), (batch, gr, LANES)).reshape(rows, LANES)
    h0 = jnp.concatenate([h0_re, h0_im], -1).reshape(rows, LANES)
    full = pl.BlockSpec((rows, LANES), lambda i: (0, 0))
    hp, hf = pl.pallas_call(
        _ssm_scan_kernel,
        grid=(nk // kc,),
        in_specs=[pl.BlockSpec((kc, rows, LANES), lambda i: (i, 0, 0)), full, full, full],
        out_specs=[pl.BlockSpec((kc, rows, LANES), lambda i: (i, 0, 0)), full],
        out_shape=[jax.ShapeDtypeStruct((nk, rows, LANES), BF16),
                   jax.ShapeDtypeStruct((rows, LANES), F32)],
        scratch_shapes=[pltpu.VMEM((rows, LANES), F32)],
        compiler_params=_cparams("arbitrary"),
        name="ssm_scan_q%d" % q,
    )(s.reshape(nk, rows, LANES), h0, tile(aqr), tile(aqi))
    y = pl.pallas_call(
        _ssm_out_kernel,
        grid=(gr // gs,),
        in_specs=[pl.BlockSpec((gs, n, qg), lambda i: (i, 0, 0)),
                  pl.BlockSpec((gs, qg, qg), lambda i: (i, 0, 0)),
                  pl.BlockSpec((n, gs * LANES), lambda i: (0, i)),
                  pl.BlockSpec((gs, qg, LANES), lambda i: (i, 0, 0))],
        out_specs=pl.BlockSpec((gs, n, qg), lambda i: (i, 0, 0)),
        out_shape=jax.ShapeDtypeStruct((gr, n, qg), F32),
        compiler_params=_cparams("parallel"),
        name="ssm_out_q%d" % q,
    )(u, tt, hp.reshape(n, gr * LANES), wout)
    y = y.reshape(gr, nk, batch, q, g).transpose(2, 1, 3, 0, 4).reshape(batch * seq, D_MODEL)
    hf = hf.reshape(batch, gr, LANES)
    return y, hf[..., :st], hf[..., st:]


SSM_TOKENS_BLK = 2048


GROUPS_PER_TILE = LANES // SSM_GROUP
N_LANE_TILES = D_MODEL // LANES


def _to_chunks_kernel(*refs, q):
    x_refs, (g_ref, u_ref, ys_scr, ut_scr) = refs[:N_LANE_TILES], refs[N_LANE_TILES:]
    tb = x_refs[0].shape[0]
    nkb = tb // q
    cpb = LANES // q
    ss = jnp.zeros((tb, 1), F32)
    for x_ref in x_refs:
        x = x_ref[...]
        ss = ss + jnp.sum(x * x, -1, keepdims=True)
    rinv = lax.rsqrt(ss * (1.0 / D_MODEL) + RMS_EPS)
    rr = lax.broadcasted_iota(jnp.int32, (LANES, LANES), 0)
    cc = lax.broadcasted_iota(jnp.int32, (LANES, LANES), 1)
    perm = ((rr % cpb) * q + rr // cpb == cc).astype(BF16)
    for c, x_ref in enumerate(x_refs):
        xb = (x_ref[...] * rinv * g_ref[:, c * LANES:(c + 1) * LANES]).astype(BF16)
        for j in range(tb // LANES):
            rows = _dot(perm, xb[j * LANES:(j + 1) * LANES, :])
            ys_scr[:, j * cpb:(j + 1) * cpb, :] = rows.reshape(q, cpb, LANES)
        for s in range(q):
            ut_scr[:, s * SSM_GROUP:(s + 1) * SSM_GROUP, :] = ys_scr[s].T.reshape(GROUPS_PER_TILE, SSM_GROUP, nkb)
        for grp in range(GROUPS_PER_TILE):
            u_ref[c * GROUPS_PER_TILE + grp] = ut_scr[grp].T.astype(BF16)


def _from_chunks_kernel(y_ref, o_ref, yt_scr, ys_scr, *, q):
    tb = o_ref.shape[0]
    nkb = tb // q
    cpb = LANES // q
    rr = lax.broadcasted_iota(jnp.int32, (LANES, LANES), 0)
    cc = lax.broadcasted_iota(jnp.int32, (LANES, LANES), 1)
    perm = ((cc % cpb) * q + cc // cpb == rr).astype(BF16)
    for c in range(N_LANE_TILES):
        for grp in range(GROUPS_PER_TILE):
            yt_scr[grp] = y_ref[c * GROUPS_PER_TILE + grp].T
        for t in range(q):
            ys_scr[t] = yt_scr[:, t * SSM_GROUP:(t + 1) * SSM_GROUP, :].reshape(LANES, nkb).T
        for j in range(tb // LANES):
            rows = ys_scr[:, j * cpb:(j + 1) * cpb, :].reshape(LANES, LANES)
            hi = rows.astype(BF16)
            lo = (rows - hi.astype(F32)).astype(BF16)
            o_ref[j * LANES:(j + 1) * LANES, c * LANES:(c + 1) * LANES] = _dot(perm, hi) + _dot(perm, lo)


def _ssm_group_kernel(u_ref, win_ref, wout_ref, tt_ref, ar_ref, ai_ref, y_ref, hf_ref, h_scr, *, nk):
    u = u_ref[...]
    h = _dot(u, win_ref[...])
    n = h.shape[0]
    k_idx = lax.broadcasted_iota(jnp.int32, (n, LANES), 0) % nk
    d = 1
    for j in range(SCAN_PASSES):
        if d >= nk:
            break
        hs = jnp.where(k_idx >= d, pltpu.roll(h, d, 0), 0.0)
        h = h + ar_ref[j:j + 1, :] * hs + ai_ref[j:j + 1, :] * pltpu.roll(hs, SSM_STATE, 1)
        d *= 2
    hp = jnp.where(k_idx >= 1, pltpu.roll(h, 1, 0), 0.0)
    y_ref[...] = _dot(u, tt_ref[...]) + _dot_nt(hp.astype(BF16), wout_ref[...])
    h_scr[...] = h
    hf_ref[...] = h_scr[pl.ds(nk - 1, n // nk, stride=nk), :]


def ssm_mix_prompt(x, g_mix, params, batch, seq, q):
    win, wout, tt, aqr, aqi = params
    gr, st = SSM_GROUPS, SSM_STATE
    nk = seq // q
    assert nk <= 2 ** SCAN_PASSES
    n = nk * batch
    qg = q * SSM_GROUP
    tb = SSM_TOKENS_BLK
    nkb = tb // q
    per_b = seq // tb
    tok = pl.BlockSpec((tb, D_MODEL), lambda b, i: (b * per_b + i, 0))
    chk = pl.BlockSpec((gr, nkb, qg), lambda b, i: (0, b * per_b + i, 0))
    lane_tiles = [pl.BlockSpec((tb, LANES), lambda b, i, c=c: (b * per_b + i, c)) for c in range(N_LANE_TILES)]
    u = pl.pallas_call(
        functools.partial(_to_chunks_kernel, q=q),
        grid=(batch, per_b),
        in_specs=lane_tiles + [pl.BlockSpec((1, D_MODEL), lambda b, i: (0, 0))],
        out_specs=chk,
        out_shape=jax.ShapeDtypeStruct((gr, n, qg), BF16),
        scratch_shapes=[pltpu.VMEM((q, nkb, LANES), F32), pltpu.VMEM((GROUPS_PER_TILE, qg, nkb), F32)],
        compiler_params=_cparams("parallel", "parallel"),
        name="ssm_to_chunks",
    )(*([x] * N_LANE_TILES), g_mix)
    per_g = lambda r, c: pl.BlockSpec((None, r, c), lambda i: (i, 0, 0))
    y, hf = pl.pallas_call(
        functools.partial(_ssm_group_kernel, nk=nk),
        grid=(gr,),
        in_specs=[per_g(n, qg), per_g(qg, LANES), per_g(qg, LANES), per_g(qg, qg),
                  per_g(SCAN_PASSES, LANES), per_g(SCAN_PASSES, LANES)],
        out_specs=[per_g(n, qg), per_g(batch, LANES)],
        out_shape=[jax.ShapeDtypeStruct((gr, n, qg), F32), jax.ShapeDtypeStruct((gr, batch, LANES), F32)],
        scratch_shapes=[pltpu.VMEM((n, LANES), F32)],
        compiler_params=_cparams("parallel"),
        name="ssm_group",
    )(u, win, wout, tt, aqr, aqi)
    y = pl.pallas_call(
        functools.partial(_from_chunks_kernel, q=q),
        grid=(batch, per_b),
        in_specs=[chk],
        out_specs=tok,
        out_shape=jax.ShapeDtypeStruct((batch * seq, D_MODEL), F32),
        scratch_shapes=[pltpu.VMEM((GROUPS_PER_TILE, qg, nkb), F32), pltpu.VMEM((q, nkb, LANES), F32)],
        compiler_params=_cparams("parallel", "parallel"),
        name="ssm_from_chunks",
    )(y)
    hf = jnp.swapaxes(hf, 0, 1)
    return y, hf[..., :st], hf[..., st:]


def _ssm_glu_kernel(x_ref, gm_ref, y_ref, d_ref, w_ref, b_ref, o_ref):
    x = x_ref[...]
    y = y_ref[...] + d_ref[...] * _rms(x, gm_ref[...])
    gl = _gelu_tanh(y)
    z = _dot(gl.astype(BF16), w_ref[...]) + b_ref[...]
    o_ref[...] = x + gl * _sigmoid(z)


def ssm_glu(x, g_mix, y, d_skip, w_glu, layer, b_glu, tm):
    t = x.shape[0]
    row = pl.BlockSpec((tm, D_MODEL), lambda i: (i, 0))
    vec = pl.BlockSpec((1, D_MODEL), lambda i: (0, 0))
    return pl.pallas_call(
        _ssm_glu_kernel,
        grid=(t // tm,),
        in_specs=[row, vec, row, vec,
                  pl.BlockSpec((None, D_MODEL, D_MODEL), lambda i: (layer, 0, 0), pipeline_mode=pl.Buffered(1)), vec],
        out_specs=row,
        out_shape=jax.ShapeDtypeStruct((t, D_MODEL), F32),
        compiler_params=_cparams("parallel"),
        name="ssm_glu",
    )(x, g_mix, y, d_skip, w_glu, b_glu)


def _row(v):
    return v.reshape(1, -1).astype(F32)


def kernel(x_prompt, x_sample, mem_prompt, state_ssm_re, state_ssm_im, cache_swa0_k, cache_swa0_v, cache_swa1_k, cache_swa1_v, cache_swa2_k, cache_swa2_v, cache_mem_k, cache_mem_v, norm_mix_g, norm_mem_g, norm_memin_g, norm_ffn_g, ssm_lambda_re, ssm_lambda_im, ssm_b_re, ssm_b_im, ssm_c_re, ssm_c_im, ssm_d, ssm_log_step, ssm_w_glu, ssm_b_glu, attn_w_qkv, attn_q_norm_g, attn_k_norm_g, attn_w_o, mem_w_q, mem_w_kv, mem_q_norm_g, mem_k_norm_g, mem_w_o, ffn_w_in, ffn_w_out):
    pb, seq, _ = x_prompt.shape
    sb, dec, _ = x_sample.shape
    n_ssm = state_ssm_re.shape[0]
    n_attn = cache_swa0_k.shape[0]
    tm_p = 512
    tm_s = sb * SAMPLE_ROWS

    w_glu = ssm_w_glu.astype(BF16)
    w_qkv = attn_w_qkv.astype(BF16)
    w_ao = attn_w_o.astype(BF16)
    w_mq = mem_w_q.astype(BF16)
    w_mkv = mem_w_kv.astype(BF16)
    w_mo = mem_w_o.astype(BF16)
    w_fi = ffn_w_in.astype(BF16)
    w_fo = ffn_w_out.astype(BF16)

    head_gain = [jnp.concatenate([jnp.tile(attn_q_norm_g[j], (1, SWA_HEADS)),
                                  jnp.tile(attn_k_norm_g[j], (1, SWA_HEADS))], 1).reshape(N_SWA, 1, 2 * SWA_WIDTH)
                 for j in range(n_attn)]
    mem_q_gain = [_row(jnp.tile(mem_q_norm_g[i], MEM_HEADS)) for i in range(DEPTH)]
    mem_k_gain = jnp.tile(mem_k_norm_g, (1, MEM_HEADS)).reshape(DEPTH, 1, D_MODEL)

    mkv_b, p_mem_k, p_mem_v = mem_kv(mem_prompt.reshape(pb * N_MEM, D_MODEL),
                                     norm_memin_g.reshape(DEPTH, 1, D_MODEL), w_mkv, mem_k_gain)
    p_mem_k = p_mem_k.reshape(DEPTH, pb, N_MEM, MEM_HEADS, MEM_HEAD_DIM)
    p_mem_v = p_mem_v.reshape(DEPTH, pb, N_MEM, MEM_HEADS, MEM_HEAD_DIM)

    ssm_p = [[ssm_params(ssm_lambda_re[j], ssm_lambda_im[j], ssm_log_step[j], ssm_b_re[j], ssm_b_im[j],
                         ssm_c_re[j], ssm_c_im[j], q) for q in (SSM_Q_PROMPT, dec)] for j in range(n_ssm)]

    rows_minor = lambda c: jnp.transpose(c, (0, 1, 3, 4, 2))
    caches_k = tuple(rows_minor(c) for c in (cache_swa0_k, cache_swa1_k, cache_swa2_k))
    caches_v = tuple(rows_minor(c) for c in (cache_swa0_v, cache_swa1_v, cache_swa2_v))
    dils = tuple(d for _, d in SWA_GROUPS)
    assert all(min(w, seq) == SWA_KEYS_BACK * d for w, d in SWA_GROUPS)

    xp = x_prompt.reshape(pb * seq, D_MODEL)
    xs = jnp.pad(x_sample, ((0, 0), (0, SAMPLE_ROWS - dec), (0, 0))).reshape(tm_s, D_MODEL)

    p_ssm_re, p_ssm_im, s_ssm_re, s_ssm_im = [], [], [], []
    p_qkv = [[] for _ in SWA_GROUPS]
    s_swa_k = [[] for _ in SWA_GROUPS]
    s_swa_v = [[] for _ in SWA_GROUPS]

    for i in range(DEPTH):
        j = i // 2
        g_mix = _row(norm_mix_g[i])
        if i % 2 == 0:
            d_skip, b_glu = _row(ssm_d[j]), _row(ssm_b_glu[j])
            y, fr, fi = ssm_mix_prompt(xp, g_mix, ssm_p[j][0], pb, seq, SSM_Q_PROMPT)
            p_ssm_re.append(fr)
            p_ssm_im.append(fi)
            xp = ssm_glu(xp, g_mix, y, d_skip, w_glu, j, b_glu, 2 * tm_p)
            hb = norm_cast(xs, g_mix, tm_s).reshape(sb, SAMPLE_ROWS, D_MODEL)[:, :dec].reshape(sb * dec, D_MODEL)
            y, fr, fi = ssm_mix(hb, state_ssm_re[j], state_ssm_im[j], ssm_p[j][1], sb, dec, dec)
            s_ssm_re.append(fr)
            s_ssm_im.append(fi)
            y = jnp.pad(y.reshape(sb, dec, D_MODEL), ((0, 0), (0, SAMPLE_ROWS - dec), (0, 0))).reshape(tm_s, D_MODEL)
            xs = ssm_glu(xs, g_mix, y, d_skip, w_glu, j, b_glu, tm_s)
        else:
            qkvs = qkv_proj(xp, g_mix, w_qkv, j, head_gain[j], pb, dils, tm_p)
            outs, lses = [], []
            for g in range(N_SWA):
                p_qkv[g].append(qkvs[g])
                o, lse = swa_prompt(qkvs[g], g)
                outs.append(o)
                lses.append(lse)
            xp = attn_out(xp, outs, lses, w_ao, j, pb, 2 * tm_p)
            qkvs = qkv_proj(xs, g_mix, w_qkv, j, head_gain[j], 1, (1,) * N_SWA, tm_s)
            outs, lses = [], []
            for g in range(N_SWA):
                qkv3 = qkvs[g].reshape(sb, SAMPLE_ROWS, 3 * SWA_WIDTH)
                qkv5 = qkv3.reshape(sb, SAMPLE_ROWS, 3, SWA_HEADS, SWA_HEAD_DIM)
                s_swa_k[g].append(qkv5[:, :dec, 1])
                s_swa_v[g].append(qkv5[:, :dec, 2])
                o, lse = swa_sample(qkv3, caches_k[g], caches_v[g], j, g, dec)
                outs.append(o.reshape(1, 1, tm_s, SWA_WIDTH))
                lses.append(lse.reshape(1, 1, tm_s, LANES))
            xs = attn_out(xs, outs, lses, w_ao, j, 1, tm_s)

        g_mem = _row(norm_mem_g[i])
        xp = mem_attn_prompt(xp, g_mem, w_mq, mem_q_gain[i], mkv_b, i, w_mo, pb, 1024)
        xs = mem_attn_sample(xs, g_mem, w_mq, mem_q_gain[i], cache_mem_k, cache_mem_v, i, w_mo, sb)
        g_ffn = _row(norm_ffn_g[i])
        xp = swiglu_block(xp, g_ffn, w_fi, w_fo, i, 1024)
        xs = swiglu_block(xs, g_ffn, w_fi, w_fo, i, tm_s)

    rows_major = lambda c: jnp.transpose(c, (0, 1, 4, 2, 3))
    p_swa = [[rows_major(c) for c in swa_last_rows(p_qkv[g])] for g in range(N_SWA)]
    y_prompt = xp.reshape(pb, seq, D_MODEL)
    y_sample = xs.reshape(sb, SAMPLE_ROWS, D_MODEL)[:, :dec]
    st = lambda a: jnp.stack(a, 0)
    return (y_prompt, y_sample,
            st(p_ssm_re), st(p_ssm_im),
            p_swa[0][0], p_swa[0][1], p_swa[1][0], p_swa[1][1], p_swa[2][0], p_swa[2][1],
            p_mem_k, p_mem_v,
            st(s_ssm_re), st(s_ssm_im),
            st(s_swa_k[0]), st(s_swa_v[0]), st(s_swa_k[1]), st(s_swa_v[1]), st(s_swa_k[2]), st(s_swa_v[2]))
```

```python
import functools
import math

import jax
import jax.numpy as jnp
from jax import lax
from jax.experimental import pallas as pl
from jax.experimental.pallas import tpu as pltpu

F32 = jnp.float32
BF16 = jnp.bfloat16

D_MODEL = 1024
DEPTH = 4
SSM_GROUP = 16
SSM_GROUPS = D_MODEL // SSM_GROUP
SSM_STATE = 64
SWA_GROUPS = ((128, 1), (512, 4), (2048, 16))
N_SWA = len(SWA_GROUPS)
SWA_HEADS = 8
SWA_HEAD_DIM = 64
SWA_WIDTH = SWA_HEADS * SWA_HEAD_DIM
QKV_WIDTH = 3 * N_SWA * SWA_WIDTH
SWA_KEYS_BACK = 128
N_MEM = 256
MEM_HEADS = 4
MEM_HEAD_DIM = D_MODEL // MEM_HEADS
D_FF = -(-8 * D_MODEL // (3 * 256)) * 256
RMS_EPS = 1e-6
NEG_BIG = -1e30

LANES = 128
SUBLANES = 8
VMEM_LIMIT = 48 * 1024 * 1024
Q_BLK = 128
FF_BLK = 256
SAMPLE_ROWS = SUBLANES
SSM_Q_PROMPT = 16


def _cparams(*sem):
    return pltpu.CompilerParams(dimension_semantics=sem, vmem_limit_bytes=VMEM_LIMIT)


def _dot(a, b):
    return jnp.dot(a, b, preferred_element_type=F32)


def _dot_nt(a, b, precision=None):
    return lax.dot_general(a, b, (((1,), (1,)), ((), ())), preferred_element_type=F32,
                           precision=precision)


def _rms(x, g):
    return x * lax.rsqrt(jnp.mean(x * x, -1, keepdims=True) + RMS_EPS) * g


def _headnorm(y, g_row, hd):
    tm, n = y.shape
    outs = []
    if hd % LANES == 0:
        for c in range(n // hd):
            yc = y[:, c * hd:(c + 1) * hd]
            ms = jnp.mean(yc * yc, -1, keepdims=True)
            outs.append(yc * lax.rsqrt(ms + RMS_EPS))
    else:
        assert 2 * hd == LANES
        lo = lax.broadcasted_iota(jnp.int32, (tm, LANES), 1) < hd
        for c in range(n // LANES):
            yc = y[:, c * LANES:(c + 1) * LANES]
            sq = yc * yc
            s_lo = jnp.sum(jnp.where(lo, sq, 0.0), -1, keepdims=True)
            s_hi = jnp.sum(jnp.where(lo, 0.0, sq), -1, keepdims=True)
            ms = jnp.where(lo, s_lo, s_hi) * (1.0 / hd)
            outs.append(yc * lax.rsqrt(ms + RMS_EPS))
    return jnp.concatenate(outs, -1) * g_row


def _sigmoid(x):
    return 1.0 / (1.0 + jnp.exp(-x))


def _gelu_tanh(x):
    return 0.5 * x * (1.0 + jnp.tanh(math.sqrt(2.0 / math.pi) * (x + 0.044715 * (x * x * x))))


def _norm_cast_kernel(x_ref, g_ref, o_ref):
    o_ref[...] = _rms(x_ref[...], g_ref[...]).astype(BF16)


def norm_cast(x, g, tm):
    t = x.shape[0]
    return pl.pallas_call(
        _norm_cast_kernel,
        grid=(t // tm,),
        in_specs=[pl.BlockSpec((tm, D_MODEL), lambda i: (i, 0)),
                  pl.BlockSpec((1, D_MODEL), lambda i: (0, 0))],
        out_specs=pl.BlockSpec((tm, D_MODEL), lambda i: (i, 0)),
        out_shape=jax.ShapeDtypeStruct((t, D_MODEL), BF16),
        compiler_params=_cparams("parallel"),
        name="norm_cast",
    )(x, g)


def _qkv_kernel(x_ref, g_ref, w_ref, hg_ref, o0_ref, o1_ref, o2_ref, y_ref, *, dils):
    tm = x_ref.shape[0]
    w = SWA_WIDTH
    xn = _rms(x_ref[...], g_ref[...]).astype(BF16)
    for grp, (o_ref, dil) in enumerate(zip((o0_ref, o1_ref, o2_ref), dils)):
        col = lambda kind: slice((kind * N_SWA + grp) * w, (kind * N_SWA + grp + 1) * w)
        parts = (_headnorm(_dot(xn, w_ref[:, col(0)]), hg_ref[grp, :, :w], SWA_HEAD_DIM),
                 _headnorm(_dot(xn, w_ref[:, col(1)]), hg_ref[grp, :, w:], SWA_HEAD_DIM),
                 _dot(xn, w_ref[:, col(2)]))
        if dil == 1:
            for kind, y in enumerate(parts):
                o_ref[0, :, kind * w:(kind + 1) * w] = y
        else:
            per = w // LANES
            for c in range(y_ref.shape[0]):
                y_ref[c] = parts[c // per][:, (c % per) * LANES:(c % per + 1) * LANES]
                for r in range(dil):
                    o_ref[r, :, c * LANES:(c + 1) * LANES] = y_ref[c, pl.ds(r, tm // dil, stride=dil), :]


def qkv_proj(x, g, w, layer, head_gain, batch, dils, tm):
    t = x.shape[0]
    seq = t // batch
    per_b = seq // tm
    gw = 3 * SWA_WIDTH
    return pl.pallas_call(
        functools.partial(_qkv_kernel, dils=dils),
        grid=(t // tm,),
        in_specs=[pl.BlockSpec((tm, D_MODEL), lambda i: (i, 0)),
                  pl.BlockSpec((1, D_MODEL), lambda i: (0, 0)),
                  pl.BlockSpec((None, D_MODEL, QKV_WIDTH), lambda i: (layer, 0, 0), pipeline_mode=pl.Buffered(1)),
                  pl.BlockSpec((N_SWA, 1, 2 * SWA_WIDTH), lambda i: (0, 0, 0))],
        out_specs=[pl.BlockSpec((None, d, tm // d, gw), lambda i: (i // per_b, 0, i % per_b, 0))
                   for d in dils],
        out_shape=[jax.ShapeDtypeStruct((batch, d, seq // d, gw), F32) for d in dils],
        scratch_shapes=[pltpu.VMEM((gw // LANES, tm, LANES), F32)],
        compiler_params=_cparams("parallel"),
        name="qkv_proj",
    )(x, g, w, head_gain)


def _mem_kv_kernel(x_ref, g_ref, w_ref, kg_ref, o_ref, ko_ref, vo_ref):
    j = pl.program_id(1)
    y = _dot(_rms(x_ref[...], g_ref[...]).astype(BF16), w_ref[...])

    def emit(val, by_head_ref):
        o_ref[...] = val
        for h in range(MEM_HEADS):
            by_head_ref[:, h, :] = val[:, h * MEM_HEAD_DIM:(h + 1) * MEM_HEAD_DIM]

    @pl.when(j == 0)
    def _():
        emit(_headnorm(y, kg_ref[...], MEM_HEAD_DIM), ko_ref)

    @pl.when(j == 1)
    def _():
        emit(y, vo_ref)


def mem_kv(mem, g_in, w_kv, k_gain):
    t = mem.shape[0]
    by_head = pl.BlockSpec((None, t, MEM_HEADS, MEM_HEAD_DIM), lambda l, j: (l, 0, 0, 0))
    by_head_shape = jax.ShapeDtypeStruct((DEPTH, t, MEM_HEADS, MEM_HEAD_DIM), F32)
    return pl.pallas_call(
        _mem_kv_kernel,
        grid=(DEPTH, 2),
        in_specs=[pl.BlockSpec((t, D_MODEL), lambda l, j: (0, 0)),
                  pl.BlockSpec((None, 1, D_MODEL), lambda l, j: (l, 0, 0)),
                  pl.BlockSpec((None, D_MODEL, D_MODEL), lambda l, j: (l, 0, j)),
                  pl.BlockSpec((None, 1, D_MODEL), lambda l, j: (l, 0, 0))],
        out_specs=[pl.BlockSpec((None, t, D_MODEL), lambda l, j: (l, 0, j)), by_head, by_head],
        out_shape=[jax.ShapeDtypeStruct((DEPTH, t, 2 * D_MODEL), F32), by_head_shape, by_head_shape],
        compiler_params=_cparams("parallel", "arbitrary"),
        name="mem_kv",
    )(mem, g_in, w_kv, k_gain)


def _swiglu_kernel(x_ref, g_ref, wi_ref, wo_ref, o_ref):
    x = x_ref[...]
    xn = _rms(x, g_ref[...]).astype(BF16)
    acc = x
    for lo in range(0, D_FF, FF_BLK):
        hi = min(lo + FF_BLK, D_FF)
        gate = _dot(xn, wi_ref[:, lo:hi])
        up = _dot(xn, wi_ref[:, D_FF + lo:D_FF + hi])
        act = gate * _sigmoid(gate) * up
        acc = acc + _dot(act.astype(BF16), wo_ref[lo:hi, :])
    o_ref[...] = acc


def swiglu_block(x, g, w_in, w_out, layer, tm):
    t = x.shape[0]
    once = pl.Buffered(1)
    return pl.pallas_call(
        _swiglu_kernel,
        grid=(t // tm,),
        in_specs=[pl.BlockSpec((tm, D_MODEL), lambda i: (i, 0)),
                  pl.BlockSpec((1, D_MODEL), lambda i: (0, 0)),
                  pl.BlockSpec((None, D_MODEL, 2 * D_FF), lambda i: (layer, 0, 0), pipeline_mode=once),
                  pl.BlockSpec((None, D_FF, D_MODEL), lambda i: (layer, 0, 0), pipeline_mode=once)],
        out_specs=pl.BlockSpec((tm, D_MODEL), lambda i: (i, 0)),
        out_shape=jax.ShapeDtypeStruct((t, D_MODEL), F32),
        compiler_params=_cparams("parallel"),
        name="swiglu",
    )(x, g, w_in, w_out)


def _mem_heads(q, key_head, value_head):
    outs = []
    for h in range(MEM_HEADS):
        sl = slice(h * MEM_HEAD_DIM, (h + 1) * MEM_HEAD_DIM)
        s = _dot_nt(q[:, sl].astype(BF16), key_head(h))
        m = jnp.max(s, -1, keepdims=True)
        p = jnp.exp(s - m)
        l = jnp.sum(p, -1, keepdims=True)
        outs.append(_dot(p.astype(BF16), value_head(h)) / l)
    return jnp.concatenate(outs, -1)


def _mem_attn_value(x, g_ref, wq_ref, qg_ref, mk_ref, mv_ref, wo_ref):
    q = _dot(_rms(x, g_ref[...]).astype(BF16), wq_ref[...])
    q = _headnorm(q, qg_ref[...], MEM_HEAD_DIM) * (MEM_HEAD_DIM ** -0.5)
    head = lambda ref: lambda h: ref[:, h * MEM_HEAD_DIM:(h + 1) * MEM_HEAD_DIM].astype(BF16)
    o = _mem_heads(q, head(mk_ref), head(mv_ref))
    return x + _dot(o.astype(BF16), wo_ref[...])


def _mem_specs(layer):
    wspec = pl.BlockSpec((None, D_MODEL, D_MODEL), lambda b, i: (layer, 0, 0), pipeline_mode=pl.Buffered(1))
    vec = pl.BlockSpec((1, D_MODEL), lambda b, i: (0, 0))
    return [vec, wspec, vec,
            pl.BlockSpec((None, N_MEM, D_MODEL), lambda b, i: (layer, b, 0)),
            pl.BlockSpec((None, N_MEM, D_MODEL), lambda b, i: (layer, b, 1)),
            wspec]


def _mem_attn_sample_kernel(x_ref, g_ref, wq_ref, qg_ref, mk_ref, mv_ref, wo_ref, o_ref,
                            q_scr, o_scr):
    b = pl.program_id(0)

    @pl.when(b == 0)
    def _():
        q = _dot(_rms(x_ref[...], g_ref[...]).astype(BF16), wq_ref[...])
        q_scr[...] = _headnorm(q, qg_ref[...], MEM_HEAD_DIM) * (MEM_HEAD_DIM ** -0.5)

    hd, nr, nk = MEM_HEAD_DIM, MEM_HEADS * SAMPLE_ROWS, N_MEM * MEM_HEADS
    own = (lax.broadcasted_iota(jnp.int32, (nr, nk), 1) % MEM_HEADS
           == lax.broadcasted_iota(jnp.int32, (nr, nk), 0) // SAMPLE_ROWS)
    for bi in range(mk_ref.shape[0]):
        rows = pl.ds(pl.multiple_of((b * mk_ref.shape[0] + bi) * SAMPLE_ROWS, SAMPLE_ROWS), SAMPLE_ROWS)
        q = q_scr[rows, :]
        qx = jnp.concatenate([q[:, h * hd:(h + 1) * hd] for h in range(MEM_HEADS)], 0).astype(BF16)
        s = jnp.where(own, _dot_nt(qx, mk_ref[bi].reshape(nk, hd).astype(BF16)), NEG_BIG)
        p = jnp.exp(s - jnp.max(s, -1, keepdims=True))
        o = _dot(p.astype(BF16), mv_ref[bi].reshape(nk, hd).astype(BF16)) / jnp.sum(p, -1, keepdims=True)
        o_scr[rows, :] = jnp.concatenate([o[h * SAMPLE_ROWS:(h + 1) * SAMPLE_ROWS] for h in range(MEM_HEADS)], -1)

    @pl.when(b == pl.num_programs(0) - 1)
    def _():
        o_ref[...] = x_ref[...] + _dot(o_scr[...].astype(BF16), wo_ref[...])


def mem_attn_sample(x, g, w_q, q_gain, cache_k, cache_v, layer, w_o, batch):
    t = x.shape[0]
    const = lambda b: (0, 0)
    wspec = pl.BlockSpec((None, D_MODEL, D_MODEL), lambda b: (layer, 0, 0))
    per_step = 4
    cspec = pl.BlockSpec((None, per_step, N_MEM, MEM_HEADS, MEM_HEAD_DIM), lambda b: (layer, b, 0, 0, 0))
    return pl.pallas_call(
        _mem_attn_sample_kernel,
        grid=(batch // per_step,),
        in_specs=[pl.BlockSpec((t, D_MODEL), const),
                  pl.BlockSpec((1, D_MODEL), const),
                  wspec,
                  pl.BlockSpec((1, D_MODEL), const),
                  cspec, cspec, wspec],
        out_specs=pl.BlockSpec((t, D_MODEL), const),
        out_shape=jax.ShapeDtypeStruct((t, D_MODEL), F32),
        scratch_shapes=[pltpu.VMEM((t, D_MODEL), F32), pltpu.VMEM((t, D_MODEL), F32)],
        compiler_params=_cparams("arbitrary"),
        name="mem_attn_sample",
    )(x, g, w_q, q_gain, cache_k, cache_v, w_o)


MAX_Q_SUB = 4


def _swa_prompt_kernel(q_ref, kp_ref, kc_ref, vp_ref, vc_ref, o_ref, l_ref):
    for res in range(q_ref.shape[0]):
        _swa_prompt_rows(*[r.at[res] for r in (q_ref, kp_ref, kc_ref, vp_ref, vc_ref, o_ref, l_ref)])


def _swa_prompt_rows(q_ref, kp_ref, kc_ref, vp_ref, vc_ref, o_ref, l_ref):
    i = pl.program_id(2)
    q = (q_ref[...] * (SWA_HEAD_DIM ** -0.5)).astype(BF16)
    k = jnp.concatenate([kp_ref[...], kc_ref[...]], 0).astype(BF16)
    v = jnp.concatenate([vp_ref[...], vc_ref[...]], 0).astype(BF16)
    qq = lax.broadcasted_iota(jnp.int32, (Q_BLK, 2 * Q_BLK), 0)
    kk = lax.broadcasted_iota(jnp.int32, (Q_BLK, 2 * Q_BLK), 1)
    band = (kk >= qq) & (kk <= qq + SWA_KEYS_BACK)
    head_lane = lax.broadcasted_iota(jnp.int32, (Q_BLK, LANES), 1)
    head_of_lane = head_lane // SWA_HEAD_DIM
    pair_ids = range(LANES // SWA_HEAD_DIM)
    for sub in range(q_ref.shape[0] // Q_BLK):
        rows = slice(sub * Q_BLK, (sub + 1) * Q_BLK)
        win = slice(sub * Q_BLK, (sub + 2) * Q_BLK)
        valid = band if sub else band & ((kk >= Q_BLK) | (i > 0))
        outs = []
        lse = jnp.zeros((Q_BLK, LANES), F32)
        for c in range(SWA_WIDTH // LANES):
            tile = slice(c * LANES, (c + 1) * LANES)
            ql, kl, vl = q[rows, tile], k[win, tile], v[win, tile]
            pair = []
            for e in pair_ids:
                qh = jnp.where(head_of_lane == e, ql, jnp.zeros_like(ql))
                s = jnp.where(valid, _dot_nt(qh, kl), NEG_BIG)
                m = jnp.max(s, -1, keepdims=True)
                p = jnp.exp(s - m)
                l = jnp.sum(p, -1, keepdims=True)
                pair.append(_dot(p.astype(BF16), vl) / l)
                lse = jnp.where(head_lane == c * len(pair_ids) + e, m + jnp.log(l), lse)
            outs.append(jnp.where(head_of_lane == 0, pair[0], pair[1]))
        o_ref[rows, :] = jnp.concatenate(outs, -1)
        l_ref[rows, :] = lse


def swa_prompt(qkv, group):
    batch, dil, rows, _ = qkv.shape
    q_sub = min(MAX_Q_SUB, rows // Q_BLK)
    n_res = min(dil, MAX_Q_SUB // q_sub)
    blk = (None, n_res, q_sub * Q_BLK, SWA_WIDTH)
    pblk = (None, n_res, Q_BLK, SWA_WIDTH)
    prev = lambda i: jnp.maximum(i * q_sub - 1, 0)
    q_spec = pl.BlockSpec(blk, lambda b, r, i: (b, r, i, 0))
    kp_spec = pl.BlockSpec(pblk, lambda b, r, i: (b, r, prev(i), 1))
    kc_spec = pl.BlockSpec(blk, lambda b, r, i: (b, r, i, 1))
    vp_spec = pl.BlockSpec(pblk, lambda b, r, i: (b, r, prev(i), 2))
    vc_spec = pl.BlockSpec(blk, lambda b, r, i: (b, r, i, 2))
    l_spec = pl.BlockSpec((None, n_res, q_sub * Q_BLK, LANES), lambda b, r, i: (b, r, i, 0))
    return pl.pallas_call(
        _swa_prompt_kernel,
        grid=(batch, dil // n_res, rows // (q_sub * Q_BLK)),
        in_specs=[q_spec, kp_spec, kc_spec, vp_spec, vc_spec],
        out_specs=[q_spec, l_spec],
        out_shape=[jax.ShapeDtypeStruct((batch, dil, rows, SWA_WIDTH), F32),
                   jax.ShapeDtypeStruct((batch, dil, rows, LANES), F32)],
        compiler_params=_cparams("parallel", "parallel", "arbitrary"),
        name="swa_prompt_g%d" % group,
    )(qkv, qkv, qkv, qkv, qkv)


def _swa_rows_kernel(*refs, dil, n_layers):
    srcs, (ko_ref, vo_ref, buf) = refs[:2 * n_layers], refs[2 * n_layers:]
    layer = pl.program_id(0)
    pair = LANES // SWA_HEAD_DIM
    for li in range(n_layers):
        @pl.when(layer == li)
        def _(li=li):
            for src, dst in ((srcs[2 * li], ko_ref), (srcs[2 * li + 1], vo_ref)):
                for c in range(SWA_WIDTH // LANES):
                    for r in range(dil):
                        buf[pl.ds(r, src.shape[1], stride=dil), :] = src[r, :, c * LANES:(c + 1) * LANES]
                    dst[c * pair:(c + 1) * pair] = buf[...].T.reshape(pair, SWA_HEAD_DIM, buf.shape[0])


def swa_last_rows(qkvs):
    n_layers = len(qkvs)
    batch, dil, rows, _ = qkvs[0].shape
    keep = SWA_KEYS_BACK * dil
    pos = min(SWA_KEYS_BACK, 1024 // dil)
    nb = SWA_KEYS_BACK // pos
    first = (rows - SWA_KEYS_BACK) // pos
    in_specs, args = [], []
    for qkv in qkvs:
        for kind in (1, 2):
            in_specs.append(pl.BlockSpec((None, dil, pos, SWA_WIDTH),
                                         lambda l, b, a, kind=kind: (b, 0, first + a, kind)))
            args.append(qkv)
    o_spec = pl.BlockSpec((None, None, SWA_HEADS, SWA_HEAD_DIM, pos * dil), lambda l, b, a: (l, b, 0, 0, a))
    shape = jax.ShapeDtypeStruct((n_layers, batch, SWA_HEADS, SWA_HEAD_DIM, keep), F32)
    return pl.pallas_call(
        functools.partial(_swa_rows_kernel, dil=dil, n_layers=n_layers),
        grid=(n_layers, batch, nb),
        in_specs=in_specs,
        out_specs=[o_spec, o_spec],
        out_shape=[shape, shape],
        scratch_shapes=[pltpu.VMEM((pos * dil, LANES), F32)],
        compiler_params=_cparams("parallel", "parallel", "parallel"),
        name="swa_last_rows_d%d" % dil,
    )(*args)


def _swa_sample_kernel(q_ref, kn_ref, vn_ref, kt_ref, vt_ref, o_ref, l_ref, bias_c, bias_n, *, dil, n_new):
    nq, nh, hd = SAMPLE_ROWS, SWA_HEADS, SWA_HEAD_DIM
    nc = nh * nq
    rows = kt_ref.shape[-1]

    @pl.when(pl.program_id(0) == 0)
    def _():
        def bias(n, valid):
            i = lax.broadcasted_iota(jnp.int32, (nc, n), 0) % nq
            return jnp.where(valid(i, lax.broadcasted_iota(jnp.int32, (nc, n), 1)), 0.0, NEG_BIG)
        bias_c[...] = bias(rows, lambda i, c: (c >= i) & ((c - i) % dil == 0))
        bias_n[...] = bias(nq, lambda i, j: (j < n_new) & (j <= i) & ((i - j) % dil == 0))

    own = (lax.broadcasted_iota(jnp.int32, (nc, SWA_WIDTH), 1) // hd
           == lax.broadcasted_iota(jnp.int32, (nc, SWA_WIDTH), 0) // nq)
    head_lane = lax.broadcasted_iota(jnp.int32, (nq, LANES), 1)
    for b in range(q_ref.shape[0]):
        q = q_ref[b] * (hd ** -0.5)
        qbd = jnp.where(own, jnp.concatenate([q] * nh, 0), 0.0).astype(BF16)
        s_c = _dot(qbd, kt_ref[b].reshape(SWA_WIDTH, rows).astype(BF16)) + bias_c[...]
        s_n = _dot_nt(qbd, kn_ref[b].astype(BF16)) + bias_n[...]
        m = jnp.maximum(jnp.max(s_c, -1, keepdims=True), jnp.max(s_n, -1, keepdims=True))
        p_c, p_n = jnp.exp(s_c - m), jnp.exp(s_n - m)
        l = jnp.sum(p_c, -1, keepdims=True) + jnp.sum(p_n, -1, keepdims=True)
        acc = (_dot_nt(p_c.astype(BF16), vt_ref[b].reshape(SWA_WIDTH, rows).astype(BF16))
               + _dot(p_n.astype(BF16), vn_ref[b].astype(BF16)))
        o = acc / l
        o_ref[b] = jnp.concatenate([o[h * nq:(h + 1) * nq, h * hd:(h + 1) * hd] for h in range(nh)], -1)
        lse_col = m + jnp.log(l)
        lse = jnp.zeros((nq, LANES), F32)
        for h in range(nh):
            lse = jnp.where(head_lane == h, lse_col[h * nq:(h + 1) * nq], lse)
        l_ref[b] = lse


def swa_sample(qkv, cache_kt, cache_vt, layer, group, n_new):
    batch = qkv.shape[0]
    _, _, heads, hd, rows = cache_kt.shape
    dil = SWA_GROUPS[group][1]
    per_step = max(1, min(8, 2048 // rows))
    blk = (per_step, SAMPLE_ROWS, SWA_WIDTH)
    c_spec = pl.BlockSpec((None, per_step, heads, hd, rows), lambda b: (layer, b, 0, 0, 0))
    shape = jax.ShapeDtypeStruct((batch, SAMPLE_ROWS, SWA_WIDTH), F32)
    nc = SAMPLE_ROWS * heads
    return pl.pallas_call(
        functools.partial(_swa_sample_kernel, dil=dil, n_new=n_new),
        grid=(batch // per_step,),
        in_specs=[pl.BlockSpec(blk, lambda b: (b, 0, 0)),
                  pl.BlockSpec(blk, lambda b: (b, 0, 1)),
                  pl.BlockSpec(blk, lambda b: (b, 0, 2)),
                  c_spec, c_spec],
        out_specs=[pl.BlockSpec(blk, lambda b: (b, 0, 0)),
                   pl.BlockSpec((per_step, SAMPLE_ROWS, LANES), lambda b: (b, 0, 0))],
        out_shape=[shape, jax.ShapeDtypeStruct((batch, SAMPLE_ROWS, LANES), F32)],
        scratch_shapes=[pltpu.VMEM((nc, rows), F32), pltpu.VMEM((nc, SAMPLE_ROWS), F32)],
        compiler_params=_cparams("arbitrary"),
        name="swa_sample_g%d" % group,
    )(qkv, qkv, qkv, cache_kt, cache_vt)


def _attn_out_kernel(x_ref, o0, o1, o2, l0, l1, l2, w_ref, out_ref, *scr, dils):
    out_ref[...] = _attn_out_value(x_ref, o0, o1, o2, l0, l1, l2, w_ref, scr, dils)


def _attn_mem_kernel(x_ref, o0, o1, o2, l0, l1, l2, w_ref, g_ref, wq_ref, qg_ref, mk_ref, mv_ref, wo_ref,
                     out_ref, *scr, dils):
    x = _attn_out_value(x_ref, o0, o1, o2, l0, l1, l2, w_ref, scr, dils)
    out_ref[...] = _mem_attn_value(x, g_ref, wq_ref, qg_ref, mk_ref, mv_ref, wo_ref)


def _attn_out_value(x_ref, o0, o1, o2, l0, l1, l2, w_ref, scr, dils):
    tm = x_ref.shape[0]
    scr = list(scr)

    def rows(ref, dil):
        if dil == 1:
            return ref[0]
        buf = scr.pop()
        for c in range(buf.shape[0]):
            for r in range(dil):
                buf[c, pl.ds(r, tm // dil, stride=dil), :] = ref[r, :, c * LANES:(c + 1) * LANES]
        return jnp.concatenate([buf[c] for c in range(buf.shape[0])], -1)

    a, b, c = rows(l0, dils[0]), rows(l1, dils[1]), rows(l2, dils[2])
    m = jnp.maximum(jnp.maximum(a, b), c)
    ea, eb, ec = jnp.exp(a - m), jnp.exp(b - m), jnp.exp(c - m)
    den = ea + eb + ec
    spread = (lax.broadcasted_iota(jnp.int32, (LANES, SWA_WIDTH), 1) // SWA_HEAD_DIM
              == lax.broadcasted_iota(jnp.int32, (LANES, SWA_WIDTH), 0)).astype(BF16)

    def per_lane(wgt):
        hi = wgt.astype(BF16)
        lo = (wgt - hi.astype(F32)).astype(BF16)
        return _dot(hi, spread) + _dot(lo, spread)

    o = (per_lane(ea / den) * rows(o0, dils[0]) + per_lane(eb / den) * rows(o1, dils[1])
         + per_lane(ec / den) * rows(o2, dils[2]))
    return x_ref[...] + _dot(o.astype(BF16), w_ref[...])


def attn_mem_prompt(x, outs, lses, w_ao, mem_args, layer_attn, layer, batch, tm):
    t = x.shape[0]
    per_b = t // batch // tm
    g, w_q, q_gain, mkv, w_o = mem_args
    dils = tuple(o.shape[1] for o in outs)
    spec = lambda d, w: pl.BlockSpec((None, d, tm // d, w), lambda b, i: (b, 0, i, 0))
    tok = pl.BlockSpec((tm, D_MODEL), lambda b, i: (b * per_b + i, 0))
    n_scr = sum(d > 1 for d in dils)
    return pl.pallas_call(
        functools.partial(_attn_mem_kernel, dils=dils),
        grid=(batch, per_b),
        in_specs=[tok] + [spec(d, SWA_WIDTH) for d in dils] + [spec(d, LANES) for d in dils]
                 + [pl.BlockSpec((None, SWA_WIDTH, D_MODEL), lambda b, i: (layer_attn, 0, 0), pipeline_mode=pl.Buffered(1))]
                 + _mem_specs(layer),
        out_specs=tok,
        out_shape=jax.ShapeDtypeStruct((t, D_MODEL), F32),
        scratch_shapes=[pltpu.VMEM((SWA_WIDTH // LANES, tm, LANES), F32)] * n_scr
                       + [pltpu.VMEM((1, tm, LANES), F32)] * n_scr,
        compiler_params=_cparams("parallel", "arbitrary"),
        name="attn_mem_prompt",
    )(x, *outs, *lses, w_ao, g, w_q, q_gain, mkv, mkv, w_o)


def attn_out(x, outs, lses, w_o, layer, batch, tm):
    t = x.shape[0]
    per_b = t // batch // tm
    dils = tuple(o.shape[1] for o in outs)
    spec = lambda d, w: pl.BlockSpec((None, d, tm // d, w), lambda i: (i // per_b, 0, i % per_b, 0))
    n_scr = sum(d > 1 for d in dils)
    return pl.pallas_call(
        functools.partial(_attn_out_kernel, dils=dils),
        grid=(t // tm,),
        in_specs=[pl.BlockSpec((tm, D_MODEL), lambda i: (i, 0))]
                 + [spec(d, SWA_WIDTH) for d in dils] + [spec(d, LANES) for d in dils]
                 + [pl.BlockSpec((None, SWA_WIDTH, D_MODEL), lambda i: (layer, 0, 0), pipeline_mode=pl.Buffered(1))],
        out_specs=pl.BlockSpec((tm, D_MODEL), lambda i: (i, 0)),
        out_shape=jax.ShapeDtypeStruct((t, D_MODEL), F32),
        scratch_shapes=[pltpu.VMEM((SWA_WIDTH // LANES, tm, LANES), F32)] * n_scr
                       + [pltpu.VMEM((1, tm, LANES), F32)] * n_scr,
        compiler_params=_cparams("parallel"),
        name="attn_out",
    )(x, *outs, *lses, w_o)


SCAN_PASSES = 8
GROUPS_PER_STEP = 8


def _ssm_param_kernel(*refs, q):
    *per_group, e_scr = refs
    for i in range(GROUPS_PER_STEP):
        _ssm_param_group(*[r.at[i] for r in per_group], e_scr, q=q)


def _ssm_param_group(lr_ref, li_ref, ls_ref, btr_ref, bti_ref, cr_ref, ci_ref,
                     win_ref, wout_ref, tt_ref, aqr_ref, aqi_ref, e_scr, *, q):
    g = SSM_GROUP
    lr, li = lr_ref[...], li_ref[...]
    step = jnp.exp(ls_ref[...])
    lo = lax.broadcasted_iota(jnp.int32, (1, LANES), 1) < SSM_STATE

    def powers(tau):
        mag = jnp.exp(tau * (lr * step))
        ang = tau * (li * step)
        return mag * jnp.cos(ang), mag * jnp.sin(ang)

    tau = lax.broadcasted_iota(jnp.int32, (q + 1, LANES), 0).astype(F32)
    c2, s2 = powers(tau)
    ar, ai = c2[1:2], s2[1:2]
    den = lr * lr + li * li
    fr = ((ar - 1.0) * lr + ai * li) / den
    fi = (ai * lr - (ar - 1.0) * li) / den
    btr, bti = btr_ref[...], bti_ref[...]
    bbr = fr * btr - fi * bti
    bbi = fr * bti + fi * btr
    bba = jnp.where(lo, bbr, bbi)
    bbb = jnp.where(lo, -bbi, bbr)
    pa = jnp.where(lo, c2, -s2)
    pb = jnp.where(lo, s2, c2)
    cr, ci = cr_ref[...], ci_ref[...]
    for t in range(q + 1):
        e_scr[t * g:(t + 1) * g, :] = cr * pa[t:t + 1] - ci * pb[t:t + 1]
    wout_ref[...] = e_scr[g:(q + 1) * g, :].astype(BF16)
    kt = _dot_nt(bba, e_scr[0:q * g, :], precision=lax.Precision.HIGHEST)
    for s in range(q):
        blk = kt if s == 0 else jnp.concatenate([jnp.zeros((g, g * s), F32), kt[:, :(q - s) * g]], 1)
        tt_ref[s * g:(s + 1) * g, :] = blk.astype(BF16)
    c2r, s2r = powers((q - 1.0) - tau[0:q])
    for s in range(q):
        win_ref[s * g:(s + 1) * g, :] = (c2r[s:s + 1] * bba + s2r[s:s + 1] * bbb).astype(BF16)
    pw = lax.broadcasted_iota(jnp.int32, (SCAN_PASSES, LANES), 0)
    cq, sq = powers((q * jnp.left_shift(1, pw)).astype(F32))
    aqr_ref[...] = cq
    aqi_ref[...] = jnp.where(lo, -sq, sq)


def ssm_params(lam_re, lam_im, log_step, b_re, b_im, c_re, c_im, q):
    gr, st, g = SSM_GROUPS, SSM_STATE, SSM_GROUP
    dup = lambda a: jnp.concatenate([a, a], -1)
    lr = dup(lam_re).reshape(gr, 1, LANES)
    li = dup(lam_im).reshape(gr, 1, LANES)
    ls = jnp.broadcast_to(log_step[:, None, None], (gr, 1, LANES))
    btr = dup(jnp.swapaxes(b_re, 1, 2))
    bti = dup(jnp.swapaxes(b_im, 1, 2))
    cr, ci = dup(c_re), dup(c_im)
    gs = GROUPS_PER_STEP
    spec = lambda r, c: pl.BlockSpec((gs, r, c), lambda i: (i, 0, 0))
    vec, mat, big, pws = spec(1, LANES), spec(g, LANES), spec(q * g, LANES), spec(SCAN_PASSES, LANES)
    return pl.pallas_call(
        functools.partial(_ssm_param_kernel, q=q),
        grid=(gr // gs,),
        in_specs=[vec, vec, vec, mat, mat, mat, mat],
        out_specs=[big, big, spec(q * g, q * g), pws, pws],
        out_shape=[jax.ShapeDtypeStruct((gr, q * g, LANES), BF16),
                   jax.ShapeDtypeStruct((gr, q * g, LANES), BF16),
                   jax.ShapeDtypeStruct((gr, q * g, q * g), BF16),
                   jax.ShapeDtypeStruct((gr, SCAN_PASSES, LANES), F32),
                   jax.ShapeDtypeStruct((gr, SCAN_PASSES, LANES), F32)],
        scratch_shapes=[pltpu.VMEM(((q + 1) * g, LANES), F32)],
        compiler_params=_cparams("parallel"),
        name="ssm_params_q%d" % q,
    )(lr, li, ls, btr, bti, cr, ci)


def _ssm_in_kernel(u_ref, win_ref, s_ref):
    for i in range(GROUPS_PER_STEP):
        s_ref[:, i * LANES:(i + 1) * LANES] = _dot(u_ref[i], win_ref[i])


def _ssm_scan_kernel(s_ref, h0_ref, ar_ref, ai_ref, hp_ref, hf_ref, h_scr):
    @pl.when(pl.program_id(0) == 0)
    def _():
        h_scr[...] = h0_ref[...]

    def body(k, h):
        hp_ref[k] = h.astype(BF16)
        return ar_ref[...] * h + ai_ref[...] * pltpu.roll(h, SSM_STATE, 1) + s_ref[k]

    h = lax.fori_loop(0, s_ref.shape[0], body, h_scr[...])
    h_scr[...] = h
    hf_ref[...] = h


def _ssm_out_kernel(u_ref, tt_ref, hp_ref, wout_ref, y_ref):
    for i in range(GROUPS_PER_STEP):
        y_ref[i] = _dot(u_ref[i], tt_ref[i]) + _dot_nt(hp_ref[:, i * LANES:(i + 1) * LANES], wout_ref[i])


def ssm_mix(h_bf16, h0_re, h0_im, params, batch, seq, q):
    win, wout, tt, aqr, aqi = params
    gr, st, g = SSM_GROUPS, SSM_STATE, SSM_GROUP
    nk = seq // q
    n = nk * batch
    qg = q * g
    u = h_bf16.reshape(batch, nk, q, gr, g).transpose(3, 1, 0, 2, 4).reshape(gr, n, qg)
    gs = GROUPS_PER_STEP
    s = pl.pallas_call(
        _ssm_in_kernel,
        grid=(gr // gs,),
        in_specs=[pl.BlockSpec((gs, n, qg), lambda i: (i, 0, 0)),
                  pl.BlockSpec((gs, qg, LANES), lambda i: (i, 0, 0))],
        out_specs=pl.BlockSpec((n, gs * LANES), lambda i: (0, i)),
        out_shape=jax.ShapeDtypeStruct((n, gr * LANES), F32),
        compiler_params=_cparams("parallel"),
        name="ssm_in_q%d" % q,
    )(u, win)
    rows = batch * gr
    kc = min(nk, 32)
    tile = lambda a: jnp.broadcast_to(a[:, 0].reshape(1, gr, LANES), (batch, gr, LANES)).reshape(rows, LANES)
    h0 = jnp.concatenate([h0_re, h0_im], -1).reshape(rows, LANES)
    full = pl.BlockSpec((rows, LANES), lambda i: (0, 0))
    hp, hf = pl.pallas_call(
        _ssm_scan_kernel,
        grid=(nk // kc,),
        in_specs=[pl.BlockSpec((kc, rows, LANES), lambda i: (i, 0, 0)), full, full, full],
        out_specs=[pl.BlockSpec((kc, rows, LANES), lambda i: (i, 0, 0)), full],
        out_shape=[jax.ShapeDtypeStruct((nk, rows, LANES), BF16),
                   jax.ShapeDtypeStruct((rows, LANES), F32)],
        scratch_shapes=[pltpu.VMEM((rows, LANES), F32)],
        compiler_params=_cparams("arbitrary"),
        name="ssm_scan_q%d" % q,
    )(s.reshape(nk, rows, LANES), h0, tile(aqr), tile(aqi))
    y = pl.pallas_call(
        _ssm_out_kernel,
        grid=(gr // gs,),
        in_specs=[pl.BlockSpec((gs, n, qg), lambda i: (i, 0, 0)),
                  pl.BlockSpec((gs, qg, qg), lambda i: (i, 0, 0)),
                  pl.BlockSpec((n, gs * LANES), lambda i: (0, i)),
                  pl.BlockSpec((gs, qg, LANES), lambda i: (i, 0, 0))],
        out_specs=pl.BlockSpec((gs, n, qg), lambda i: (i, 0, 0)),
        out_shape=jax.ShapeDtypeStruct((gr, n, qg), F32),
        compiler_params=_cparams("parallel"),
        name="ssm_out_q%d" % q,
    )(u, tt, hp.reshape(n, gr * LANES), wout)
    y = y.reshape(gr, nk, batch, q, g).transpose(2, 1, 3, 0, 4).reshape(batch * seq, D_MODEL)
    hf = hf.reshape(batch, gr, LANES)
    return y, hf[..., :st], hf[..., st:]


SSM_TOKENS_BLK = 2048


GROUPS_PER_TILE = LANES // SSM_GROUP
N_LANE_TILES = D_MODEL // LANES


def _to_chunks_kernel(*refs, q):
    x_refs, (g_ref, u_ref, ys_scr, ut_scr) = refs[:N_LANE_TILES], refs[N_LANE_TILES:]
    tb = x_refs[0].shape[0]
    nkb = tb // q
    cpb = LANES // q
    ss = jnp.zeros((tb, 1), F32)
    for x_ref in x_refs:
        x = x_ref[...]
        ss = ss + jnp.sum(x * x, -1, keepdims=True)
    rinv = lax.rsqrt(ss * (1.0 / D_MODEL) + RMS_EPS)
    rr = lax.broadcasted_iota(jnp.int32, (LANES, LANES), 0)
    cc = lax.broadcasted_iota(jnp.int32, (LANES, LANES), 1)
    perm = ((rr % cpb) * q + rr // cpb == cc).astype(BF16)
    for c, x_ref in enumerate(x_refs):
        xb = (x_ref[...] * rinv * g_ref[:, c * LANES:(c + 1) * LANES]).astype(BF16)
        for j in range(tb // LANES):
            rows = _dot(perm, xb[j * LANES:(j + 1) * LANES, :])
            ys_scr[:, j * cpb:(j + 1) * cpb, :] = rows.reshape(q, cpb, LANES)
        for s in range(q):
            ut_scr[:, s * SSM_GROUP:(s + 1) * SSM_GROUP, :] = ys_scr[s].T.reshape(GROUPS_PER_TILE, SSM_GROUP, nkb)
        for grp in range(GROUPS_PER_TILE):
            u_ref[c * GROUPS_PER_TILE + grp] = ut_scr[grp].T.astype(BF16)


def _from_chunks_kernel(y_ref, o_ref, yt_scr, ys_scr, *, q):
    tb = o_ref.shape[0]
    nkb = tb // q
    cpb = LANES // q
    rr = lax.broadcasted_iota(jnp.int32, (LANES, LANES), 0)
    cc = lax.broadcasted_iota(jnp.int32, (LANES, LANES), 1)
    perm = ((cc % cpb) * q + cc // cpb == rr).astype(BF16)
    for c in range(N_LANE_TILES):
        for grp in range(GROUPS_PER_TILE):
            yt_scr[grp] = y_ref[c * GROUPS_PER_TILE + grp].T
        for t in range(q):
            ys_scr[t] = yt_scr[:, t * SSM_GROUP:(t + 1) * SSM_GROUP, :].reshape(LANES, nkb).T
        for j in range(tb // LANES):
            rows = ys_scr[:, j * cpb:(j + 1) * cpb, :].reshape(LANES, LANES)
            hi = rows.astype(BF16)
            lo = (rows - hi.astype(F32)).astype(BF16)
            o_ref[j * LANES:(j + 1) * LANES, c * LANES:(c + 1) * LANES] = _dot(perm, hi) + _dot(perm, lo)


def _ssm_group_kernel(u_ref, win_ref, wout_ref, tt_ref, ar_ref, ai_ref, y_ref, hf_ref, h_scr, *, nk):
    u = u_ref[...]
    h = _dot(u, win_ref[...])
    n = h.shape[0]
    k_idx = lax.broadcasted_iota(jnp.int32, (n, LANES), 0) % nk
    d = 1
    for j in range(SCAN_PASSES):
        if d >= nk:
            break
        hs = jnp.where(k_idx >= d, pltpu.roll(h, d, 0), 0.0)
        h = h + ar_ref[j:j + 1, :] * hs + ai_ref[j:j + 1, :] * pltpu.roll(hs, SSM_STATE, 1)
        d *= 2
    hp = jnp.where(k_idx >= 1, pltpu.roll(h, 1, 0), 0.0)
    y_ref[...] = _dot(u, tt_ref[...]) + _dot_nt(hp.astype(BF16), wout_ref[...])
    h_scr[...] = h
    hf_ref[...] = h_scr[pl.ds(nk - 1, n // nk, stride=nk), :]


def ssm_mix_prompt(x, g_mix, params, batch, seq, q):
    win, wout, tt, aqr, aqi = params
    gr, st = SSM_GROUPS, SSM_STATE
    nk = seq // q
    assert nk <= 2 ** SCAN_PASSES
    n = nk * batch
    qg = q * SSM_GROUP
    tb = SSM_TOKENS_BLK
    nkb = tb // q
    per_b = seq // tb
    tok = pl.BlockSpec((tb, D_MODEL), lambda b, i: (b * per_b + i, 0))
    chk = pl.BlockSpec((gr, nkb, qg), lambda b, i: (0, b * per_b + i, 0))
    lane_tiles = [pl.BlockSpec((tb, LANES), lambda b, i, c=c: (b * per_b + i, c)) for c in range(N_LANE_TILES)]
    u = pl.pallas_call(
        functools.partial(_to_chunks_kernel, q=q),
        grid=(batch, per_b),
        in_specs=lane_tiles + [pl.BlockSpec((1, D_MODEL), lambda b, i: (0, 0))],
        out_specs=chk,
        out_shape=jax.ShapeDtypeStruct((gr, n, qg), BF16),
        scratch_shapes=[pltpu.VMEM((q, nkb, LANES), F32), pltpu.VMEM((GROUPS_PER_TILE, qg, nkb), F32)],
        compiler_params=_cparams("parallel", "parallel"),
        name="ssm_to_chunks",
    )(*([x] * N_LANE_TILES), g_mix)
    per_g = lambda r, c: pl.BlockSpec((None, r, c), lambda i: (i, 0, 0))
    y, hf = pl.pallas_call(
        functools.partial(_ssm_group_kernel, nk=nk),
        grid=(gr,),
        in_specs=[per_g(n, qg), per_g(qg, LANES), per_g(qg, LANES), per_g(qg, qg),
                  per_g(SCAN_PASSES, LANES), per_g(SCAN_PASSES, LANES)],
        out_specs=[per_g(n, qg), per_g(batch, LANES)],
        out_shape=[jax.ShapeDtypeStruct((gr, n, qg), F32), jax.ShapeDtypeStruct((gr, batch, LANES), F32)],
        scratch_shapes=[pltpu.VMEM((n, LANES), F32)],
        compiler_params=_cparams("parallel"),
        name="ssm_group",
    )(u, win, wout, tt, aqr, aqi)
    y = pl.pallas_call(
        functools.partial(_from_chunks_kernel, q=q),
        grid=(batch, per_b),
        in_specs=[chk],
        out_specs=tok,
        out_shape=jax.ShapeDtypeStruct((batch * seq, D_MODEL), F32),
        scratch_shapes=[pltpu.VMEM((GROUPS_PER_TILE, qg, nkb), F32), pltpu.VMEM((q, nkb, LANES), F32)],
        compiler_params=_cparams("parallel", "parallel"),
        name="ssm_from_chunks",
    )(y)
    hf = jnp.swapaxes(hf, 0, 1)
    return y, hf[..., :st], hf[..., st:]


def _ssm_glu_value(x_ref, gm_ref, y_ref, d_ref, w_ref, b_ref):
    x = x_ref[...]
    y = y_ref[...] + d_ref[...] * _rms(x, gm_ref[...])
    gl = _gelu_tanh(y)
    z = _dot(gl.astype(BF16), w_ref[...]) + b_ref[...]
    return x + gl * _sigmoid(z)


def _ssm_glu_kernel(x_ref, gm_ref, y_ref, d_ref, w_ref, b_ref, o_ref):
    o_ref[...] = _ssm_glu_value(x_ref, gm_ref, y_ref, d_ref, w_ref, b_ref)


def _glu_mem_kernel(x_ref, gm_ref, y_ref, d_ref, w_ref, b_ref, g_ref, wq_ref, qg_ref, mk_ref, mv_ref, wo_ref, o_ref):
    x = _ssm_glu_value(x_ref, gm_ref, y_ref, d_ref, w_ref, b_ref)
    o_ref[...] = _mem_attn_value(x, g_ref, wq_ref, qg_ref, mk_ref, mv_ref, wo_ref)


def glu_mem_prompt(x, g_mix, y, d_skip, w_glu, layer_ssm, b_glu, mem_args, layer, batch, tm):
    t = x.shape[0]
    per_b = t // batch // tm
    g, w_q, q_gain, mkv, w_o = mem_args
    tok = pl.BlockSpec((tm, D_MODEL), lambda b, i: (b * per_b + i, 0))
    vec = pl.BlockSpec((1, D_MODEL), lambda b, i: (0, 0))
    wglu = pl.BlockSpec((None, D_MODEL, D_MODEL), lambda b, i: (layer_ssm, 0, 0), pipeline_mode=pl.Buffered(1))
    return pl.pallas_call(
        _glu_mem_kernel,
        grid=(batch, per_b),
        in_specs=[tok, vec, tok, vec, wglu, vec] + _mem_specs(layer),
        out_specs=tok,
        out_shape=jax.ShapeDtypeStruct((t, D_MODEL), F32),
        compiler_params=_cparams("parallel", "arbitrary"),
        name="glu_mem_prompt",
    )(x, g_mix, y, d_skip, w_glu, b_glu, g, w_q, q_gain, mkv, mkv, w_o)


def ssm_glu(x, g_mix, y, d_skip, w_glu, layer, b_glu, tm):
    t = x.shape[0]
    row = pl.BlockSpec((tm, D_MODEL), lambda i: (i, 0))
    vec = pl.BlockSpec((1, D_MODEL), lambda i: (0, 0))
    return pl.pallas_call(
        _ssm_glu_kernel,
        grid=(t // tm,),
        in_specs=[row, vec, row, vec,
                  pl.BlockSpec((None, D_MODEL, D_MODEL), lambda i: (layer, 0, 0), pipeline_mode=pl.Buffered(1)), vec],
        out_specs=row,
        out_shape=jax.ShapeDtypeStruct((t, D_MODEL), F32),
        compiler_params=_cparams("parallel"),
        name="ssm_glu",
    )(x, g_mix, y, d_skip, w_glu, b_glu)


def _row(v):
    return v.reshape(1, -1).astype(F32)


def kernel(x_prompt, x_sample, mem_prompt, state_ssm_re, state_ssm_im, cache_swa0_k, cache_swa0_v, cache_swa1_k, cache_swa1_v, cache_swa2_k, cache_swa2_v, cache_mem_k, cache_mem_v, norm_mix_g, norm_mem_g, norm_memin_g, norm_ffn_g, ssm_lambda_re, ssm_lambda_im, ssm_b_re, ssm_b_im, ssm_c_re, ssm_c_im, ssm_d, ssm_log_step, ssm_w_glu, ssm_b_glu, attn_w_qkv, attn_q_norm_g, attn_k_norm_g, attn_w_o, mem_w_q, mem_w_kv, mem_q_norm_g, mem_k_norm_g, mem_w_o, ffn_w_in, ffn_w_out):
    pb, seq, _ = x_prompt.shape
    sb, dec, _ = x_sample.shape
    n_ssm = state_ssm_re.shape[0]
    n_attn = cache_swa0_k.shape[0]
    tm_p = 512
    tm_s = sb * SAMPLE_ROWS

    w_glu = ssm_w_glu.astype(BF16)
    w_qkv = attn_w_qkv.astype(BF16)
    w_ao = attn_w_o.astype(BF16)
    w_mq = mem_w_q.astype(BF16)
    w_mkv = mem_w_kv.astype(BF16)
    w_mo = mem_w_o.astype(BF16)
    w_fi = ffn_w_in.astype(BF16)
    w_fo = ffn_w_out.astype(BF16)

    head_gain = [jnp.concatenate([jnp.tile(attn_q_norm_g[j], (1, SWA_HEADS)),
                                  jnp.tile(attn_k_norm_g[j], (1, SWA_HEADS))], 1).reshape(N_SWA, 1, 2 * SWA_WIDTH)
                 for j in range(n_attn)]
    mem_q_gain = [_row(jnp.tile(mem_q_norm_g[i], MEM_HEADS)) for i in range(DEPTH)]
    mem_k_gain = jnp.tile(mem_k_norm_g, (1, MEM_HEADS)).reshape(DEPTH, 1, D_MODEL)

    mkv_b, p_mem_k, p_mem_v = mem_kv(mem_prompt.reshape(pb * N_MEM, D_MODEL),
                                     norm_memin_g.reshape(DEPTH, 1, D_MODEL), w_mkv, mem_k_gain)
    p_mem_k = p_mem_k.reshape(DEPTH, pb, N_MEM, MEM_HEADS, MEM_HEAD_DIM)
    p_mem_v = p_mem_v.reshape(DEPTH, pb, N_MEM, MEM_HEADS, MEM_HEAD_DIM)

    ssm_p = [[ssm_params(ssm_lambda_re[j], ssm_lambda_im[j], ssm_log_step[j], ssm_b_re[j], ssm_b_im[j],
                         ssm_c_re[j], ssm_c_im[j], q) for q in (SSM_Q_PROMPT, dec)] for j in range(n_ssm)]

    rows_minor = lambda c: jnp.transpose(c, (0, 1, 3, 4, 2))
    caches_k = tuple(rows_minor(c) for c in (cache_swa0_k, cache_swa1_k, cache_swa2_k))
    caches_v = tuple(rows_minor(c) for c in (cache_swa0_v, cache_swa1_v, cache_swa2_v))
    dils = tuple(d for _, d in SWA_GROUPS)
    assert all(min(w, seq) == SWA_KEYS_BACK * d for w, d in SWA_GROUPS)

    xp = x_prompt.reshape(pb * seq, D_MODEL)
    xs = jnp.pad(x_sample, ((0, 0), (0, SAMPLE_ROWS - dec), (0, 0))).reshape(tm_s, D_MODEL)

    p_ssm_re, p_ssm_im, s_ssm_re, s_ssm_im = [], [], [], []
    p_qkv = [[] for _ in SWA_GROUPS]
    s_swa_k = [[] for _ in SWA_GROUPS]
    s_swa_v = [[] for _ in SWA_GROUPS]

    for i in range(DEPTH):
        j = i // 2
        g_mix = _row(norm_mix_g[i])
        g_mem = _row(norm_mem_g[i])
        mem_args = (g_mem, w_mq, mem_q_gain[i], mkv_b, w_mo)
        if i % 2 == 0:
            d_skip, b_glu = _row(ssm_d[j]), _row(ssm_b_glu[j])
            y, fr, fi = ssm_mix_prompt(xp, g_mix, ssm_p[j][0], pb, seq, SSM_Q_PROMPT)
            p_ssm_re.append(fr)
            p_ssm_im.append(fi)
            xp = glu_mem_prompt(xp, g_mix, y, d_skip, w_glu, j, b_glu, mem_args, i, pb, tm_p)
            hb = norm_cast(xs, g_mix, tm_s).reshape(sb, SAMPLE_ROWS, D_MODEL)[:, :dec].reshape(sb * dec, D_MODEL)
            y, fr, fi = ssm_mix(hb, state_ssm_re[j], state_ssm_im[j], ssm_p[j][1], sb, dec, dec)
            s_ssm_re.append(fr)
            s_ssm_im.append(fi)
            y = jnp.pad(y.reshape(sb, dec, D_MODEL), ((0, 0), (0, SAMPLE_ROWS - dec), (0, 0))).reshape(tm_s, D_MODEL)
            xs = ssm_glu(xs, g_mix, y, d_skip, w_glu, j, b_glu, tm_s)
        else:
            qkvs = qkv_proj(xp, g_mix, w_qkv, j, head_gain[j], pb, dils, tm_p)
            outs, lses = [], []
            for g in range(N_SWA):
                p_qkv[g].append(qkvs[g])
                o, lse = swa_prompt(qkvs[g], g)
                outs.append(o)
                lses.append(lse)
            xp = attn_mem_prompt(xp, outs, lses, w_ao, mem_args, j, i, pb, tm_p)
            qkvs = qkv_proj(xs, g_mix, w_qkv, j, head_gain[j], 1, (1,) * N_SWA, tm_s)
            outs, lses = [], []
            for g in range(N_SWA):
                qkv3 = qkvs[g].reshape(sb, SAMPLE_ROWS, 3 * SWA_WIDTH)
                qkv5 = qkv3.reshape(sb, SAMPLE_ROWS, 3, SWA_HEADS, SWA_HEAD_DIM)
                s_swa_k[g].append(qkv5[:, :dec, 1])
                s_swa_v[g].append(qkv5[:, :dec, 2])
                o, lse = swa_sample(qkv3, caches_k[g], caches_v[g], j, g, dec)
                outs.append(o.reshape(1, 1, tm_s, SWA_WIDTH))
                lses.append(lse.reshape(1, 1, tm_s, LANES))
            xs = attn_out(xs, outs, lses, w_ao, j, 1, tm_s)

        xs = mem_attn_sample(xs, g_mem, w_mq, mem_q_gain[i], cache_mem_k, cache_mem_v, i, w_mo, sb)
        g_ffn = _row(norm_ffn_g[i])
        xp = swiglu_block(xp, g_ffn, w_fi, w_fo, i, 1024)
        xs = swiglu_block(xs, g_ffn, w_fi, w_fo, i, tm_s)

    rows_major = lambda c: jnp.transpose(c, (0, 1, 4, 2, 3))
    p_swa = [[rows_major(c) for c in swa_last_rows(p_qkv[g])] for g in range(N_SWA)]
    y_prompt = xp.reshape(pb, seq, D_MODEL)
    y_sample = xs.reshape(sb, SAMPLE_ROWS, D_MODEL)[:, :dec]
    st = lambda a: jnp.stack(a, 0)
    return (y_prompt, y_sample,
            st(p_ssm_re), st(p_ssm_im),
            p_swa[0][0], p_swa[0][1], p_swa[1][0], p_swa[1][1], p_swa[2][0], p_swa[2][1],
            p_mem_k, p_mem_v,
            st(s_ssm_re), st(s_ssm_im),
            st(s_swa_k[0]), st(s_swa_v[0]), st(s_swa_k[1]), st(s_swa_v[1]), st(s_swa_k[2]), st(s_swa_v[2]))
```

```python
import functools
import math

import jax
import jax.numpy as jnp
from jax import lax
from jax.experimental import pallas as pl
from jax.experimental.pallas import tpu as pltpu

F32 = jnp.float32
BF16 = jnp.bfloat16

D_MODEL = 1024
DEPTH = 4
SSM_GROUP = 16
SSM_GROUPS = D_MODEL // SSM_GROUP
SSM_STATE = 64
SWA_GROUPS = ((128, 1), (512, 4), (2048, 16))
N_SWA = len(SWA_GROUPS)
SWA_HEADS = 8
SWA_HEAD_DIM = 64
SWA_WIDTH = SWA_HEADS * SWA_HEAD_DIM
QKV_WIDTH = 3 * N_SWA * SWA_WIDTH
SWA_KEYS_BACK = 128
N_MEM = 256
MEM_HEADS = 4
MEM_HEAD_DIM = D_MODEL // MEM_HEADS
D_FF = -(-8 * D_MODEL // (3 * 256)) * 256
RMS_EPS = 1e-6
NEG_BIG = -1e30

LANES = 128
SUBLANES = 8
VMEM_LIMIT = 48 * 1024 * 1024
Q_BLK = 128
FF_BLK = 256
SAMPLE_ROWS = SUBLANES
SSM_Q_PROMPT = 16


def _cparams(*sem):
    return pltpu.CompilerParams(dimension_semantics=sem, vmem_limit_bytes=VMEM_LIMIT)


def _dot(a, b):
    return jnp.dot(a, b, preferred_element_type=F32)


def _dot_nt(a, b, precision=None):
    return lax.dot_general(a, b, (((1,), (1,)), ((), ())), preferred_element_type=F32,
                           precision=precision)


def _rms(x, g):
    return x * lax.rsqrt(jnp.mean(x * x, -1, keepdims=True) + RMS_EPS) * g


def _headnorm(y, g_row, hd):
    tm, n = y.shape
    outs = []
    if hd % LANES == 0:
        for c in range(n // hd):
            yc = y[:, c * hd:(c + 1) * hd]
            ms = jnp.mean(yc * yc, -1, keepdims=True)
            outs.append(yc * lax.rsqrt(ms + RMS_EPS))
    else:
        assert 2 * hd == LANES
        lo = lax.broadcasted_iota(jnp.int32, (tm, LANES), 1) < hd
        for c in range(n // LANES):
            yc = y[:, c * LANES:(c + 1) * LANES]
            sq = yc * yc
            s_lo = jnp.sum(jnp.where(lo, sq, 0.0), -1, keepdims=True)
            s_hi = jnp.sum(jnp.where(lo, 0.0, sq), -1, keepdims=True)
            ms = jnp.where(lo, s_lo, s_hi) * (1.0 / hd)
            outs.append(yc * lax.rsqrt(ms + RMS_EPS))
    return jnp.concatenate(outs, -1) * g_row


def _sigmoid(x):
    return 1.0 / (1.0 + jnp.exp(-x))


def _gelu_tanh(x):
    return 0.5 * x * (1.0 + jnp.tanh(math.sqrt(2.0 / math.pi) * (x + 0.044715 * (x * x * x))))


def _norm_cast_kernel(x_ref, g_ref, o_ref):
    o_ref[...] = _rms(x_ref[...], g_ref[...]).astype(BF16)


def norm_cast(x, g, tm):
    t = x.shape[0]
    return pl.pallas_call(
        _norm_cast_kernel,
        grid=(t // tm,),
        in_specs=[pl.BlockSpec((tm, D_MODEL), lambda i: (i, 0)),
                  pl.BlockSpec((1, D_MODEL), lambda i: (0, 0))],
        out_specs=pl.BlockSpec((tm, D_MODEL), lambda i: (i, 0)),
        out_shape=jax.ShapeDtypeStruct((t, D_MODEL), BF16),
        compiler_params=_cparams("parallel"),
        name="norm_cast",
    )(x, g)


def _qkv_kernel(x_ref, g_ref, w_ref, hg_ref, o0_ref, o1_ref, o2_ref, y_ref, *, dils):
    tm = x_ref.shape[0]
    w = SWA_WIDTH
    xn = _rms(x_ref[...], g_ref[...]).astype(BF16)
    for grp, (o_ref, dil) in enumerate(zip((o0_ref, o1_ref, o2_ref), dils)):
        col = lambda kind: slice((kind * N_SWA + grp) * w, (kind * N_SWA + grp + 1) * w)
        parts = (_headnorm(_dot(xn, w_ref[:, col(0)]), hg_ref[grp, :, :w], SWA_HEAD_DIM),
                 _headnorm(_dot(xn, w_ref[:, col(1)]), hg_ref[grp, :, w:], SWA_HEAD_DIM),
                 _dot(xn, w_ref[:, col(2)]))
        if dil == 1:
            for kind, y in enumerate(parts):
                o_ref[0, :, kind * w:(kind + 1) * w] = y
        else:
            per = w // LANES
            for c in range(y_ref.shape[0]):
                y_ref[c] = parts[c // per][:, (c % per) * LANES:(c % per + 1) * LANES]
                for r in range(dil):
                    o_ref[r, :, c * LANES:(c + 1) * LANES] = y_ref[c, pl.ds(r, tm // dil, stride=dil), :]


def qkv_proj(x, g, w, layer, head_gain, batch, dils, tm):
    t = x.shape[0]
    seq = t // batch
    per_b = seq // tm
    gw = 3 * SWA_WIDTH
    return pl.pallas_call(
        functools.partial(_qkv_kernel, dils=dils),
        grid=(t // tm,),
        in_specs=[pl.BlockSpec((tm, D_MODEL), lambda i: (i, 0)),
                  pl.BlockSpec((1, D_MODEL), lambda i: (0, 0)),
                  pl.BlockSpec((None, D_MODEL, QKV_WIDTH), lambda i: (layer, 0, 0), pipeline_mode=pl.Buffered(1)),
                  pl.BlockSpec((N_SWA, 1, 2 * SWA_WIDTH), lambda i: (0, 0, 0))],
        out_specs=[pl.BlockSpec((None, d, tm // d, gw), lambda i: (i // per_b, 0, i % per_b, 0))
                   for d in dils],
        out_shape=[jax.ShapeDtypeStruct((batch, d, seq // d, gw), F32) for d in dils],
        scratch_shapes=[pltpu.VMEM((gw // LANES, tm, LANES), F32)],
        compiler_params=_cparams("parallel"),
        name="qkv_proj",
    )(x, g, w, head_gain)


def _mem_kv_kernel(x_ref, g_ref, w_ref, kg_ref, o_ref, ko_ref, vo_ref):
    j = pl.program_id(1)
    y = _dot(_rms(x_ref[...], g_ref[...]).astype(BF16), w_ref[...])

    def emit(val, by_head_ref):
        o_ref[...] = val
        for h in range(MEM_HEADS):
            by_head_ref[:, h, :] = val[:, h * MEM_HEAD_DIM:(h + 1) * MEM_HEAD_DIM]

    @pl.when(j == 0)
    def _():
        emit(_headnorm(y, kg_ref[...], MEM_HEAD_DIM), ko_ref)

    @pl.when(j == 1)
    def _():
        emit(y, vo_ref)


def mem_kv(mem, g_in, w_kv, k_gain):
    t = mem.shape[0]
    by_head = pl.BlockSpec((None, t, MEM_HEADS, MEM_HEAD_DIM), lambda l, j: (l, 0, 0, 0))
    by_head_shape = jax.ShapeDtypeStruct((DEPTH, t, MEM_HEADS, MEM_HEAD_DIM), F32)
    return pl.pallas_call(
        _mem_kv_kernel,
        grid=(DEPTH, 2),
        in_specs=[pl.BlockSpec((t, D_MODEL), lambda l, j: (0, 0)),
                  pl.BlockSpec((None, 1, D_MODEL), lambda l, j: (l, 0, 0)),
                  pl.BlockSpec((None, D_MODEL, D_MODEL), lambda l, j: (l, 0, j)),
                  pl.BlockSpec((None, 1, D_MODEL), lambda l, j: (l, 0, 0))],
        out_specs=[pl.BlockSpec((None, t, D_MODEL), lambda l, j: (l, 0, j)), by_head, by_head],
        out_shape=[jax.ShapeDtypeStruct((DEPTH, t, 2 * D_MODEL), F32), by_head_shape, by_head_shape],
        compiler_params=_cparams("parallel", "arbitrary"),
        name="mem_kv",
    )(mem, g_in, w_kv, k_gain)


def _swiglu_kernel(x_ref, g_ref, wi_ref, wo_ref, o_ref):
    x = x_ref[...]
    xn = _rms(x, g_ref[...]).astype(BF16)
    acc = x
    for lo in range(0, D_FF, FF_BLK):
        hi = min(lo + FF_BLK, D_FF)
        gate = _dot(xn, wi_ref[:, lo:hi])
        up = _dot(xn, wi_ref[:, D_FF + lo:D_FF + hi])
        act = gate * _sigmoid(gate) * up
        acc = acc + _dot(act.astype(BF16), wo_ref[lo:hi, :])
    o_ref[...] = acc


def swiglu_block(x, g, w_in, w_out, layer, tm):
    t = x.shape[0]
    once = pl.Buffered(1)
    return pl.pallas_call(
        _swiglu_kernel,
        grid=(t // tm,),
        in_specs=[pl.BlockSpec((tm, D_MODEL), lambda i: (i, 0)),
                  pl.BlockSpec((1, D_MODEL), lambda i: (0, 0)),
                  pl.BlockSpec((None, D_MODEL, 2 * D_FF), lambda i: (layer, 0, 0), pipeline_mode=once),
                  pl.BlockSpec((None, D_FF, D_MODEL), lambda i: (layer, 0, 0), pipeline_mode=once)],
        out_specs=pl.BlockSpec((tm, D_MODEL), lambda i: (i, 0)),
        out_shape=jax.ShapeDtypeStruct((t, D_MODEL), F32),
        compiler_params=_cparams("parallel"),
        name="swiglu",
    )(x, g, w_in, w_out)


def _mem_heads(q, key_head, value_head):
    outs = []
    for h in range(MEM_HEADS):
        sl = slice(h * MEM_HEAD_DIM, (h + 1) * MEM_HEAD_DIM)
        s = _dot_nt(q[:, sl].astype(BF16), key_head(h))
        m = jnp.max(s, -1, keepdims=True)
        p = jnp.exp(s - m)
        l = jnp.sum(p, -1, keepdims=True)
        outs.append(_dot(p.astype(BF16), value_head(h)) / l)
    return jnp.concatenate(outs, -1)


def _mem_attn_value(x, g_ref, wq_ref, qg_ref, mk_ref, mv_ref, wo_ref):
    q = _dot(_rms(x, g_ref[...]).astype(BF16), wq_ref[...])
    q = _headnorm(q, qg_ref[...], MEM_HEAD_DIM) * (MEM_HEAD_DIM ** -0.5)
    head = lambda ref: lambda h: ref[:, h * MEM_HEAD_DIM:(h + 1) * MEM_HEAD_DIM].astype(BF16)
    o = _mem_heads(q, head(mk_ref), head(mv_ref))
    return x + _dot(o.astype(BF16), wo_ref[...])


def _mem_specs(layer):
    wspec = pl.BlockSpec((None, D_MODEL, D_MODEL), lambda b, i: (layer, 0, 0), pipeline_mode=pl.Buffered(1))
    vec = pl.BlockSpec((1, D_MODEL), lambda b, i: (0, 0))
    return [vec, wspec, vec,
            pl.BlockSpec((None, N_MEM, D_MODEL), lambda b, i: (layer, b, 0)),
            pl.BlockSpec((None, N_MEM, D_MODEL), lambda b, i: (layer, b, 1)),
            wspec]


def _mem_attn_sample_kernel(x_ref, g_ref, wq_ref, qg_ref, mk_ref, mv_ref, wo_ref, o_ref,
                            q_scr, o_scr):
    b = pl.program_id(0)

    @pl.when(b == 0)
    def _():
        q = _dot(_rms(x_ref[...], g_ref[...]).astype(BF16), wq_ref[...])
        q_scr[...] = _headnorm(q, qg_ref[...], MEM_HEAD_DIM) * (MEM_HEAD_DIM ** -0.5)

    hd, nr, nk = MEM_HEAD_DIM, MEM_HEADS * SAMPLE_ROWS, N_MEM * MEM_HEADS
    own = (lax.broadcasted_iota(jnp.int32, (nr, nk), 1) % MEM_HEADS
           == lax.broadcasted_iota(jnp.int32, (nr, nk), 0) // SAMPLE_ROWS)
    for bi in range(mk_ref.shape[0]):
        rows = pl.ds(pl.multiple_of((b * mk_ref.shape[0] + bi) * SAMPLE_ROWS, SAMPLE_ROWS), SAMPLE_ROWS)
        q = q_scr[rows, :]
        qx = jnp.concatenate([q[:, h * hd:(h + 1) * hd] for h in range(MEM_HEADS)], 0).astype(BF16)
        s = jnp.where(own, _dot_nt(qx, mk_ref[bi].reshape(nk, hd).astype(BF16)), NEG_BIG)
        p = jnp.exp(s - jnp.max(s, -1, keepdims=True))
        o = _dot(p.astype(BF16), mv_ref[bi].reshape(nk, hd).astype(BF16)) / jnp.sum(p, -1, keepdims=True)
        o_scr[rows, :] = jnp.concatenate([o[h * SAMPLE_ROWS:(h + 1) * SAMPLE_ROWS] for h in range(MEM_HEADS)], -1)

    @pl.when(b == pl.num_programs(0) - 1)
    def _():
        o_ref[...] = x_ref[...] + _dot(o_scr[...].astype(BF16), wo_ref[...])


def mem_attn_sample(x, g, w_q, q_gain, cache_k, cache_v, layer, w_o, batch):
    t = x.shape[0]
    const = lambda b: (0, 0)
    wspec = pl.BlockSpec((None, D_MODEL, D_MODEL), lambda b: (layer, 0, 0))
    per_step = 4
    cspec = pl.BlockSpec((None, per_step, N_MEM, MEM_HEADS, MEM_HEAD_DIM), lambda b: (layer, b, 0, 0, 0))
    return pl.pallas_call(
        _mem_attn_sample_kernel,
        grid=(batch // per_step,),
        in_specs=[pl.BlockSpec((t, D_MODEL), const),
                  pl.BlockSpec((1, D_MODEL), const),
                  wspec,
                  pl.BlockSpec((1, D_MODEL), const),
                  cspec, cspec, wspec],
        out_specs=pl.BlockSpec((t, D_MODEL), const),
        out_shape=jax.ShapeDtypeStruct((t, D_MODEL), F32),
        scratch_shapes=[pltpu.VMEM((t, D_MODEL), F32), pltpu.VMEM((t, D_MODEL), F32)],
        compiler_params=_cparams("arbitrary"),
        name="mem_attn_sample",
    )(x, g, w_q, q_gain, cache_k, cache_v, w_o)


MAX_Q_SUB = 8


def _swa_prompt_kernel(q_ref, kp_ref, kc_ref, vp_ref, vc_ref, o_ref, l_ref):
    for res in range(q_ref.shape[0]):
        _swa_prompt_rows(*[r.at[res] for r in (q_ref, kp_ref, kc_ref, vp_ref, vc_ref, o_ref, l_ref)])


def _swa_prompt_rows(q_ref, kp_ref, kc_ref, vp_ref, vc_ref, o_ref, l_ref):
    i = pl.program_id(2)
    q = (q_ref[...] * (SWA_HEAD_DIM ** -0.5)).astype(BF16)
    k = jnp.concatenate([kp_ref[...], kc_ref[...]], 0).astype(BF16)
    v = jnp.concatenate([vp_ref[...], vc_ref[...]], 0).astype(BF16)
    qq = lax.broadcasted_iota(jnp.int32, (Q_BLK, 2 * Q_BLK), 0)
    kk = lax.broadcasted_iota(jnp.int32, (Q_BLK, 2 * Q_BLK), 1)
    band = (kk >= qq) & (kk <= qq + SWA_KEYS_BACK)
    head_lane = lax.broadcasted_iota(jnp.int32, (Q_BLK, LANES), 1)
    head_of_lane = head_lane // SWA_HEAD_DIM
    pair_ids = range(LANES // SWA_HEAD_DIM)
    for sub in range(q_ref.shape[0] // Q_BLK):
        rows = slice(sub * Q_BLK, (sub + 1) * Q_BLK)
        win = slice(sub * Q_BLK, (sub + 2) * Q_BLK)
        valid = band if sub else band & ((kk >= Q_BLK) | (i > 0))
        outs = []
        lse = jnp.zeros((Q_BLK, LANES), F32)
        for c in range(SWA_WIDTH // LANES):
            tile = slice(c * LANES, (c + 1) * LANES)
            ql, kl, vl = q[rows, tile], k[win, tile], v[win, tile]
            pair = []
            for e in pair_ids:
                qh = jnp.where(head_of_lane == e, ql, jnp.zeros_like(ql))
                s = jnp.where(valid, _dot_nt(qh, kl), NEG_BIG)
                m = jnp.max(s, -1, keepdims=True)
                p = jnp.exp(s - m)
                l = jnp.sum(p, -1, keepdims=True)
                pair.append(_dot(p.astype(BF16), vl) / l)
                lse = jnp.where(head_lane == c * len(pair_ids) + e, m + jnp.log(l), lse)
            outs.append(jnp.where(head_of_lane == 0, pair[0], pair[1]))
        o_ref[rows, :] = jnp.concatenate(outs, -1)
        l_ref[rows, :] = lse


def swa_prompt(qkv, group):
    batch, dil, rows, _ = qkv.shape
    q_sub = min(MAX_Q_SUB, rows // Q_BLK)
    n_res = min(dil, MAX_Q_SUB // q_sub)
    blk = (None, n_res, q_sub * Q_BLK, SWA_WIDTH)
    pblk = (None, n_res, Q_BLK, SWA_WIDTH)
    prev = lambda i: jnp.maximum(i * q_sub - 1, 0)
    q_spec = pl.BlockSpec(blk, lambda b, r, i: (b, r, i, 0))
    kp_spec = pl.BlockSpec(pblk, lambda b, r, i: (b, r, prev(i), 1))
    kc_spec = pl.BlockSpec(blk, lambda b, r, i: (b, r, i, 1))
    vp_spec = pl.BlockSpec(pblk, lambda b, r, i: (b, r, prev(i), 2))
    vc_spec = pl.BlockSpec(blk, lambda b, r, i: (b, r, i, 2))
    l_spec = pl.BlockSpec((None, n_res, q_sub * Q_BLK, LANES), lambda b, r, i: (b, r, i, 0))
    return pl.pallas_call(
        _swa_prompt_kernel,
        grid=(batch, dil // n_res, rows // (q_sub * Q_BLK)),
        in_specs=[q_spec, kp_spec, kc_spec, vp_spec, vc_spec],
        out_specs=[q_spec, l_spec],
        out_shape=[jax.ShapeDtypeStruct((batch, dil, rows, SWA_WIDTH), F32),
                   jax.ShapeDtypeStruct((batch, dil, rows, LANES), F32)],
        compiler_params=_cparams("parallel", "parallel", "arbitrary"),
        name="swa_prompt_g%d" % group,
    )(qkv, qkv, qkv, qkv, qkv)


def _swa_rows_kernel(*refs, dil, n_layers):
    srcs, (ko_ref, vo_ref, buf) = refs[:2 * n_layers], refs[2 * n_layers:]
    layer = pl.program_id(0)
    pair = LANES // SWA_HEAD_DIM
    for li in range(n_layers):
        @pl.when(layer == li)
        def _(li=li):
            for src, dst in ((srcs[2 * li], ko_ref), (srcs[2 * li + 1], vo_ref)):
                for c in range(SWA_WIDTH // LANES):
                    for r in range(dil):
                        buf[pl.ds(r, src.shape[1], stride=dil), :] = src[r, :, c * LANES:(c + 1) * LANES]
                    dst[c * pair:(c + 1) * pair] = buf[...].T.reshape(pair, SWA_HEAD_DIM, buf.shape[0])


def swa_last_rows(qkvs):
    n_layers = len(qkvs)
    batch, dil, rows, _ = qkvs[0].shape
    keep = SWA_KEYS_BACK * dil
    pos = min(SWA_KEYS_BACK, 1024 // dil)
    nb = SWA_KEYS_BACK // pos
    first = (rows - SWA_KEYS_BACK) // pos
    in_specs, args = [], []
    for qkv in qkvs:
        for kind in (1, 2):
            in_specs.append(pl.BlockSpec((None, dil, pos, SWA_WIDTH),
                                         lambda l, b, a, kind=kind: (b, 0, first + a, kind)))
            args.append(qkv)
    o_spec = pl.BlockSpec((None, None, SWA_HEADS, SWA_HEAD_DIM, pos * dil), lambda l, b, a: (l, b, 0, 0, a))
    shape = jax.ShapeDtypeStruct((n_layers, batch, SWA_HEADS, SWA_HEAD_DIM, keep), F32)
    return pl.pallas_call(
        functools.partial(_swa_rows_kernel, dil=dil, n_layers=n_layers),
        grid=(n_layers, batch, nb),
        in_specs=in_specs,
        out_specs=[o_spec, o_spec],
        out_shape=[shape, shape],
        scratch_shapes=[pltpu.VMEM((pos * dil, LANES), F32)],
        compiler_params=_cparams("parallel", "parallel", "parallel"),
        name="swa_last_rows_d%d" % dil,
    )(*args)


def _swa_sample_kernel(q_ref, kn_ref, vn_ref, kt_ref, vt_ref, o_ref, l_ref, bias_c, bias_n, *, dil, n_new):
    nq, nh, hd = SAMPLE_ROWS, SWA_HEADS, SWA_HEAD_DIM
    nc = nh * nq
    rows = kt_ref.shape[-1]

    @pl.when(pl.program_id(0) == 0)
    def _():
        def bias(n, valid):
            i = lax.broadcasted_iota(jnp.int32, (nc, n), 0) % nq
            return jnp.where(valid(i, lax.broadcasted_iota(jnp.int32, (nc, n), 1)), 0.0, NEG_BIG)
        bias_c[...] = bias(rows, lambda i, c: (c >= i) & ((c - i) % dil == 0))
        bias_n[...] = bias(nq, lambda i, j: (j < n_new) & (j <= i) & ((i - j) % dil == 0))

    own = (lax.broadcasted_iota(jnp.int32, (nc, SWA_WIDTH), 1) // hd
           == lax.broadcasted_iota(jnp.int32, (nc, SWA_WIDTH), 0) // nq)
    head_lane = lax.broadcasted_iota(jnp.int32, (nq, LANES), 1)
    for b in range(q_ref.shape[0]):
        q = q_ref[b] * (hd ** -0.5)
        qbd = jnp.where(own, jnp.concatenate([q] * nh, 0), 0.0).astype(BF16)
        s_c = _dot(qbd, kt_ref[b].reshape(SWA_WIDTH, rows).astype(BF16)) + bias_c[...]
        s_n = _dot_nt(qbd, kn_ref[b].astype(BF16)) + bias_n[...]
        m = jnp.maximum(jnp.max(s_c, -1, keepdims=True), jnp.max(s_n, -1, keepdims=True))
        p_c, p_n = jnp.exp(s_c - m), jnp.exp(s_n - m)
        l = jnp.sum(p_c, -1, keepdims=True) + jnp.sum(p_n, -1, keepdims=True)
        acc = (_dot_nt(p_c.astype(BF16), vt_ref[b].reshape(SWA_WIDTH, rows).astype(BF16))
               + _dot(p_n.astype(BF16), vn_ref[b].astype(BF16)))
        o = acc / l
        o_ref[b] = jnp.concatenate([o[h * nq:(h + 1) * nq, h * hd:(h + 1) * hd] for h in range(nh)], -1)
        lse_col = m + jnp.log(l)
        lse = jnp.zeros((nq, LANES), F32)
        for h in range(nh):
            lse = jnp.where(head_lane == h, lse_col[h * nq:(h + 1) * nq], lse)
        l_ref[b] = lse


def swa_sample(qkv, cache_kt, cache_vt, layer, group, n_new):
    batch = qkv.shape[0]
    _, _, heads, hd, rows = cache_kt.shape
    dil = SWA_GROUPS[group][1]
    per_step = max(1, min(8, 2048 // rows))
    blk = (per_step, SAMPLE_ROWS, SWA_WIDTH)
    c_spec = pl.BlockSpec((None, per_step, heads, hd, rows), lambda b: (layer, b, 0, 0, 0))
    shape = jax.ShapeDtypeStruct((batch, SAMPLE_ROWS, SWA_WIDTH), F32)
    nc = SAMPLE_ROWS * heads
    return pl.pallas_call(
        functools.partial(_swa_sample_kernel, dil=dil, n_new=n_new),
        grid=(batch // per_step,),
        in_specs=[pl.BlockSpec(blk, lambda b: (b, 0, 0)),
                  pl.BlockSpec(blk, lambda b: (b, 0, 1)),
                  pl.BlockSpec(blk, lambda b: (b, 0, 2)),
                  c_spec, c_spec],
        out_specs=[pl.BlockSpec(blk, lambda b: (b, 0, 0)),
                   pl.BlockSpec((per_step, SAMPLE_ROWS, LANES), lambda b: (b, 0, 0))],
        out_shape=[shape, jax.ShapeDtypeStruct((batch, SAMPLE_ROWS, LANES), F32)],
        scratch_shapes=[pltpu.VMEM((nc, rows), F32), pltpu.VMEM((nc, SAMPLE_ROWS), F32)],
        compiler_params=_cparams("arbitrary"),
        name="swa_sample_g%d" % group,
    )(qkv, qkv, qkv, cache_kt, cache_vt)


def _attn_out_kernel(x_ref, o0, o1, o2, l0, l1, l2, w_ref, out_ref, *scr, dils):
    out_ref[...] = _attn_out_value(x_ref, o0, o1, o2, l0, l1, l2, w_ref, scr, dils)


def _attn_mem_kernel(x_ref, o0, o1, o2, l0, l1, l2, w_ref, g_ref, wq_ref, qg_ref, mk_ref, mv_ref, wo_ref,
                     out_ref, *scr, dils):
    x = _attn_out_value(x_ref, o0, o1, o2, l0, l1, l2, w_ref, scr, dils)
    out_ref[...] = _mem_attn_value(x, g_ref, wq_ref, qg_ref, mk_ref, mv_ref, wo_ref)


def _attn_out_value(x_ref, o0, o1, o2, l0, l1, l2, w_ref, scr, dils):
    tm = x_ref.shape[0]
    scr = list(scr)

    def rows(ref, dil):
        if dil == 1:
            return ref[0]
        buf = scr.pop()
        for c in range(buf.shape[0]):
            for r in range(dil):
                buf[c, pl.ds(r, tm // dil, stride=dil), :] = ref[r, :, c * LANES:(c + 1) * LANES]
        return jnp.concatenate([buf[c] for c in range(buf.shape[0])], -1)

    a, b, c = rows(l0, dils[0]), rows(l1, dils[1]), rows(l2, dils[2])
    m = jnp.maximum(jnp.maximum(a, b), c)
    ea, eb, ec = jnp.exp(a - m), jnp.exp(b - m), jnp.exp(c - m)
    den = ea + eb + ec
    spread = (lax.broadcasted_iota(jnp.int32, (LANES, SWA_WIDTH), 1) // SWA_HEAD_DIM
              == lax.broadcasted_iota(jnp.int32, (LANES, SWA_WIDTH), 0)).astype(BF16)

    def per_lane(wgt):
        hi = wgt.astype(BF16)
        lo = (wgt - hi.astype(F32)).astype(BF16)
        return _dot(hi, spread) + _dot(lo, spread)

    o = (per_lane(ea / den) * rows(o0, dils[0]) + per_lane(eb / den) * rows(o1, dils[1])
         + per_lane(ec / den) * rows(o2, dils[2]))
    return x_ref[...] + _dot(o.astype(BF16), w_ref[...])


def attn_mem_prompt(x, outs, lses, w_ao, mem_args, layer_attn, layer, batch, tm):
    t = x.shape[0]
    per_b = t // batch // tm
    g, w_q, q_gain, mkv, w_o = mem_args
    dils = tuple(o.shape[1] for o in outs)
    spec = lambda d, w: pl.BlockSpec((None, d, tm // d, w), lambda b, i: (b, 0, i, 0))
    tok = pl.BlockSpec((tm, D_MODEL), lambda b, i: (b * per_b + i, 0))
    n_scr = sum(d > 1 for d in dils)
    return pl.pallas_call(
        functools.partial(_attn_mem_kernel, dils=dils),
        grid=(batch, per_b),
        in_specs=[tok] + [spec(d, SWA_WIDTH) for d in dils] + [spec(d, LANES) for d in dils]
                 + [pl.BlockSpec((None, SWA_WIDTH, D_MODEL), lambda b, i: (layer_attn, 0, 0), pipeline_mode=pl.Buffered(1))]
                 + _mem_specs(layer),
        out_specs=tok,
        out_shape=jax.ShapeDtypeStruct((t, D_MODEL), F32),
        scratch_shapes=[pltpu.VMEM((SWA_WIDTH // LANES, tm, LANES), F32)] * n_scr
                       + [pltpu.VMEM((1, tm, LANES), F32)] * n_scr,
        compiler_params=_cparams("parallel", "arbitrary"),
        name="attn_mem_prompt",
    )(x, *outs, *lses, w_ao, g, w_q, q_gain, mkv, mkv, w_o)


def attn_out(x, outs, lses, w_o, layer, batch, tm):
    t = x.shape[0]
    per_b = t // batch // tm
    dils = tuple(o.shape[1] for o in outs)
    spec = lambda d, w: pl.BlockSpec((None, d, tm // d, w), lambda i: (i // per_b, 0, i % per_b, 0))
    n_scr = sum(d > 1 for d in dils)
    return pl.pallas_call(
        functools.partial(_attn_out_kernel, dils=dils),
        grid=(t // tm,),
        in_specs=[pl.BlockSpec((tm, D_MODEL), lambda i: (i, 0))]
                 + [spec(d, SWA_WIDTH) for d in dils] + [spec(d, LANES) for d in dils]
                 + [pl.BlockSpec((None, SWA_WIDTH, D_MODEL), lambda i: (layer, 0, 0), pipeline_mode=pl.Buffered(1))],
        out_specs=pl.BlockSpec((tm, D_MODEL), lambda i: (i, 0)),
        out_shape=jax.ShapeDtypeStruct((t, D_MODEL), F32),
        scratch_shapes=[pltpu.VMEM((SWA_WIDTH // LANES, tm, LANES), F32)] * n_scr
                       + [pltpu.VMEM((1, tm, LANES), F32)] * n_scr,
        compiler_params=_cparams("parallel"),
        name="attn_out",
    )(x, *outs, *lses, w_o)


SCAN_PASSES = 8
GROUPS_PER_STEP = 8


def _ssm_param_kernel(*refs, q):
    *per_group, e_scr = refs
    for i in range(GROUPS_PER_STEP):
        _ssm_param_group(*[r.at[i] for r in per_group], e_scr, q=q)


def _ssm_param_group(lr_ref, li_ref, ls_ref, btr_ref, bti_ref, cr_ref, ci_ref,
                     win_ref, wout_ref, tt_ref, aqr_ref, aqi_ref, e_scr, *, q):
    g = SSM_GROUP
    lr, li = lr_ref[...], li_ref[...]
    step = jnp.exp(ls_ref[...])
    lo = lax.broadcasted_iota(jnp.int32, (1, LANES), 1) < SSM_STATE

    def powers(tau):
        mag = jnp.exp(tau * (lr * step))
        ang = tau * (li * step)
        return mag * jnp.cos(ang), mag * jnp.sin(ang)

    tau = lax.broadcasted_iota(jnp.int32, (q + 1, LANES), 0).astype(F32)
    c2, s2 = powers(tau)
    ar, ai = c2[1:2], s2[1:2]
    den = lr * lr + li * li
    fr = ((ar - 1.0) * lr + ai * li) / den
    fi = (ai * lr - (ar - 1.0) * li) / den
    btr, bti = btr_ref[...], bti_ref[...]
    bbr = fr * btr - fi * bti
    bbi = fr * bti + fi * btr
    bba = jnp.where(lo, bbr, bbi)
    bbb = jnp.where(lo, -bbi, bbr)
    pa = jnp.where(lo, c2, -s2)
    pb = jnp.where(lo, s2, c2)
    cr, ci = cr_ref[...], ci_ref[...]
    for t in range(q + 1):
        e_scr[t * g:(t + 1) * g, :] = cr * pa[t:t + 1] - ci * pb[t:t + 1]
    wout_ref[...] = e_scr[g:(q + 1) * g, :].astype(BF16)
    kt = _dot_nt(bba, e_scr[0:q * g, :], precision=lax.Precision.HIGHEST)
    for s in range(q):
        blk = kt if s == 0 else jnp.concatenate([jnp.zeros((g, g * s), F32), kt[:, :(q - s) * g]], 1)
        tt_ref[s * g:(s + 1) * g, :] = blk.astype(BF16)
    c2r, s2r = powers((q - 1.0) - tau[0:q])
    for s in range(q):
        win_ref[s * g:(s + 1) * g, :] = (c2r[s:s + 1] * bba + s2r[s:s + 1] * bbb).astype(BF16)
    pw = lax.broadcasted_iota(jnp.int32, (SCAN_PASSES, LANES), 0)
    cq, sq = powers((q * jnp.left_shift(1, pw)).astype(F32))
    aqr_ref[...] = cq
    aqi_ref[...] = jnp.where(lo, -sq, sq)


def ssm_params(lam_re, lam_im, log_step, b_re, b_im, c_re, c_im, q):
    gr, st, g = SSM_GROUPS, SSM_STATE, SSM_GROUP
    dup = lambda a: jnp.concatenate([a, a], -1)
    lr = dup(lam_re).reshape(gr, 1, LANES)
    li = dup(lam_im).reshape(gr, 1, LANES)
    ls = jnp.broadcast_to(log_step[:, None, None], (gr, 1, LANES))
    btr = dup(jnp.swapaxes(b_re, 1, 2))
    bti = dup(jnp.swapaxes(b_im, 1, 2))
    cr, ci = dup(c_re), dup(c_im)
    gs = GROUPS_PER_STEP
    spec = lambda r, c: pl.BlockSpec((gs, r, c), lambda i: (i, 0, 0))
    vec, mat, big, pws = spec(1, LANES), spec(g, LANES), spec(q * g, LANES), spec(SCAN_PASSES, LANES)
    return pl.pallas_call(
        functools.partial(_ssm_param_kernel, q=q),
        grid=(gr // gs,),
        in_specs=[vec, vec, vec, mat, mat, mat, mat],
        out_specs=[big, big, spec(q * g, q * g), pws, pws],
        out_shape=[jax.ShapeDtypeStruct((gr, q * g, LANES), BF16),
                   jax.ShapeDtypeStruct((gr, q * g, LANES), BF16),
                   jax.ShapeDtypeStruct((gr, q * g, q * g), BF16),
                   jax.ShapeDtypeStruct((gr, SCAN_PASSES, LANES), F32),
                   jax.ShapeDtypeStruct((gr, SCAN_PASSES, LANES), F32)],
        scratch_shapes=[pltpu.VMEM(((q + 1) * g, LANES), F32)],
        compiler_params=_cparams("parallel"),
        name="ssm_params_q%d" % q,
    )(lr, li, ls, btr, bti, cr, ci)


def _ssm_in_kernel(u_ref, win_ref, s_ref):
    for i in range(GROUPS_PER_STEP):
        s_ref[:, i * LANES:(i + 1) * LANES] = _dot(u_ref[i], win_ref[i])


def _ssm_scan_kernel(s_ref, h0_ref, ar_ref, ai_ref, hp_ref, hf_ref, h_scr):
    @pl.when(pl.program_id(0) == 0)
    def _():
        h_scr[...] = h0_ref[...]

    def body(k, h):
        hp_ref[k] = h.astype(BF16)
        return ar_ref[...] * h + ai_ref[...] * pltpu.roll(h, SSM_STATE, 1) + s_ref[k]

    h = lax.fori_loop(0, s_ref.shape[0], body, h_scr[...])
    h_scr[...] = h
    hf_ref[...] = h


def _ssm_out_kernel(u_ref, tt_ref, hp_ref, wout_ref, y_ref):
    for i in range(GROUPS_PER_STEP):
        y_ref[i] = _dot(u_ref[i], tt_ref[i]) + _dot_nt(hp_ref[:, i * LANES:(i + 1) * LANES], wout_ref[i])


def ssm_mix(h_bf16, h0_re, h0_im, params, batch, seq, q):
    win, wout, tt, aqr, aqi = params
    gr, st, g = SSM_GROUPS, SSM_STATE, SSM_GROUP
    nk = seq // q
    n = nk * batch
    qg = q * g
    u = h_bf16.reshape(batch, nk, q, gr, g).transpose(3, 1, 0, 2, 4).reshape(gr, n, qg)
    gs = GROUPS_PER_STEP
    s = pl.pallas_call(
        _ssm_in_kernel,
        grid=(gr // gs,),
        in_specs=[pl.BlockSpec((gs, n, qg), lambda i: (i, 0, 0)),
                  pl.BlockSpec((gs, qg, LANES), lambda i: (i, 0, 0))],
        out_specs=pl.BlockSpec((n, gs * LANES), lambda i: (0, i)),
        out_shape=jax.ShapeDtypeStruct((n, gr * LANES), F32),
        compiler_params=_cparams("parallel"),
        name="ssm_in_q%d" % q,
    )(u, win)
    rows = batch * gr
    kc = min(nk, 32)
    tile = lambda a: jnp.broadcast_to(a[:, 0].reshape(1, gr, LANES), (batch, gr, LANES)).reshape(rows, LANES)
    h0 = jnp.concatenate([h0_re, h0_im], -1).reshape(rows, LANES)
    full = pl.BlockSpec((rows, LANES), lambda i: (0, 0))
    hp, hf = pl.pallas_call(
        _ssm_scan_kernel,
        grid=(nk // kc,),
        in_specs=[pl.BlockSpec((kc, rows, LANES), lambda i: (i, 0, 0)), full, full, full],
        out_specs=[pl.BlockSpec((kc, rows, LANES), lambda i: (i, 0, 0)), full],
        out_shape=[jax.ShapeDtypeStruct((nk, rows, LANES), BF16),
                   jax.ShapeDtypeStruct((rows, LANES), F32)],
        scratch_shapes=[pltpu.VMEM((rows, LANES), F32)],
        compiler_params=_cparams("arbitrary"),
        name="ssm_scan_q%d" % q,
    )(s.reshape(nk, rows, LANES), h0, tile(aqr), tile(aqi))
    y = pl.pallas_call(
        _ssm_out_kernel,
        grid=(gr // gs,),
        in_specs=[pl.BlockSpec((gs, n, qg), lambda i: (i, 0, 0)),
                  pl.BlockSpec((gs, qg, qg), lambda i: (i, 0, 0)),
                  pl.BlockSpec((n, gs * LANES), lambda i: (0, i)),
                  pl.BlockSpec((gs, qg, LANES), lambda i: (i, 0, 0))],
        out_specs=pl.BlockSpec((gs, n, qg), lambda i: (i, 0, 0)),
        out_shape=jax.ShapeDtypeStruct((gr, n, qg), F32),
        compiler_params=_cparams("parallel"),
        name="ssm_out_q%d" % q,
    )(u, tt, hp.reshape(n, gr * LANES), wout)
    y = y.reshape(gr, nk, batch, q, g).transpose(2, 1, 3, 0, 4).reshape(batch * seq, D_MODEL)
    hf = hf.reshape(batch, gr, LANES)
    return y, hf[..., :st], hf[..., st:]


SSM_TOKENS_BLK = 2048


GROUPS_PER_TILE = LANES // SSM_GROUP
N_LANE_TILES = D_MODEL // LANES


def _to_chunks_kernel(*refs, q):
    x_refs, (g_ref, u_ref, ys_scr, ut_scr) = refs[:N_LANE_TILES], refs[N_LANE_TILES:]
    tb = x_refs[0].shape[0]
    nkb = tb // q
    cpb = LANES // q
    ss = jnp.zeros((tb, 1), F32)
    for x_ref in x_refs:
        x = x_ref[...]
        ss = ss + jnp.sum(x * x, -1, keepdims=True)
    rinv = lax.rsqrt(ss * (1.0 / D_MODEL) + RMS_EPS)
    rr = lax.broadcasted_iota(jnp.int32, (LANES, LANES), 0)
    cc = lax.broadcasted_iota(jnp.int32, (LANES, LANES), 1)
    perm = ((rr % cpb) * q + rr // cpb == cc).astype(BF16)
    for c, x_ref in enumerate(x_refs):
        xb = (x_ref[...] * rinv * g_ref[:, c * LANES:(c + 1) * LANES]).astype(BF16)
        for j in range(tb // LANES):
            rows = _dot(perm, xb[j * LANES:(j + 1) * LANES, :])
            ys_scr[:, j * cpb:(j + 1) * cpb, :] = rows.reshape(q, cpb, LANES)
        for s in range(q):
            ut_scr[:, s * SSM_GROUP:(s + 1) * SSM_GROUP, :] = ys_scr[s].T.reshape(GROUPS_PER_TILE, SSM_GROUP, nkb)
        for grp in range(GROUPS_PER_TILE):
            u_ref[c * GROUPS_PER_TILE + grp] = ut_scr[grp].T.astype(BF16)


def _from_chunks_kernel(y_ref, o_ref, yt_scr, ys_scr, *, q):
    tb = o_ref.shape[0]
    nkb = tb // q
    cpb = LANES // q
    rr = lax.broadcasted_iota(jnp.int32, (LANES, LANES), 0)
    cc = lax.broadcasted_iota(jnp.int32, (LANES, LANES), 1)
    perm = ((cc % cpb) * q + cc // cpb == rr).astype(BF16)
    for c in range(N_LANE_TILES):
        for grp in range(GROUPS_PER_TILE):
            yt_scr[grp] = y_ref[c * GROUPS_PER_TILE + grp].T
        for t in range(q):
            ys_scr[t] = yt_scr[:, t * SSM_GROUP:(t + 1) * SSM_GROUP, :].reshape(LANES, nkb).T
        for j in range(tb // LANES):
            rows = ys_scr[:, j * cpb:(j + 1) * cpb, :].reshape(LANES, LANES)
            hi = rows.astype(BF16)
            lo = (rows - hi.astype(F32)).astype(BF16)
            o_ref[j * LANES:(j + 1) * LANES, c * LANES:(c + 1) * LANES] = _dot(perm, hi) + _dot(perm, lo)


def _ssm_group_kernel(u_ref, win_ref, wout_ref, tt_ref, ar_ref, ai_ref, y_ref, hf_ref, h_scr, *, nk):
    u = u_ref[...]
    h = _dot(u, win_ref[...])
    n = h.shape[0]
    k_idx = lax.broadcasted_iota(jnp.int32, (n, LANES), 0) % nk
    d = 1
    for j in range(SCAN_PASSES):
        if d >= nk:
            break
        hs = jnp.where(k_idx >= d, pltpu.roll(h, d, 0), 0.0)
        h = h + ar_ref[j:j + 1, :] * hs + ai_ref[j:j + 1, :] * pltpu.roll(hs, SSM_STATE, 1)
        d *= 2
    hp = jnp.where(k_idx >= 1, pltpu.roll(h, 1, 0), 0.0)
    y_ref[...] = _dot(u, tt_ref[...]) + _dot_nt(hp.astype(BF16), wout_ref[...])
    h_scr[...] = h
    hf_ref[...] = h_scr[pl.ds(nk - 1, n // nk, stride=nk), :]


def ssm_mix_prompt(x, g_mix, params, batch, seq, q):
    win, wout, tt, aqr, aqi = params
    gr, st = SSM_GROUPS, SSM_STATE
    nk = seq // q
    assert nk <= 2 ** SCAN_PASSES
    n = nk * batch
    qg = q * SSM_GROUP
    tb = SSM_TOKENS_BLK
    nkb = tb // q
    per_b = seq // tb
    tok = pl.BlockSpec((tb, D_MODEL), lambda b, i: (b * per_b + i, 0))
    chk = pl.BlockSpec((gr, nkb, qg), lambda b, i: (0, b * per_b + i, 0))
    lane_tiles = [pl.BlockSpec((tb, LANES), lambda b, i, c=c: (b * per_b + i, c)) for c in range(N_LANE_TILES)]
    u = pl.pallas_call(
        functools.partial(_to_chunks_kernel, q=q),
        grid=(batch, per_b),
        in_specs=lane_tiles + [pl.BlockSpec((1, D_MODEL), lambda b, i: (0, 0))],
        out_specs=chk,
        out_shape=jax.ShapeDtypeStruct((gr, n, qg), BF16),
        scratch_shapes=[pltpu.VMEM((q, nkb, LANES), F32), pltpu.VMEM((GROUPS_PER_TILE, qg, nkb), F32)],
        compiler_params=_cparams("parallel", "parallel"),
        name="ssm_to_chunks",
    )(*([x] * N_LANE_TILES), g_mix)
    per_g = lambda r, c: pl.BlockSpec((None, r, c), lambda i: (i, 0, 0))
    y, hf = pl.pallas_call(
        functools.partial(_ssm_group_kernel, nk=nk),
        grid=(gr,),
        in_specs=[per_g(n, qg), per_g(qg, LANES), per_g(qg, LANES), per_g(qg, qg),
                  per_g(SCAN_PASSES, LANES), per_g(SCAN_PASSES, LANES)],
        out_specs=[per_g(n, qg), per_g(batch, LANES)],
        out_shape=[jax.ShapeDtypeStruct((gr, n, qg), F32), jax.ShapeDtypeStruct((gr, batch, LANES), F32)],
        scratch_shapes=[pltpu.VMEM((n, LANES), F32)],
        compiler_params=_cparams("parallel"),
        name="ssm_group",
    )(u, win, wout, tt, aqr, aqi)
    y = pl.pallas_call(
        functools.partial(_from_chunks_kernel, q=q),
        grid=(batch, per_b),
        in_specs=[chk],
        out_specs=tok,
        out_shape=jax.ShapeDtypeStruct((batch * seq, D_MODEL), F32),
        scratch_shapes=[pltpu.VMEM((GROUPS_PER_TILE, qg, nkb), F32), pltpu.VMEM((q, nkb, LANES), F32)],
        compiler_params=_cparams("parallel", "parallel"),
        name="ssm_from_chunks",
    )(y)
    hf = jnp.swapaxes(hf, 0, 1)
    return y, hf[..., :st], hf[..., st:]


def _ssm_glu_value(x_ref, gm_ref, y_ref, d_ref, w_ref, b_ref):
    x = x_ref[...]
    y = y_ref[...] + d_ref[...] * _rms(x, gm_ref[...])
    gl = _gelu_tanh(y)
    z = _dot(gl.astype(BF16), w_ref[...]) + b_ref[...]
    return x + gl * _sigmoid(z)


def _ssm_glu_kernel(x_ref, gm_ref, y_ref, d_ref, w_ref, b_ref, o_ref):
    o_ref[...] = _ssm_glu_value(x_ref, gm_ref, y_ref, d_ref, w_ref, b_ref)


def _glu_mem_kernel(x_ref, gm_ref, y_ref, d_ref, w_ref, b_ref, g_ref, wq_ref, qg_ref, mk_ref, mv_ref, wo_ref, o_ref):
    x = _ssm_glu_value(x_ref, gm_ref, y_ref, d_ref, w_ref, b_ref)
    o_ref[...] = _mem_attn_value(x, g_ref, wq_ref, qg_ref, mk_ref, mv_ref, wo_ref)


def glu_mem_prompt(x, g_mix, y, d_skip, w_glu, layer_ssm, b_glu, mem_args, layer, batch, tm):
    t = x.shape[0]
    per_b = t // batch // tm
    g, w_q, q_gain, mkv, w_o = mem_args
    tok = pl.BlockSpec((tm, D_MODEL), lambda b, i: (b * per_b + i, 0))
    vec = pl.BlockSpec((1, D_MODEL), lambda b, i: (0, 0))
    wglu = pl.BlockSpec((None, D_MODEL, D_MODEL), lambda b, i: (layer_ssm, 0, 0), pipeline_mode=pl.Buffered(1))
    return pl.pallas_call(
        _glu_mem_kernel,
        grid=(batch, per_b),
        in_specs=[tok, vec, tok, vec, wglu, vec] + _mem_specs(layer),
        out_specs=tok,
        out_shape=jax.ShapeDtypeStruct((t, D_MODEL), F32),
        compiler_params=_cparams("parallel", "arbitrary"),
        name="glu_mem_prompt",
    )(x, g_mix, y, d_skip, w_glu, b_glu, g, w_q, q_gain, mkv, mkv, w_o)


def ssm_glu(x, g_mix, y, d_skip, w_glu, layer, b_glu, tm):
    t = x.shape[0]
    row = pl.BlockSpec((tm, D_MODEL), lambda i: (i, 0))
    vec = pl.BlockSpec((1, D_MODEL), lambda i: (0, 0))
    return pl.pallas_call(
        _ssm_glu_kernel,
        grid=(t // tm,),
        in_specs=[row, vec, row, vec,
                  pl.BlockSpec((None, D_MODEL, D_MODEL), lambda i: (layer, 0, 0), pipeline_mode=pl.Buffered(1)), vec],
        out_specs=row,
        out_shape=jax.ShapeDtypeStruct((t, D_MODEL), F32),
        compiler_params=_cparams("parallel"),
        name="ssm_glu",
    )(x, g_mix, y, d_skip, w_glu, b_glu)


def _row(v):
    return v.reshape(1, -1).astype(F32)


def kernel(x_prompt, x_sample, mem_prompt, state_ssm_re, state_ssm_im, cache_swa0_k, cache_swa0_v, cache_swa1_k, cache_swa1_v, cache_swa2_k, cache_swa2_v, cache_mem_k, cache_mem_v, norm_mix_g, norm_mem_g, norm_memin_g, norm_ffn_g, ssm_lambda_re, ssm_lambda_im, ssm_b_re, ssm_b_im, ssm_c_re, ssm_c_im, ssm_d, ssm_log_step, ssm_w_glu, ssm_b_glu, attn_w_qkv, attn_q_norm_g, attn_k_norm_g, attn_w_o, mem_w_q, mem_w_kv, mem_q_norm_g, mem_k_norm_g, mem_w_o, ffn_w_in, ffn_w_out):
    pb, seq, _ = x_prompt.shape
    sb, dec, _ = x_sample.shape
    n_ssm = state_ssm_re.shape[0]
    n_attn = cache_swa0_k.shape[0]
    tm_p = 512
    tm_s = sb * SAMPLE_ROWS

    w_glu = ssm_w_glu.astype(BF16)
    w_qkv = attn_w_qkv.astype(BF16)
    w_ao = attn_w_o.astype(BF16)
    w_mq = mem_w_q.astype(BF16)
    w_mkv = mem_w_kv.astype(BF16)
    w_mo = mem_w_o.astype(BF16)
    w_fi = ffn_w_in.astype(BF16)
    w_fo = ffn_w_out.astype(BF16)

    head_gain = [jnp.concatenate([jnp.tile(attn_q_norm_g[j], (1, SWA_HEADS)),
                                  jnp.tile(attn_k_norm_g[j], (1, SWA_HEADS))], 1).reshape(N_SWA, 1, 2 * SWA_WIDTH)
                 for j in range(n_attn)]
    mem_q_gain = [_row(jnp.tile(mem_q_norm_g[i], MEM_HEADS)) for i in range(DEPTH)]
    mem_k_gain = jnp.tile(mem_k_norm_g, (1, MEM_HEADS)).reshape(DEPTH, 1, D_MODEL)

    mkv_b, p_mem_k, p_mem_v = mem_kv(mem_prompt.reshape(pb * N_MEM, D_MODEL),
                                     norm_memin_g.reshape(DEPTH, 1, D_MODEL), w_mkv, mem_k_gain)
    p_mem_k = p_mem_k.reshape(DEPTH, pb, N_MEM, MEM_HEADS, MEM_HEAD_DIM)
    p_mem_v = p_mem_v.reshape(DEPTH, pb, N_MEM, MEM_HEADS, MEM_HEAD_DIM)

    ssm_p = [[ssm_params(ssm_lambda_re[j], ssm_lambda_im[j], ssm_log_step[j], ssm_b_re[j], ssm_b_im[j],
                         ssm_c_re[j], ssm_c_im[j], q) for q in (SSM_Q_PROMPT, dec)] for j in range(n_ssm)]

    rows_minor = lambda c: jnp.transpose(c, (0, 1, 3, 4, 2))
    caches_k = tuple(rows_minor(c) for c in (cache_swa0_k, cache_swa1_k, cache_swa2_k))
    caches_v = tuple(rows_minor(c) for c in (cache_swa0_v, cache_swa1_v, cache_swa2_v))
    dils = tuple(d for _, d in SWA_GROUPS)
    assert all(min(w, seq) == SWA_KEYS_BACK * d for w, d in SWA_GROUPS)

    xp = x_prompt.reshape(pb * seq, D_MODEL)
    xs = jnp.pad(x_sample, ((0, 0), (0, SAMPLE_ROWS - dec), (0, 0))).reshape(tm_s, D_MODEL)

    p_ssm_re, p_ssm_im, s_ssm_re, s_ssm_im = [], [], [], []
    p_qkv = [[] for _ in SWA_GROUPS]
    s_swa_k = [[] for _ in SWA_GROUPS]
    s_swa_v = [[] for _ in SWA_GROUPS]

    for i in range(DEPTH):
        j = i // 2
        g_mix = _row(norm_mix_g[i])
        g_mem = _row(norm_mem_g[i])
        mem_args = (g_mem, w_mq, mem_q_gain[i], mkv_b, w_mo)
        if i % 2 == 0:
            d_skip, b_glu = _row(ssm_d[j]), _row(ssm_b_glu[j])
            y, fr, fi = ssm_mix_prompt(xp, g_mix, ssm_p[j][0], pb, seq, SSM_Q_PROMPT)
            p_ssm_re.append(fr)
            p_ssm_im.append(fi)
            xp = glu_mem_prompt(xp, g_mix, y, d_skip, w_glu, j, b_glu, mem_args, i, pb, tm_p)
            hb = norm_cast(xs, g_mix, tm_s).reshape(sb, SAMPLE_ROWS, D_MODEL)[:, :dec].reshape(sb * dec, D_MODEL)
            y, fr, fi = ssm_mix(hb, state_ssm_re[j], state_ssm_im[j], ssm_p[j][1], sb, dec, dec)
            s_ssm_re.append(fr)
            s_ssm_im.append(fi)
            y = jnp.pad(y.reshape(sb, dec, D_MODEL), ((0, 0), (0, SAMPLE_ROWS - dec), (0, 0))).reshape(tm_s, D_MODEL)
            xs = ssm_glu(xs, g_mix, y, d_skip, w_glu, j, b_glu, tm_s)
        else:
            qkvs = qkv_proj(xp, g_mix, w_qkv, j, head_gain[j], pb, dils, tm_p)
            outs, lses = [], []
            for g in range(N_SWA):
                p_qkv[g].append(qkvs[g])
                o, lse = swa_prompt(qkvs[g], g)
                outs.append(o)
                lses.append(lse)
            xp = attn_mem_prompt(xp, outs, lses, w_ao, mem_args, j, i, pb, tm_p)
            qkvs = qkv_proj(xs, g_mix, w_qkv, j, head_gain[j], 1, (1,) * N_SWA, tm_s)
            outs, lses = [], []
            for g in range(N_SWA):
                qkv3 = qkvs[g].reshape(sb, SAMPLE_ROWS, 3 * SWA_WIDTH)
                qkv5 = qkv3.reshape(sb, SAMPLE_ROWS, 3, SWA_HEADS, SWA_HEAD_DIM)
                s_swa_k[g].append(qkv5[:, :dec, 1])
                s_swa_v[g].append(qkv5[:, :dec, 2])
                o, lse = swa_sample(qkv3, caches_k[g], caches_v[g], j, g, dec)
                outs.append(o.reshape(1, 1, tm_s, SWA_WIDTH))
                lses.append(lse.reshape(1, 1, tm_s, LANES))
            xs = attn_out(xs, outs, lses, w_ao, j, 1, tm_s)

        xs = mem_attn_sample(xs, g_mem, w_mq, mem_q_gain[i], cache_mem_k, cache_mem_v, i, w_mo, sb)
        g_ffn = _row(norm_ffn_g[i])
        xp = swiglu_block(xp, g_ffn, w_fi, w_fo, i, 1024)
        xs = swiglu_block(xs, g_ffn, w_fi, w_fo, i, tm_s)

    rows_major = lambda c: jnp.transpose(c, (0, 1, 4, 2, 3))
    p_swa = [[rows_major(c) for c in swa_last_rows(p_qkv[g])] for g in range(N_SWA)]
    y_prompt = xp.reshape(pb, seq, D_MODEL)
    y_sample = xs.reshape(sb, SAMPLE_ROWS, D_MODEL)[:, :dec]
    st = lambda a: jnp.stack(a, 0)
    return (y_prompt, y_sample,
            st(p_ssm_re), st(p_ssm_im),
            p_swa[0][0], p_swa[0][1], p_swa[1][0], p_swa[1][1], p_swa[2][0], p_swa[2][1],
            p_mem_k, p_mem_v,
            st(s_ssm_re), st(s_ssm_im),
            st(s_swa_k[0]), st(s_swa_v[0]), st(s_swa_k[1]), st(s_swa_v[1]), st(s_swa_k[2]), st(s_swa_v[2]))
```
